```python
import math
import jax, jax.numpy as jnp
from jax import lax
import numpy as np

D_MODEL = 1024
BATCH = 16
SEQ = 256
DEPTH = 4
DEC_BATCH = 2
DEC_SEQ = 4096
PAST_LEN = 512

GRID_W = 64
HEAD_DIM = 64
A_HEADS = 6
A_KV_HEADS = 2
A_GROUP = A_HEADS // A_KV_HEADS
A_WIDTH = A_HEADS * HEAD_DIM
WINDOW = 128
BLK = 128
B_HEADS = 4
B_QK_DIM = 32
B_V_DIM = 2 * B_QK_DIM
B_WIDTH = B_HEADS * B_V_DIM
C_HEADS = 6
C_HEAD_DIM = 64
C_WIDTH = C_HEADS * C_HEAD_DIM
C_W_RANK = 64
C_A_RANK = 64
C_G_RANK = 128
MIX_W = A_WIDTH + B_WIDTH + C_WIDTH
D_FF = -(-8 * D_MODEL // (3 * 256)) * 256
IN_SPLITS = (A_WIDTH, A_KV_HEADS * HEAD_DIM, A_KV_HEADS * HEAD_DIM,
             B_WIDTH, B_WIDTH, B_WIDTH,
             3 * C_WIDTH, 2 * C_W_RANK, 2 * C_A_RANK, C_G_RANK)
IN_COLS = sum(IN_SPLITS)
ROPE_THETA = 10000.0
NORM_EPS = 1e-6
GN_EPS = 64e-5
DECAY_SCALE = 0.606531
NEG_INF = -1e30

kernel_name = "hybrid_prefix_diffusion_step"


def rmsnorm(x, g):
    xf = x.astype(jnp.float32)
    y = xf * lax.rsqrt(jnp.mean(xf * xf, axis=-1, keepdims=True) + NORM_EPS)
    return (y * g.astype(jnp.float32)).astype(x.dtype)


def rope_1d(x, pos):
    d = x.shape[-1]
    inv = ROPE_THETA ** (-jnp.arange(0, d, 2, dtype=jnp.float32) / d)
    ang = pos.astype(jnp.float32)[:, None] * inv[None, :]
    ang = jnp.concatenate([ang, ang], axis=-1)
    shape = (x.shape[1],) + (1,) * (x.ndim - 3) + (d,)
    cos = jnp.cos(ang).reshape(shape).astype(x.dtype)
    sin = jnp.sin(ang).reshape(shape).astype(x.dtype)
    x1, x2 = jnp.split(x, 2, axis=-1)
    return x * cos + jnp.concatenate([-x2, x1], axis=-1) * sin


def axial_rope(x):
    n_rows = x.shape[1] // GRID_W
    rows = jnp.repeat(jnp.arange(n_rows), GRID_W)
    cols = jnp.tile(jnp.arange(GRID_W), n_rows)
    xr, xc = jnp.split(x, 2, axis=-1)
    return jnp.concatenate([rope_1d(xr, rows), rope_1d(xc, cols)], axis=-1)


def short_conv3(x, w):
    xp = jnp.pad(x, ((0, 0), (1, 1), (0, 0)))
    return xp[:, :-2] * w[0] + xp[:, 1:-1] * w[1] + xp[:, 2:] * w[2]


def adaln(cvec, w, b):
    m = jax.nn.silu(cvec) @ w + b
    return jnp.split(m[:, None, :], 6, axis=-1)


def swiglu(h, w1, w3, w2):
    return (jax.nn.silu(h @ w1) * (h @ w3)) @ w2


def to_blocks(q):
    b, t = q.shape[:2]
    return jnp.moveaxis(q.reshape((b, t // BLK, BLK) + q.shape[2:]), 1, 0)


def from_blocks(o):
    nb, b = o.shape[:2]
    return jnp.moveaxis(o, 0, 1).reshape((b, nb * BLK) + o.shape[3:])


def sink_attend(q, k, v, sink, mask):
    s = jnp.einsum('bqhgd,bshd->bhgqs', q, k, preferred_element_type=jnp.float32) * (HEAD_DIM ** -0.5)
    if mask is not None:
        s = jnp.where(mask, s, NEG_INF)
    sk = jnp.broadcast_to(sink.astype(jnp.float32)[None, :, :, None, None], s.shape[:-1] + (1,))
    p = jax.nn.softmax(jnp.concatenate([s, sk], axis=-1), axis=-1)[..., :-1]
    return jnp.einsum('bhgqs,bshd->bqhgd', p.astype(v.dtype), v)


def window_attn_context(q, k, v, sink):
    return from_blocks(lax.map(lambda qb: sink_attend(qb, k, v, sink, None), to_blocks(q)))


def window_attn_latent(q, k, v, kc, vc, sink):
    t = q.shape[1]
    c_len = kc.shape[1]
    pad = ((0, 0), (BLK, BLK), (0, 0), (0, 0))
    kp = jnp.pad(k, pad)
    vp = jnp.pad(v, pad)
    qi = jnp.arange(BLK)[:, None]
    kj = jnp.arange(3 * BLK)[None, :]
    ctx_mask = jnp.ones((BLK, c_len), dtype=bool)

    def blk(args):
        qb, b = args
        start = b * BLK
        kw = lax.dynamic_slice_in_dim(kp, start, 3 * BLK, axis=1)
        vw = lax.dynamic_slice_in_dim(vp, start, 3 * BLK, axis=1)
        pos = start - BLK + kj
        win = (jnp.abs(kj - BLK - qi) <= WINDOW) & (pos >= 0) & (pos < t)
        mask = jnp.concatenate([win, ctx_mask], axis=1)
        return sink_attend(qb, jnp.concatenate([kw, kc], axis=1), jnp.concatenate([vw, vc], axis=1), sink, mask)

    return from_blocks(lax.map(blk, (to_blocks(q), jnp.arange(t // BLK))))


def diff_lambda(lp, lam_init):
    lp = lp.astype(jnp.float32)
    return jnp.exp(jnp.sum(lp[0] * lp[1])) - jnp.exp(jnp.sum(lp[2] * lp[3])) + lam_init


def diff_attend(q, k, v, lam):
    def blk(qb):
        s = jnp.einsum('bqhmd,bshmd->bhmqs', qb, k, preferred_element_type=jnp.float32) * (B_QK_DIM ** -0.5)
        p = jax.nn.softmax(s, axis=-1)
        a = p[:, :, 0] - lam * p[:, :, 1]
        return jnp.einsum('bhqs,bshd->bqhd', a.astype(v.dtype), v)
    return from_blocks(lax.map(blk, to_blocks(q)))


def wkv_scan(s0, r, w, k, v, kk, a, reverse):
    xs = tuple(jnp.moveaxis(t, 1, 0) for t in (r, w, k, v, kk, a))

    def step(s, inp):
        rt, wt, kt, vt, kkt, at = inp
        sa = jnp.einsum('bhij,bhj->bhi', s, -kkt)
        s = s * wt[:, :, None, :] + sa[..., None] * (kkt * at)[:, :, None, :] + vt[..., None] * kt[:, :, None, :]
        return s, jnp.einsum('bhij,bhj->bhi', s, rt)

    s, y = lax.scan(step, s0, xs, reverse=reverse)
    return s, jnp.moveaxis(y, 0, 1)


def rwkv_mix(rkv, cw, ca, cg, p, s_f, s_b):
    b, t = rkv.shape[:2]
    f32 = jnp.float32
    hs = lambda z: z.reshape(z.shape[:-1] + (C_HEADS, C_HEAD_DIM))
    r, k, v = [hs(z.astype(f32)) for z in jnp.split(rkv, 3, axis=-1)]
    cw = cw.astype(f32).reshape(b, t, 2, C_W_RANK)
    ca = ca.astype(f32).reshape(b, t, 2, C_A_RANK)
    w = jnp.exp(-DECAY_SCALE * jax.nn.sigmoid(p['c_w0'] + jnp.einsum('btdr,drc->btdc', jnp.tanh(cw), p['c_w2'])))
    a = jax.nn.sigmoid(p['c_a0'] + jnp.einsum('btdr,drc->btdc', ca, p['c_a2']))
    w, a = hs(w), hs(a)
    kk = k * hs(p['c_kk']).astype(f32)
    kk = kk / jnp.maximum(jnp.sqrt(jnp.sum(kk * kk, axis=-1, keepdims=True)), 1e-12)
    kd = k[:, :, None] * (1.0 + (a - 1.0) * hs(p['c_ka']).astype(f32))
    s_f, y_f = wkv_scan(s_f, r, w[:, :, 0], kd[:, :, 0], v, kk, a[:, :, 0], False)
    s_b, y_b = wkv_scan(s_b, r, w[:, :, 1], kd[:, :, 1], v, kk, a[:, :, 1], True)
    bonus = jnp.sum(jnp.sum(r[:, :, None] * kd * p['c_rk'].astype(f32), axis=-1, keepdims=True) * v[:, :, None], axis=2)
    y = y_f + y_b + bonus
    mu = jnp.mean(y, axis=-1, keepdims=True)
    var = jnp.mean(jnp.square(y - mu), axis=-1, keepdims=True)
    y = ((y - mu) * lax.rsqrt(var + GN_EPS)).reshape(b, t, C_WIDTH) * p['c_lnx_g'] + p['c_lnx_b']
    g = jax.nn.sigmoid(cg) @ p['c_g2']
    return (y * g).astype(rkv.dtype), s_f, s_b


def project(h, p, latent):
    b, t = h.shape[:2]
    z = h @ p['w_in']
    idx = np.cumsum(IN_SPLITS)[:-1].tolist()
    aq, ak, av, bq, bk, bv, rkv, cw, ca, cg = jnp.split(z, idx, axis=-1)
    aq = aq.reshape(b, t, A_KV_HEADS, A_GROUP, HEAD_DIM)
    ak = ak.reshape(b, t, A_KV_HEADS, HEAD_DIM)
    av = av.reshape(b, t, A_KV_HEADS, HEAD_DIM)
    bq = bq.reshape(b, t, B_HEADS, 2, B_QK_DIM)
    bk = bk.reshape(b, t, B_HEADS, 2, B_QK_DIM)
    bv = bv.reshape(b, t, B_HEADS, B_V_DIM)
    if latent:
        aq, ak, bq, bk = axial_rope(aq), axial_rope(ak), axial_rope(bq), axial_rope(bk)
    rkv = short_conv3(rkv, p['c_conv'])
    return aq, ak, av, bq, bk, bv, rkv, cw, ca, cg


def merge_out(a_o, b_o, c_o, p, lam_init):
    b, t = c_o.shape[:2]
    b_o = rmsnorm(b_o, p['b_subln_g']) * (1.0 - lam_init)
    m = jnp.concatenate([a_o.reshape(b, t, A_WIDTH), b_o.reshape(b, t, B_WIDTH), c_o], axis=-1)
    return m @ p['w_out']


def mixers_context(h, p, l):
    aq, ak, av, bq, bk, bv, rkv, cw, ca, cg = project(h, p, False)
    lam_init = 0.8 - 0.6 * math.exp(-0.3 * l)
    lam = diff_lambda(p['b_lambda'], lam_init)
    a_o = window_attn_context(aq, ak, av, p['a_sink'].reshape(A_KV_HEADS, A_GROUP))
    b_o = diff_attend(bq, bk, bv, lam)
    s0 = jnp.zeros((h.shape[0], C_HEADS, C_HEAD_DIM, C_HEAD_DIM), jnp.float32)
    c_o, s_f, s_b = rwkv_mix(rkv, cw, ca, cg, p, s0, s0)
    return merge_out(a_o, b_o, c_o, p, lam_init), (ak, av, bk, bv, s_f, s_b)


def mixers_latent(h, p, l, ctx):
    kc_a, vc_a, kc_b, vc_b, s_f, s_b = ctx
    aq, ak, av, bq, bk, bv, rkv, cw, ca, cg = project(h, p, True)
    lam_init = 0.8 - 0.6 * math.exp(-0.3 * l)
    lam = diff_lambda(p['b_lambda'], lam_init)
    a_o = window_attn_latent(aq, ak, av, kc_a, vc_a, p['a_sink'].reshape(A_KV_HEADS, A_GROUP))
    b_o = diff_attend(bq, jnp.concatenate([bk, kc_b], axis=1), jnp.concatenate([bv, vc_b], axis=1), lam)
    c_o, _, _ = rwkv_mix(rkv, cw, ca, cg, p, s_f.astype(jnp.float32), s_b.astype(jnp.float32))
    return merge_out(a_o, b_o, c_o, p, lam_init)


def block(x, cvec, p, mixer):
    sh1, sc1, g1, sh2, sc2, g2 = adaln(cvec, p['ada_w'], p['ada_b'])
    h = rmsnorm(x, p['norm1_g']) * (1 + sc1) + sh1
    m, extra = mixer(h)
    x = x + g1 * m
    h = rmsnorm(x, p['norm2_g']) * (1 + sc2) + sh2
    x = x + g2 * swiglu(h, p['ffn_w1'], p['ffn_w3'], p['ffn_w2'])
    return x, extra


def setup_inputs(seed: int = 0) -> dict:
    key = jax.random.key(seed)
    ks = iter(jax.random.split(key, 40))

    def nrm(shape, scale):
        return jax.random.normal(next(ks), shape, jnp.float32) * scale

    L = DEPTH
    return {
        "x_prompt": nrm((BATCH, SEQ, D_MODEL), 1.0),
        "x_sample": nrm((DEC_BATCH, DEC_SEQ, D_MODEL), 1.0),
        "cache_a_k": nrm((DEC_BATCH, L, PAST_LEN, A_KV_HEADS, HEAD_DIM), 1.0),
        "cache_a_v": nrm((DEC_BATCH, L, PAST_LEN, A_KV_HEADS, HEAD_DIM), 1.0),
        "cache_b_k": nrm((DEC_BATCH, L, PAST_LEN, B_HEADS, 2, B_QK_DIM), 1.0),
        "cache_b_v": nrm((DEC_BATCH, L, PAST_LEN, B_HEADS, B_V_DIM), 1.0),
        "state_c_fwd": nrm((DEC_BATCH, L, C_HEADS, C_HEAD_DIM, C_HEAD_DIM), 0.3),
        "state_c_bwd": nrm((DEC_BATCH, L, C_HEADS, C_HEAD_DIM, C_HEAD_DIM), 0.3),
        "c": nrm((DEC_BATCH, D_MODEL), 1.0),
        "c_ctx": nrm((D_MODEL,), 1.0),
        "ada_w": nrm((L, D_MODEL, 6 * D_MODEL), 0.5 * D_MODEL ** -0.5),
        "ada_b": nrm((L, 6 * D_MODEL), 0.02),
        "norm1_g": 1.0 + nrm((L, D_MODEL), 0.02),
        "norm2_g": 1.0 + nrm((L, D_MODEL), 0.02),
        "w_in": nrm((L, D_MODEL, IN_COLS), D_MODEL ** -0.5),
        "a_sink": nrm((L, A_HEADS), 0.5),
        "b_lambda": nrm((L, 4, B_QK_DIM), 0.1),
        "b_subln_g": 1.0 + nrm((L, B_V_DIM), 0.02),
        "c_conv": jnp.array([0.25, 1.0, 0.25], jnp.float32)[None, :, None] + nrm((L, 3, 3 * C_WIDTH), 0.1),
        "c_w0": nrm((L, 2, C_WIDTH), 0.5),
        "c_w2": nrm((L, 2, C_W_RANK, C_WIDTH), 0.1),
        "c_a0": nrm((L, 2, C_WIDTH), 0.5),
        "c_a2": nrm((L, 2, C_A_RANK, C_WIDTH), 0.1),
        "c_g2": nrm((L, C_G_RANK, C_WIDTH), C_G_RANK ** -0.5),
        "c_kk": 1.0 + nrm((L, C_WIDTH), 0.1),
        "c_ka": 1.0 + nrm((L, C_WIDTH), 0.1),
        "c_rk": nrm((L, C_HEADS, C_HEAD_DIM), 0.1),
        "c_lnx_g": 1.0 + nrm((L, C_WIDTH), 0.02),
        "c_lnx_b": nrm((L, C_WIDTH), 0.02),
        "w_out": nrm((L, MIX_W, D_MODEL), MIX_W ** -0.5),
        "ffn_w1": nrm((L, D_MODEL, D_FF), D_MODEL ** -0.5),
        "ffn_w3": nrm((L, D_MODEL, D_FF), D_MODEL ** -0.5),
        "ffn_w2": nrm((L, D_FF, D_MODEL), D_FF ** -0.5),
        "final_g": 1.0 + nrm((D_MODEL,), 0.02),
    }


def reference(x_prompt, x_sample, cache_a_k, cache_a_v, cache_b_k, cache_b_v, state_c_fwd, state_c_bwd,
              c, c_ctx, ada_w, ada_b, norm1_g, norm2_g, w_in, a_sink, b_lambda, b_subln_g,
              c_conv, c_w0, c_w2, c_a0, c_a2, c_g2, c_kk, c_ka, c_rk, c_lnx_g, c_lnx_b,
              w_out, ffn_w1, ffn_w3, ffn_w2, final_g):
    weights = dict(ada_w=ada_w, ada_b=ada_b, norm1_g=norm1_g, norm2_g=norm2_g, w_in=w_in, a_sink=a_sink,
                   b_lambda=b_lambda, b_subln_g=b_subln_g, c_conv=c_conv, c_w0=c_w0, c_w2=c_w2, c_a0=c_a0,
                   c_a2=c_a2, c_g2=c_g2, c_kk=c_kk, c_ka=c_ka, c_rk=c_rk, c_lnx_g=c_lnx_g, c_lnx_b=c_lnx_b,
                   w_out=w_out, ffn_w1=ffn_w1, ffn_w3=ffn_w3, ffn_w2=ffn_w2)
    xp = x_prompt
    xs = x_sample
    new_ak, new_av, new_bk, new_bv, new_sf, new_sb = [], [], [], [], [], []
    for l in range(DEPTH):
        p = {name: w[l] for name, w in weights.items()}
        xp, ctx_new = block(xp, c_ctx[None], p, lambda h: mixers_context(h, p, l))
        ak, av, bk, bv, sf, sb = ctx_new
        new_ak.append(ak)
        new_av.append(av)
        new_bk.append(bk)
        new_bv.append(bv)
        new_sf.append(sf.astype(xp.dtype))
        new_sb.append(sb.astype(xp.dtype))
        ctx_cached = (cache_a_k[:, l], cache_a_v[:, l], cache_b_k[:, l], cache_b_v[:, l],
                      state_c_fwd[:, l], state_c_bwd[:, l])
        xs, _ = block(xs, c, p, lambda h: (mixers_latent(h, p, l, ctx_cached), None))
    y_prompt = rmsnorm(xp, final_g)
    y_sample = rmsnorm(xs, final_g)
    new_a_k = jnp.stack(new_ak, axis=1)
    new_a_v = jnp.stack(new_av, axis=1)
    new_b_k = jnp.stack(new_bk, axis=1)
    new_b_v = jnp.stack(new_bv, axis=1)
    new_c_fwd = jnp.stack(new_sf, axis=1)
    new_c_bwd = jnp.stack(new_sb, axis=1)
    return (y_prompt, y_sample, new_a_k, new_a_v, new_b_k, new_b_v, new_c_fwd, new_c_bwd)
```

```python
import functools
import math

import numpy as np
import jax
import jax.numpy as jnp
from jax import lax
from jax.experimental import pallas as pl
from jax.experimental.pallas import tpu as pltpu

F32 = jnp.float32
BF16 = jnp.bfloat16

D_MODEL = 1024
DEPTH = 4
NB_CTX, T_CTX = 16, 256
NB_LAT, T_LAT = 2, 4096
PAST = 512
GRID_W = 64
HD = 64
A_HEADS, A_KV = 6, 2
A_W = A_HEADS * HD
A_KVW = A_KV * HD
WINDOW = 128
B_HEADS, B_DQ, B_DV = 4, 32, 64
B_W = B_HEADS * B_DV
C_HEADS, C_N = 6, 64
C_W = C_HEADS * C_N
C_RANK = 64
C_G_RANK = 128
D_FF = 2816
IN_COLS = 2944
ROPE_THETA = 10000.0
NORM_EPS = 1e-6
GN_EPS = 64e-5
DECAY_SCALE = 0.606531
NEG_INF = -1e30

TM = 256
N_CTX_TOK = NB_CTX * T_CTX
N_LAT_TOK = NB_LAT * T_LAT
N_TOK = N_CTX_TOK + N_LAT_TOK
N_CTX_TILES = N_CTX_TOK // TM
N_TILES = N_TOK // TM
LAT_TILES_PER_SEQ = T_LAT // TM
LANES = 128
VMEM_LIMIT = 56 * 1024 * 1024

O_AQ, O_AK, O_AV, O_BQ, O_BK, O_BV, O_RKV, O_CW, O_CA, O_CG = (
    0, 384, 512, 640, 896, 1152, 1408, 2560, 2688, 2816)


def _cparams(n_grid):
    return pltpu.CompilerParams(dimension_semantics=("arbitrary",) * n_grid,
                                vmem_limit_bytes=VMEM_LIMIT)


def _sigmoid(x):
    return 1.0 / (1.0 + jnp.exp(-x))


def _tile_group(i):
    return jnp.where(i < N_CTX_TILES, 0, 1 + (i - N_CTX_TILES) // LAT_TILES_PER_SEQ)


def _lat_tile(i):
    return jnp.maximum(i - N_CTX_TILES, 0) % LAT_TILES_PER_SEQ


ADA_TN = 1536


def _ada_kernel(c_ref, w_ref, b_ref, o_ref):
    c = c_ref[...]
    s = c * _sigmoid(c)
    o_ref[0] = jnp.dot(s.astype(BF16), w_ref[0].astype(BF16),
                       preferred_element_type=F32) + b_ref[0]


def _ada_call(cvec8, ada_w, ada_b):
    n = 6 * D_MODEL
    return pl.pallas_call(
        _ada_kernel,
        grid=(DEPTH, n // ADA_TN),
        in_specs=[pl.BlockSpec((8, D_MODEL), lambda l, j: (0, 0)),
                  pl.BlockSpec((1, D_MODEL, ADA_TN), lambda l, j: (l, 0, j)),
                  pl.BlockSpec((1, 1, ADA_TN), lambda l, j: (l, 0, j))],
        out_specs=pl.BlockSpec((1, 8, ADA_TN), lambda l, j: (l, 0, j)),
        out_shape=jax.ShapeDtypeStruct((DEPTH, 8, n), F32),
        compiler_params=_cparams(2),
        name="ada",
    )(cvec8, ada_w, ada_b.reshape(DEPTH, 1, n))


def _modulated_norm(x, g, shift, scale):
    ms = jnp.mean(x * x, axis=-1, keepdims=True)
    h = x * lax.rsqrt(ms + NORM_EPS) * g
    return h * (1.0 + scale) + shift


def _rope_chunk(x, cos, sin_lo, sin_hi, half):
    up = pltpu.roll(x, LANES - half, axis=1)
    dn = pltpu.roll(x, half, axis=1)
    return x * cos + up * sin_lo + dn * sin_hi


def _inproj_kernel(x_ref, mod_ref, g_ref, w_ref,
                   cos_a, sl_a, sh_a, cos_b, sl_b, sh_b,
                   aq_ref, ak_ref, av_ref, bq_ref, bk_ref, bv_ref,
                   rkv_ref, cw_ref, ca_ref, cg_ref):
    i = pl.program_id(0)
    h = _modulated_norm(x_ref[...], g_ref[...], mod_ref[0, 0:1, :], mod_ref[0, 1:2, :])
    z = jnp.dot(h.astype(BF16), w_ref[...], preferred_element_type=F32)
    av_ref[...] = z[:, O_AV:O_BQ]
    bv_ref[...] = z[:, O_BV:O_RKV]
    rkv_ref[...] = z[:, O_RKV:O_CW]
    cw_ref[...] = z[:, O_CW:O_CA]
    ca_ref[...] = z[:, O_CA:O_CG]
    cg_ref[...] = z[:, O_CG:IN_COLS]

    @pl.when(i < N_CTX_TILES)
    def _():
        aq_ref[...] = z[:, O_AQ:O_AK]
        ak_ref[...] = z[:, O_AK:O_AV]
        bq_ref[...] = z[:, O_BQ:O_BK]
        bk_ref[...] = z[:, O_BK:O_BV]

    @pl.when(i >= N_CTX_TILES)
    def _():
        ca_, la_, ha_ = cos_a[...], sl_a[...], sh_a[...]
        cb_, lb_, hb_ = cos_b[...], sl_b[...], sh_b[...]
        for j in range(A_W // LANES):
            o = O_AQ + j * LANES
            aq_ref[:, j * LANES:(j + 1) * LANES] = _rope_chunk(z[:, o:o + LANES], ca_, la_, ha_, 16)
        ak_ref[...] = _rope_chunk(z[:, O_AK:O_AV], ca_, la_, ha_, 16)
        for j in range(B_W // LANES):
            o = O_BQ + j * LANES
            bq_ref[:, j * LANES:(j + 1) * LANES] = _rope_chunk(z[:, o:o + LANES], cb_, lb_, hb_, 8)
            o = O_BK + j * LANES
            bk_ref[:, j * LANES:(j + 1) * LANES] = _rope_chunk(z[:, o:o + LANES], cb_, lb_, hb_, 8)


def _inproj_call(x, mod_l, g, w_bf16, tabs):
    widths = (A_W, A_KVW, A_KVW, B_W, B_W, B_W, 3 * C_W, 2 * C_RANK, 2 * C_RANK, C_G_RANK)
    tab_spec = pl.BlockSpec((TM, LANES), lambda i: (_lat_tile(i), 0))
    return pl.pallas_call(
        _inproj_kernel,
        grid=(N_TILES,),
        in_specs=[pl.BlockSpec((TM, D_MODEL), lambda i: (i, 0)),
                  pl.BlockSpec((1, 8, D_MODEL), lambda i: (_tile_group(i), 0, 0)),
                  pl.BlockSpec((1, D_MODEL), lambda i: (0, 0)),
                  pl.BlockSpec((D_MODEL, IN_COLS), lambda i: (0, 0))] + [tab_spec] * 6,
        out_specs=[pl.BlockSpec((TM, w), lambda i: (i, 0)) for w in widths],
        out_shape=[jax.ShapeDtypeStruct((N_TOK, w), F32) for w in widths],
        compiler_params=_cparams(1),
        name="inproj",
    )(x, mod_l, g, w_bf16, *tabs)


def _rope_tables():
    t = np.arange(T_LAT)
    rows, cols = t // GRID_W, t % GRID_W

    def build(width):
        half = width // 4
        d = width // 2
        inv = ROPE_THETA ** (-jnp.arange(0, d, 2, dtype=F32) / d)
        lane = np.arange(LANES) % width
        part = lane // d
        p = lane % d
        f = p % half
        pos = jnp.where(jnp.asarray(part)[None, :] == 0,
                        jnp.asarray(rows, F32)[:, None], jnp.asarray(cols, F32)[:, None])
        ang = pos * inv[jnp.asarray(f)][None, :]
        cos, sin = jnp.cos(ang), jnp.sin(ang)
        lo = jnp.asarray(p < half)[None, :]
        return cos, jnp.where(lo, -sin, 0.0), jnp.where(lo, 0.0, sin)

    return build(HD) + build(B_DQ)


def _attend_a(q, sink_ref, segs):
    outs = []
    for h in range(A_HEADS):
        g = h // (A_HEADS // A_KV)
        qh = (q[:, h * HD:(h + 1) * HD] * (HD ** -0.5)).astype(BF16)
        sink = sink_ref[h]
        ss = []
        m = None
        for kt, _, mask in segs:
            s = jnp.dot(qh, kt(g), preferred_element_type=F32)
            if mask is not None:
                s = jnp.where(mask, s, NEG_INF)
            ss.append(s)
            sm = jnp.max(s, axis=-1, keepdims=True)
            m = sm if m is None else jnp.maximum(m, sm)
        m = jnp.maximum(m, sink)
        l = jnp.exp(sink - m)
        o = None
        for s, (_, v, _) in zip(ss, segs):
            p = jnp.exp(s - m)
            l = l + jnp.sum(p, axis=-1, keepdims=True)
            pv = jnp.dot(p.astype(BF16), v().astype(BF16), preferred_element_type=F32)[:, g * HD:(g + 1) * HD]
            o = pv if o is None else o + pv
        outs.append(o / l)
    return outs


def _attn_a_ctx_kernel(sink_ref, q_ref, kt_ref, v_ref, o_ref):
    outs = _attend_a(q_ref[...], sink_ref, [(lambda g: kt_ref[0, g], lambda: v_ref[...], None)])
    for h in range(A_HEADS):
        o_ref[:, h * HD:(h + 1) * HD] = outs[h]


def _attn_a_lat_kernel(sink_ref, q_ref, ktp_ref, ktc_ref, ktn_ref, vp_ref, vc_ref, vn_ref,
                       ktx_ref, vx_ref, o_ref):
    qb = pl.program_id(1)
    nqb = pl.num_programs(1)
    qi = lax.broadcasted_iota(jnp.int32, (WINDOW, WINDOW), 0)
    kj = lax.broadcasted_iota(jnp.int32, (WINDOW, WINDOW), 1)
    mask_prev = (kj >= qi) & (qb > 0)
    mask_next = (kj <= qi) & (qb < nqb - 1)
    segs = [(lambda g: ktp_ref[0, g], lambda: vp_ref[...], mask_prev),
            (lambda g: ktc_ref[0, g], lambda: vc_ref[...], None),
            (lambda g: ktn_ref[0, g], lambda: vn_ref[...], mask_next),
            (lambda g: ktx_ref[0, g], lambda: vx_ref[0], None)]
    outs = _attend_a(q_ref[...], sink_ref, segs)
    for h in range(A_HEADS):
        o_ref[:, h * HD:(h + 1) * HD] = outs[h]


def _attn_a_ctx_call(sink, aq, akt_ctx, av):
    return pl.pallas_call(
        _attn_a_ctx_kernel,
        grid=(NB_CTX,),
        in_specs=[pl.BlockSpec(memory_space=pltpu.SMEM),
                  pl.BlockSpec((T_CTX, A_W), lambda b: (b, 0)),
                  pl.BlockSpec((1, A_KV, HD, T_CTX), lambda b: (b, 0, 0, 0)),
                  pl.BlockSpec((T_CTX, A_KVW), lambda b: (b, 0))],
        out_specs=pl.BlockSpec((T_CTX, A_W), lambda b: (b, 0)),
        out_shape=jax.ShapeDtypeStruct((N_CTX_TOK, A_W), F32),
        compiler_params=_cparams(1),
        name="attn_a_ctx",
    )(sink, aq, akt_ctx, av)


def _attn_a_lat_call(sink, aq, akt_lat, av, ktx, vx):
    nqb = T_LAT // WINDOW
    ctx_blocks = N_CTX_TOK // WINDOW
    row = lambda b, j: ctx_blocks + b * nqb + j
    prev = lambda j: jnp.maximum(j - 1, 0)
    nxt = lambda j: jnp.minimum(j + 1, nqb - 1)
    kt_spec = lambda f: pl.BlockSpec((1, A_KV, HD, WINDOW), lambda b, j: (b, 0, 0, f(j)))
    v_spec = lambda f: pl.BlockSpec((WINDOW, A_KVW), lambda b, j: (row(b, f(j)), 0))
    same = lambda j: j
    return pl.pallas_call(
        _attn_a_lat_kernel,
        grid=(NB_LAT, nqb),
        in_specs=[pl.BlockSpec(memory_space=pltpu.SMEM),
                  pl.BlockSpec((WINDOW, A_W), lambda b, j: (row(b, j), 0)),
                  kt_spec(prev), kt_spec(same), kt_spec(nxt),
                  v_spec(prev), v_spec(same), v_spec(nxt),
                  pl.BlockSpec((1, A_KV, HD, PAST), lambda b, j: (b, 0, 0, 0)),
                  pl.BlockSpec((1, PAST, A_KVW), lambda b, j: (b, 0, 0))],
        out_specs=pl.BlockSpec((WINDOW, A_W), lambda b, j: (b * nqb + j, 0)),
        out_shape=jax.ShapeDtypeStruct((N_LAT_TOK, A_W), F32),
        compiler_params=_cparams(2),
        name="attn_a_lat",
    )(sink, aq, akt_lat, akt_lat, akt_lat, av, av, av, ktx, vx)


B_TQ = 256


def _attn_b_body(lam_init, q, lam_ref, g_ref, segs):
    lp = lam_ref[...]
    lam = (jnp.exp(jnp.sum(lp[0:1, :] * lp[1:2, :], axis=1, keepdims=True))
           - jnp.exp(jnp.sum(lp[2:3, :] * lp[3:4, :], axis=1, keepdims=True)) + lam_init)
    outs = []
    for h in range(B_HEADS):
        maps = []
        for mi in range(2):
            c0 = h * B_DV + mi * B_DQ
            qm = (q[:, c0:c0 + B_DQ] * (B_DQ ** -0.5)).astype(BF16)
            ss = [jnp.dot(qm, kt[0, 2 * h + mi], preferred_element_type=F32) for kt, _ in segs]
            m = None
            for s in ss:
                sm = jnp.max(s, axis=-1, keepdims=True)
                m = sm if m is None else jnp.maximum(m, sm)
            l = None
            o = None
            for s, (_, v) in zip(ss, segs):
                p = jnp.exp(s - m)
                ps = jnp.sum(p, axis=-1, keepdims=True)
                l = ps if l is None else l + ps
                pv = jnp.dot(p.astype(BF16), v[0], preferred_element_type=F32)[:, h * B_DV:(h + 1) * B_DV]
                o = pv if o is None else o + pv
            maps.append(o / l)
        a = maps[0] - lam * maps[1]
        ms = jnp.mean(a * a, axis=-1, keepdims=True)
        outs.append(a * lax.rsqrt(ms + NORM_EPS) * g_ref[...] * (1.0 - lam_init))
    return outs


def _attn_b_ctx_kernel(lam_init, lam_ref, g_ref, q_ref, kt_ref, v_ref, o_ref):
    outs = _attn_b_body(lam_init, q_ref[...], lam_ref, g_ref, [(kt_ref, v_ref)])
    for h in range(B_HEADS):
        o_ref[:, h * B_DV:(h + 1) * B_DV] = outs[h]


def _attn_b_lat_kernel(lam_init, lam_ref, g_ref, q_ref, kt_ref, v_ref, ktx_ref, vx_ref, o_ref):
    outs = _attn_b_body(lam_init, q_ref[...], lam_ref, g_ref,
                        [(kt_ref, v_ref), (ktx_ref, vx_ref)])
    for h in range(B_HEADS):
        o_ref[:, h * B_DV:(h + 1) * B_DV] = outs[h]


def _attn_b_ctx_call(lam_init, lam_p, g, bq, bkt_ctx, bv_ctx):
    return pl.pallas_call(
        functools.partial(_attn_b_ctx_kernel, lam_init),
        grid=(NB_CTX,),
        in_specs=[pl.BlockSpec((4, B_DQ), lambda b: (0, 0)),
                  pl.BlockSpec((1, B_DV), lambda b: (0, 0)),
                  pl.BlockSpec((T_CTX, B_W), lambda b: (b, 0)),
                  pl.BlockSpec((1, 2 * B_HEADS, B_DQ, T_CTX), lambda b: (b, 0, 0, 0)),
                  pl.BlockSpec((1, T_CTX, B_W), lambda b: (b, 0, 0))],
        out_specs=pl.BlockSpec((T_CTX, B_W), lambda b: (b, 0)),
        out_shape=jax.ShapeDtypeStruct((N_CTX_TOK, B_W), F32),
        compiler_params=_cparams(1),
        name="attn_b_ctx",
    )(lam_p, g, bq, bkt_ctx, bv_ctx)


def _attn_b_lat_call(lam_init, lam_p, g, bq, bkt_lat, bv_lat, ktx, vx):
    nq = T_LAT // B_TQ
    ctx_blocks = N_CTX_TOK // B_TQ
    return pl.pallas_call(
        functools.partial(_attn_b_lat_kernel, lam_init),
        grid=(NB_LAT, nq),
        in_specs=[pl.BlockSpec((4, B_DQ), lambda b, j: (0, 0)),
                  pl.BlockSpec((1, B_DV), lambda b, j: (0, 0)),
                  pl.BlockSpec((B_TQ, B_W), lambda b, j: (ctx_blocks + b * nq + j, 0)),
                  pl.BlockSpec((1, 2 * B_HEADS, B_DQ, T_LAT), lambda b, j: (b, 0, 0, 0)),
                  pl.BlockSpec((1, T_LAT, B_W), lambda b, j: (b, 0, 0)),
                  pl.BlockSpec((1, 2 * B_HEADS, B_DQ, PAST), lambda b, j: (b, 0, 0, 0)),
                  pl.BlockSpec((1, PAST, B_W), lambda b, j: (b, 0, 0))],
        out_specs=pl.BlockSpec((B_TQ, B_W), lambda b, j: (b * nq + j, 0)),
        out_shape=jax.ShapeDtypeStruct((N_LAT_TOK, B_W), F32),
        compiler_params=_cparams(2),
        name="attn_b_lat",
    )(lam_p, g, bq, bkt_lat, bv_lat, ktx, vx)


HALO = 8


def _dot_f32(a, b):
    return jnp.dot(a, b, preferred_element_type=F32, precision=lax.Precision.HIGHEST)


def _rwkv_prep_kernel(rkv_ref, prev_ref, next_ref, cw_ref, ca_ref,
                      conv_ref, w0_ref, w2_ref, a0_ref, a2_ref, kk_ref, ka_ref, rk_ref, ones_ref,
                      r_ref, nkk_ref, v_ref, w_f, b_f, kd_f, w_b, b_b, kd_b, bonus_ref):
    i = pl.program_id(0)
    li = _lat_tile(i)
    is_ctx = i < N_CTX_TILES
    has_prev = jnp.logical_and(jnp.logical_not(is_ctx), li > 0).astype(F32)
    has_next = jnp.logical_and(jnp.logical_not(is_ctx), li < LAT_TILES_PER_SEQ - 1).astype(F32)
    x = rkv_ref[...]
    row = lax.broadcasted_iota(jnp.int32, x.shape, 0)
    xm = jnp.where(row == 0, prev_ref[HALO - 1:HALO, :] * has_prev, pltpu.roll(x, 1, axis=0))
    xp = jnp.where(row == TM - 1, next_ref[0:1, :] * has_next, pltpu.roll(x, TM - 1, axis=0))
    y = xm * conv_ref[0:1, :] + x * conv_ref[1:2, :] + xp * conv_ref[2:3, :]
    r, k, v = y[:, :C_W], y[:, C_W:2 * C_W], y[:, 2 * C_W:]
    ones = ones_ref[...]

    kk = k * kk_ref[...]
    kk = kk / jnp.maximum(jnp.sqrt(_dot_f32(kk * kk, ones)), 1e-12)
    lw = _dot_f32(jnp.tanh(cw_ref[...]), w2_ref[...])
    la = _dot_f32(ca_ref[...], a2_ref[...])
    r_ref[...] = r
    nkk_ref[...] = -kk
    v_ref[...] = v
    bonus = jnp.zeros_like(v)
    for d, (w_o, b_o, kd_o) in enumerate(((w_f, b_f, kd_f), (w_b, b_b, kd_b))):
        sl = slice(d * C_W, (d + 1) * C_W)
        w = jnp.exp(-DECAY_SCALE * _sigmoid(w0_ref[d:d + 1, :] + lw[:, sl]))
        a = _sigmoid(a0_ref[d:d + 1, :] + la[:, sl])
        kd = k * (1.0 + (a - 1.0) * ka_ref[...])
        w_o[...] = w
        b_o[...] = kk * a
        kd_o[...] = kd
        bonus = bonus + _dot_f32(r * kd * rk_ref[...], ones) * v
    bonus_ref[...] = bonus


def _rwkv_prep_call(rkv, cw, ca, p):
    nh = TM // HALO
    last = N_TOK // HALO - 1
    full = lambda shape: pl.BlockSpec(shape, lambda i: (0,) * len(shape))
    tile = lambda w: pl.BlockSpec((TM, w), lambda i: (i, 0))
    return pl.pallas_call(
        _rwkv_prep_kernel,
        grid=(N_TILES,),
        in_specs=[tile(3 * C_W),
                  pl.BlockSpec((HALO, 3 * C_W), lambda i: (jnp.maximum(i * nh - 1, 0), 0)),
                  pl.BlockSpec((HALO, 3 * C_W), lambda i: (jnp.minimum((i + 1) * nh, last), 0)),
                  tile(2 * C_RANK), tile(2 * C_RANK),
                  full((3, 3 * C_W)), full((2, C_W)), full((2 * C_RANK, 2 * C_W)),
                  full((2, C_W)), full((2 * C_RANK, 2 * C_W)),
                  full((1, C_W)), full((1, C_W)), full((1, C_W)), full((C_W, C_W))],
        out_specs=[tile(C_W)] * 10,
        out_shape=[jax.ShapeDtypeStruct((N_TOK, C_W), F32)] * 10,
        compiler_params=_cparams(1),
        name="rwkv_prep",
    )(rkv, rkv, rkv, cw, ca, p["conv"], p["w0"], p["w2"], p["a0"], p["a2"],
      p["kk"], p["ka"], p["rk"], p["ones"])


N_BLK = TM // LANES


def _scan_kernel(r_f, nkk_f, v_f, w_f, b_f, kd_f, r_b, nkk_b, v_b, w_b, b_b, kd_b, s0f_ref, s0b_ref,
                 yf_ref, yb_ref, sf_ref, sb_ref,
                 st, rows, vt, yt):
    k = pl.program_id(1)
    nk = pl.num_programs(1)
    lane = lax.broadcasted_iota(jnp.int32, (TM, LANES), 1)

    @pl.when(k == 0)
    def _():
        st[...] = jnp.zeros_like(st)
        for d, s0 in enumerate((s0f_ref, s0b_ref)):
            for h in range(C_HEADS):
                st[d, h, :, 0:C_N] = s0[0, h]

    for d, srcs in enumerate(((r_f, nkk_f, w_f, b_f, kd_f), (r_b, nkk_b, w_b, b_b, kd_b))):
        for a, src in enumerate(srcs):
            for pr in range(C_HEADS // 2):
                chunk = src[:, pr * LANES:(pr + 1) * LANES]
                rows[d, a, 2 * pr] = jnp.where(lane < C_N, chunk, 0.0)
                rows[d, a, 2 * pr + 1] = jnp.where(lane < C_N, pltpu.roll(chunk, C_N, axis=1), 0.0)
    for d, src in enumerate((v_f, v_b)):
        for pr in range(C_HEADS // 2):
            for blk in range(N_BLK):
                tr = src[blk * LANES:(blk + 1) * LANES, pr * LANES:(pr + 1) * LANES].T
                vt[d, 2 * pr, blk] = tr[0:C_N, :]
                vt[d, 2 * pr + 1, blk] = tr[C_N:2 * C_N, :]

    yt[...] = jnp.zeros_like(yt)
    lane1 = lax.broadcasted_iota(jnp.int32, (1, LANES), 1)

    def step(i, carry):
        for d in range(2):
            t = i if d == 0 else TM - 1 - i
            blk = t // LANES
            onehot = lane1 == (t % LANES)
            for h in range(C_HEADS):
                s = st[d, h]
                rr = rows[d, 0, h, pl.ds(t, 1), :]
                nkk = rows[d, 1, h, pl.ds(t, 1), :]
                w = rows[d, 2, h, pl.ds(t, 1), :]
                b = rows[d, 3, h, pl.ds(t, 1), :]
                kd = rows[d, 4, h, pl.ds(t, 1), :]
                sa = jnp.sum(s * nkk, axis=1, keepdims=True)
                vcol = jnp.sum(jnp.where(onehot, vt[d, h, blk], 0.0), axis=1, keepdims=True)
                s = s * w + sa * b + vcol * kd
                st[d, h] = s
                ycol = jnp.sum(s * rr, axis=1, keepdims=True)
                yt[d, h, blk] = jnp.where(onehot, ycol, yt[d, h, blk])
        return carry

    lax.fori_loop(0, TM, step, 0)

    for d, y_ref in enumerate((yf_ref, yb_ref)):
        for pr in range(C_HEADS // 2):
            for blk in range(N_BLK):
                both = jnp.concatenate([yt[d, 2 * pr, blk], yt[d, 2 * pr + 1, blk]], axis=0)
                y_ref[blk * LANES:(blk + 1) * LANES, pr * LANES:(pr + 1) * LANES] = both.T

    @pl.when(k == nk - 1)
    def _():
        for d, s_out in enumerate((sf_ref, sb_ref)):
            for h in range(C_HEADS):
                s_out[0, h] = st[d, h, :, 0:C_N]


def _scan_call(name, n_seq, n_tiles, tile0, prep, s0f, s0b):
    r, nkk, v, w_f, b_f, kd_f, w_b, b_b, kd_b = prep
    fwd = pl.BlockSpec((TM, C_W), lambda s, k: (tile0 + s * n_tiles + k, 0))
    bwd = pl.BlockSpec((TM, C_W), lambda s, k: (tile0 + s * n_tiles + (n_tiles - 1 - k), 0))
    st_spec = pl.BlockSpec((1, C_HEADS, C_N, C_N), lambda s, k: (s, 0, 0, 0))
    yf_spec = pl.BlockSpec((TM, C_W), lambda s, k: (s * n_tiles + k, 0))
    yb_spec = pl.BlockSpec((TM, C_W), lambda s, k: (s * n_tiles + (n_tiles - 1 - k), 0))
    n_tok = n_seq * n_tiles * TM
    return pl.pallas_call(
        _scan_kernel,
        grid=(n_seq, n_tiles),
        in_specs=[fwd] * 6 + [bwd] * 6 + [st_spec, st_spec],
        out_specs=[yf_spec, yb_spec, st_spec, st_spec],
        out_shape=[jax.ShapeDtypeStruct((n_tok, C_W), F32)] * 2
                  + [jax.ShapeDtypeStruct((n_seq, C_HEADS, C_N, C_N), F32)] * 2,
        scratch_shapes=[pltpu.VMEM((2, C_HEADS, C_N, LANES), F32),
                        pltpu.VMEM((2, 5, C_HEADS, TM, LANES), F32),
                        pltpu.VMEM((2, C_HEADS, N_BLK, C_N, LANES), F32),
                        pltpu.VMEM((2, C_HEADS, N_BLK, C_N, LANES), F32)],
        compiler_params=_cparams(2),
        name=name,
    )(r, nkk, v, w_f, b_f, kd_f, r, nkk, v, w_b, b_b, kd_b, s0f, s0b)


def _outproj_kernel(x_ref, mod_ref, ao_ref, bo_ref, yf_ref, yb_ref, bonus_ref, cg_ref,
                    lng_ref, lnb_ref, g2_ref, mean_ref, wa_ref, wb_ref, wc_ref, o_ref):
    y = yf_ref[...] + yb_ref[...] + bonus_ref[...]
    mean_m = mean_ref[...]
    mu = _dot_f32(y, mean_m)
    dy = y - mu
    var = _dot_f32(dy * dy, mean_m)
    yn = dy * lax.rsqrt(var + GN_EPS) * lng_ref[...] + lnb_ref[...]
    gate = jnp.dot(_sigmoid(cg_ref[...]).astype(BF16), g2_ref[...], preferred_element_type=F32)
    co = yn * gate
    m = (jnp.dot(ao_ref[...].astype(BF16), wa_ref[...], preferred_element_type=F32)
         + jnp.dot(bo_ref[...].astype(BF16), wb_ref[...], preferred_element_type=F32)
         + jnp.dot(co.astype(BF16), wc_ref[...], preferred_element_type=F32))
    o_ref[...] = x_ref[...] + mod_ref[0, 2:3, :] * m


def _outproj_call(x, mod_l, ao, bo, yf, yb, bonus, cg, p):
    full = lambda shape: pl.BlockSpec(shape, lambda i: (0,) * len(shape))
    tile = lambda w: pl.BlockSpec((TM, w), lambda i: (i, 0))
    return pl.pallas_call(
        _outproj_kernel,
        grid=(N_TILES,),
        in_specs=[tile(D_MODEL),
                  pl.BlockSpec((1, 8, D_MODEL), lambda i: (_tile_group(i), 0, 0)),
                  tile(A_W), tile(B_W), tile(C_W), tile(C_W), tile(C_W), tile(C_G_RANK),
                  full((1, C_W)), full((1, C_W)), full((C_G_RANK, C_W)), full((C_W, C_W)),
                  full((A_W, D_MODEL)), full((B_W, D_MODEL)), full((C_W, D_MODEL))],
        out_specs=tile(D_MODEL),
        out_shape=jax.ShapeDtypeStruct((N_TOK, D_MODEL), F32),
        compiler_params=_cparams(1),
        name="outproj",
    )(x, mod_l, ao, bo, yf, yb, bonus, cg, p["lnx_g"], p["lnx_b"], p["g2"], p["mean"],
      p["wa"], p["wb"], p["wc"])


def _ffn_kernel(final, x_ref, mod_ref, g_ref, w1_ref, w3_ref, w2_ref, fg_ref, o_ref):
    x = x_ref[...]
    h = _modulated_norm(x, g_ref[...], mod_ref[0, 3:4, :], mod_ref[0, 4:5, :]).astype(BF16)
    u = jnp.dot(h, w1_ref[...], preferred_element_type=F32)
    t = jnp.dot(h, w3_ref[...], preferred_element_type=F32)
    act = (u * _sigmoid(u) * t).astype(BF16)
    y = x + mod_ref[0, 5:6, :] * jnp.dot(act, w2_ref[...], preferred_element_type=F32)
    if final:
        ms = jnp.mean(y * y, axis=-1, keepdims=True)
        y = y * lax.rsqrt(ms + NORM_EPS) * fg_ref[...]
    o_ref[...] = y


def _ffn_call(final, x, mod_l, g, w1, w3, w2, final_g):
    once = lambda shape: pl.BlockSpec(shape, lambda i: (0,) * len(shape),
                                      pipeline_mode=pl.Buffered(1))
    return pl.pallas_call(
        functools.partial(_ffn_kernel, final),
        grid=(N_TILES,),
        in_specs=[pl.BlockSpec((TM, D_MODEL), lambda i: (i, 0)),
                  pl.BlockSpec((1, 8, D_MODEL), lambda i: (_tile_group(i), 0, 0)),
                  once((1, D_MODEL)),
                  once((D_MODEL, D_FF)), once((D_MODEL, D_FF)), once((D_FF, D_MODEL)),
                  once((1, D_MODEL))],
        out_specs=pl.BlockSpec((TM, D_MODEL), lambda i: (i, 0)),
        out_shape=jax.ShapeDtypeStruct((N_TOK, D_MODEL), F32),
        compiler_params=_cparams(1),
        name="ffn",
    )(x, mod_l, g, w1, w3, w2, final_g)


def _block_diag2(m):
    z = jnp.zeros_like(m[0])
    return jnp.concatenate([jnp.concatenate([m[0], z], axis=1),
                            jnp.concatenate([z, m[1]], axis=1)], axis=0)


def _keys_t(k, nb, t, heads, dim):
    return k.reshape(nb, t, heads, dim).transpose(0, 2, 3, 1).astype(BF16)


def kernel(x_prompt, x_sample, cache_a_k, cache_a_v, cache_b_k, cache_b_v, state_c_fwd, state_c_bwd,
           c, c_ctx, ada_w, ada_b, norm1_g, norm2_g, w_in, a_sink, b_lambda, b_subln_g,
           c_conv, c_w0, c_w2, c_a0, c_a2, c_g2, c_kk, c_ka, c_rk, c_lnx_g, c_lnx_b,
           w_out, ffn_w1, ffn_w3, ffn_w2, final_g):
    x = jnp.concatenate([x_prompt.reshape(N_CTX_TOK, D_MODEL),
                         x_sample.reshape(N_LAT_TOK, D_MODEL)], axis=0)
    cvec8 = jnp.concatenate([c_ctx[None], c, jnp.zeros((8 - 1 - NB_LAT, D_MODEL), F32)], axis=0)
    mod = _ada_call(cvec8, ada_w, ada_b)
    mod = mod[:, :1 + NB_LAT].reshape(DEPTH, 1 + NB_LAT, 6, D_MODEL)
    mod = jnp.pad(mod, ((0, 0), (0, 0), (0, 2), (0, 0)))

    tabs = _rope_tables()
    head_id = np.arange(C_W) // C_N
    block_ones = jnp.asarray(head_id[:, None] == head_id[None, :], F32)
    final_g2 = final_g.reshape(1, D_MODEL)
    s0_ctx = jnp.zeros((NB_CTX, C_HEADS, C_N, C_N), F32)

    new_ak, new_av, new_bk, new_bv, new_sf, new_sb = [], [], [], [], [], []
    for l in range(DEPTH):
        lam_init = 0.8 - 0.6 * math.exp(-0.3 * l)
        aq, ak, av, bq, bk, bv, rkv, cw, ca, cg = _inproj_call(
            x, mod[l], norm1_g[l].reshape(1, D_MODEL), w_in[l].astype(BF16), tabs)

        new_ak.append(ak[:N_CTX_TOK].reshape(NB_CTX, T_CTX, A_KV, HD))
        new_av.append(av[:N_CTX_TOK].reshape(NB_CTX, T_CTX, A_KV, HD))
        new_bk.append(bk[:N_CTX_TOK].reshape(NB_CTX, T_CTX, B_HEADS, 2, B_DQ))
        new_bv.append(bv[:N_CTX_TOK].reshape(NB_CTX, T_CTX, B_HEADS, B_DV))

        sink = a_sink[l]
        ao_ctx = _attn_a_ctx_call(sink, aq, _keys_t(ak[:N_CTX_TOK], NB_CTX, T_CTX, A_KV, HD), av)
        ao_lat = _attn_a_lat_call(
            sink, aq, _keys_t(ak[N_CTX_TOK:], NB_LAT, T_LAT, A_KV, HD), av,
            _keys_t(cache_a_k[:, l].reshape(NB_LAT * PAST, A_KVW), NB_LAT, PAST, A_KV, HD),
            cache_a_v[:, l].reshape(NB_LAT, PAST, A_KVW))
        ao = jnp.concatenate([ao_ctx, ao_lat], axis=0)

        lam_p, sub_g = b_lambda[l], b_subln_g[l].reshape(1, B_DV)
        bo_ctx = _attn_b_ctx_call(
            lam_init, lam_p, sub_g, bq, _keys_t(bk[:N_CTX_TOK], NB_CTX, T_CTX, 2 * B_HEADS, B_DQ),
            bv[:N_CTX_TOK].reshape(NB_CTX, T_CTX, B_W).astype(BF16))
        bo_lat = _attn_b_lat_call(
            lam_init, lam_p, sub_g, bq, _keys_t(bk[N_CTX_TOK:], NB_LAT, T_LAT, 2 * B_HEADS, B_DQ),
            bv[N_CTX_TOK:].reshape(NB_LAT, T_LAT, B_W).astype(BF16),
            _keys_t(cache_b_k[:, l].reshape(NB_LAT * PAST, B_W), NB_LAT, PAST, 2 * B_HEADS, B_DQ),
            cache_b_v[:, l].reshape(NB_LAT, PAST, B_W).astype(BF16))
        bo = jnp.concatenate([bo_ctx, bo_lat], axis=0)

        prep = _rwkv_prep_call(rkv, cw, ca, dict(
            conv=c_conv[l], w0=c_w0[l], w2=_block_diag2(c_w2[l]), a0=c_a0[l], a2=_block_diag2(c_a2[l]),
            kk=c_kk[l].reshape(1, C_W), ka=c_ka[l].reshape(1, C_W), rk=c_rk[l].reshape(1, C_W),
            ones=block_ones))
        r, nkk, v, w_f, b_f, kd_f, w_b, b_b, kd_b, bonus = prep
        scan_in = (r, nkk, v, w_f, b_f, kd_f, w_b, b_b, kd_b)
        yf_c, yb_c, sf, sb = _scan_call("scan_ctx", NB_CTX, 1, 0, scan_in, s0_ctx, s0_ctx)
        yf_l, yb_l, _, _ = _scan_call("scan_lat", NB_LAT, LAT_TILES_PER_SEQ, N_CTX_TILES, scan_in,
                                      state_c_fwd[:, l], state_c_bwd[:, l])
        new_sf.append(sf)
        new_sb.append(sb)
        yf = jnp.concatenate([yf_c, yf_l], axis=0)
        yb = jnp.concatenate([yb_c, yb_l], axis=0)

        wo = w_out[l].astype(BF16)
        x = _outproj_call(x, mod[l], ao, bo, yf, yb, bonus, cg, dict(
            lnx_g=c_lnx_g[l].reshape(1, C_W), lnx_b=c_lnx_b[l].reshape(1, C_W),
            g2=c_g2[l].astype(BF16), mean=block_ones / C_N,
            wa=wo[:A_W], wb=wo[A_W:A_W + B_W], wc=wo[A_W + B_W:]))

        x = _ffn_call(l == DEPTH - 1, x, mod[l], norm2_g[l].reshape(1, D_MODEL),
                      ffn_w1[l].astype(BF16), ffn_w3[l].astype(BF16), ffn_w2[l].astype(BF16), final_g2)

    y_prompt = x[:N_CTX_TOK].reshape(NB_CTX, T_CTX, D_MODEL)
    y_sample = x[N_CTX_TOK:].reshape(NB_LAT, T_LAT, D_MODEL)
    return (y_prompt, y_sample,
            jnp.stack(new_ak, axis=1), jnp.stack(new_av, axis=1),
            jnp.stack(new_bk, axis=1), jnp.stack(new_bv, axis=1),
            jnp.stack(new_sf, axis=1), jnp.stack(new_sb, axis=1))
```

```python
import functools
import math

import numpy as np
import jax
import jax.numpy as jnp
from jax import lax
from jax.experimental import pallas as pl
from jax.experimental.pallas import tpu as pltpu

F32 = jnp.float32
BF16 = jnp.bfloat16

D_MODEL = 1024
DEPTH = 4
NB_CTX, T_CTX = 16, 256
NB_LAT, T_LAT = 2, 4096
PAST = 512
GRID_W = 64
HD = 64
A_HEADS, A_KV = 6, 2
A_W = A_HEADS * HD
A_KVW = A_KV * HD
WINDOW = 128
B_HEADS, B_DQ, B_DV = 4, 32, 64
B_W = B_HEADS * B_DV
C_HEADS, C_N = 6, 64
C_W = C_HEADS * C_N
C_RANK = 64
C_G_RANK = 128
D_FF = 2816
IN_COLS = 2944
ROPE_THETA = 10000.0
NORM_EPS = 1e-6
GN_EPS = 64e-5
DECAY_SCALE = 0.606531
NEG_INF = -1e30

TM = 256
N_CTX_TOK = NB_CTX * T_CTX
N_LAT_TOK = NB_LAT * T_LAT
N_TOK = N_CTX_TOK + N_LAT_TOK
N_CTX_TILES = N_CTX_TOK // TM
N_TILES = N_TOK // TM
LAT_TILES_PER_SEQ = T_LAT // TM
LANES = 128
VMEM_LIMIT = 56 * 1024 * 1024

O_AQ, O_AK, O_AV, O_BQ, O_BK, O_BV, O_RKV, O_CW, O_CA, O_CG = (
    0, 384, 512, 640, 896, 1152, 1408, 2560, 2688, 2816)


def _cparams(n_grid):
    return pltpu.CompilerParams(dimension_semantics=("arbitrary",) * n_grid,
                                vmem_limit_bytes=VMEM_LIMIT)


def _sigmoid(x):
    return 1.0 / (1.0 + jnp.exp(-x))


def _tile_group(i):
    return jnp.where(i < N_CTX_TILES, 0, 1 + (i - N_CTX_TILES) // LAT_TILES_PER_SEQ)


def _lat_tile(i):
    return jnp.maximum(i - N_CTX_TILES, 0) % LAT_TILES_PER_SEQ


ADA_TN = 1536


def _ada_kernel(c_ref, w_ref, b_ref, o_ref):
    c = c_ref[...]
    s = c * _sigmoid(c)
    o_ref[0] = jnp.dot(s.astype(BF16), w_ref[0].astype(BF16),
                       preferred_element_type=F32) + b_ref[0]


def _ada_call(cvec8, ada_w, ada_b):
    n = 6 * D_MODEL
    return pl.pallas_call(
        _ada_kernel,
        grid=(DEPTH, n // ADA_TN),
        in_specs=[pl.BlockSpec((8, D_MODEL), lambda l, j: (0, 0)),
                  pl.BlockSpec((1, D_MODEL, ADA_TN), lambda l, j: (l, 0, j)),
                  pl.BlockSpec((1, 1, ADA_TN), lambda l, j: (l, 0, j))],
        out_specs=pl.BlockSpec((1, 8, ADA_TN), lambda l, j: (l, 0, j)),
        out_shape=jax.ShapeDtypeStruct((DEPTH, 8, n), F32),
        compiler_params=_cparams(2),
        name="ada",
    )(cvec8, ada_w, ada_b.reshape(DEPTH, 1, n))


def _modulated_norm(x, g, shift, scale):
    ms = jnp.mean(x * x, axis=-1, keepdims=True)
    h = x * lax.rsqrt(ms + NORM_EPS) * g
    return h * (1.0 + scale) + shift


def _rope_chunk(x, cos, sin_lo, sin_hi, half):
    up = pltpu.roll(x, LANES - half, axis=1)
    dn = pltpu.roll(x, half, axis=1)
    return x * cos + up * sin_lo + dn * sin_hi


def _inproj_kernel(x_ref, mod_ref, g_ref, w_ref,
                   cos_a, sl_a, sh_a, cos_b, sl_b, sh_b,
                   aq_ref, ak_ref, av_ref, bq_ref, bk_ref, bv_ref,
                   rkv_ref, cw_ref, ca_ref, cg_ref):
    i = pl.program_id(0)
    h = _modulated_norm(x_ref[...], g_ref[...], mod_ref[0, 0:1, :], mod_ref[0, 1:2, :])
    z = jnp.dot(h.astype(BF16), w_ref[...], preferred_element_type=F32)
    av_ref[...] = z[:, O_AV:O_BQ]
    bv_ref[...] = z[:, O_BV:O_RKV]
    rkv_ref[...] = z[:, O_RKV:O_CW]
    cw_ref[...] = z[:, O_CW:O_CA]
    ca_ref[...] = z[:, O_CA:O_CG]
    cg_ref[...] = z[:, O_CG:IN_COLS]

    @pl.when(i < N_CTX_TILES)
    def _():
        aq_ref[...] = z[:, O_AQ:O_AK]
        ak_ref[...] = z[:, O_AK:O_AV]
        bq_ref[...] = z[:, O_BQ:O_BK]
        bk_ref[...] = z[:, O_BK:O_BV]

    @pl.when(i >= N_CTX_TILES)
    def _():
        ca_, la_, ha_ = cos_a[...], sl_a[...], sh_a[...]
        cb_, lb_, hb_ = cos_b[...], sl_b[...], sh_b[...]
        for j in range(A_W // LANES):
            o = O_AQ + j * LANES
            aq_ref[:, j * LANES:(j + 1) * LANES] = _rope_chunk(z[:, o:o + LANES], ca_, la_, ha_, 16)
        ak_ref[...] = _rope_chunk(z[:, O_AK:O_AV], ca_, la_, ha_, 16)
        for j in range(B_W // LANES):
            o = O_BQ + j * LANES
            bq_ref[:, j * LANES:(j + 1) * LANES] = _rope_chunk(z[:, o:o + LANES], cb_, lb_, hb_, 8)
            o = O_BK + j * LANES
            bk_ref[:, j * LANES:(j + 1) * LANES] = _rope_chunk(z[:, o:o + LANES], cb_, lb_, hb_, 8)


def _inproj_call(x, mod_l, g, w_bf16, tabs):
    widths = (A_W, A_KVW, A_KVW, B_W, B_W, B_W, 3 * C_W, 2 * C_RANK, 2 * C_RANK, C_G_RANK)
    tab_spec = pl.BlockSpec((TM, LANES), lambda i: (_lat_tile(i), 0))
    return pl.pallas_call(
        _inproj_kernel,
        grid=(N_TILES,),
        in_specs=[pl.BlockSpec((TM, D_MODEL), lambda i: (i, 0)),
                  pl.BlockSpec((1, 8, D_MODEL), lambda i: (_tile_group(i), 0, 0)),
                  pl.BlockSpec((1, D_MODEL), lambda i: (0, 0)),
                  pl.BlockSpec((D_MODEL, IN_COLS), lambda i: (0, 0))] + [tab_spec] * 6,
        out_specs=[pl.BlockSpec((TM, w), lambda i: (i, 0)) for w in widths],
        out_shape=[jax.ShapeDtypeStruct((N_TOK, w), F32) for w in widths],
        compiler_params=_cparams(1),
        name="inproj",
    )(x, mod_l, g, w_bf16, *tabs)


def _rope_tables():
    t = np.arange(T_LAT)
    rows, cols = t // GRID_W, t % GRID_W

    def build(width):
        half = width // 4
        d = width // 2
        inv = ROPE_THETA ** (-jnp.arange(0, d, 2, dtype=F32) / d)
        lane = np.arange(LANES) % width
        part = lane // d
        p = lane % d
        f = p % half
        pos = jnp.where(jnp.asarray(part)[None, :] == 0,
                        jnp.asarray(rows, F32)[:, None], jnp.asarray(cols, F32)[:, None])
        ang = pos * inv[jnp.asarray(f)][None, :]
        cos, sin = jnp.cos(ang), jnp.sin(ang)
        lo = jnp.asarray(p < half)[None, :]
        return cos, jnp.where(lo, -sin, 0.0), jnp.where(lo, 0.0, sin)

    return build(HD) + build(B_DQ)


def _attend_a(q, sink_ref, segs):
    outs = []
    for h in range(A_HEADS):
        g = h // (A_HEADS // A_KV)
        qh = (q[:, h * HD:(h + 1) * HD] * (HD ** -0.5)).astype(BF16)
        sink = sink_ref[h]
        ss = []
        m = None
        for kt, _, mask in segs:
            s = jnp.dot(qh, kt(g), preferred_element_type=F32)
            if mask is not None:
                s = jnp.where(mask, s, NEG_INF)
            ss.append(s)
            sm = jnp.max(s, axis=-1, keepdims=True)
            m = sm if m is None else jnp.maximum(m, sm)
        m = jnp.maximum(m, sink)
        l = jnp.exp(sink - m)
        o = None
        for s, (_, v, _) in zip(ss, segs):
            p = jnp.exp(s - m)
            l = l + jnp.sum(p, axis=-1, keepdims=True)
            pv = jnp.dot(p.astype(BF16), v().astype(BF16), preferred_element_type=F32)[:, g * HD:(g + 1) * HD]
            o = pv if o is None else o + pv
        outs.append(o / l)
    return outs


def _attn_a_ctx_kernel(sink_ref, q_ref, kt_ref, v_ref, o_ref):
    outs = _attend_a(q_ref[...], sink_ref, [(lambda g: kt_ref[0, g], lambda: v_ref[...], None)])
    for h in range(A_HEADS):
        o_ref[:, h * HD:(h + 1) * HD] = outs[h]


def _attn_a_lat_kernel(sink_ref, q_ref, ktp_ref, ktc_ref, ktn_ref, vp_ref, vc_ref, vn_ref,
                       ktx_ref, vx_ref, o_ref):
    qb = pl.program_id(1)
    nqb = pl.num_programs(1)
    qi = lax.broadcasted_iota(jnp.int32, (WINDOW, WINDOW), 0)
    kj = lax.broadcasted_iota(jnp.int32, (WINDOW, WINDOW), 1)
    mask_prev = (kj >= qi) & (qb > 0)
    mask_next = (kj <= qi) & (qb < nqb - 1)
    segs = [(lambda g: ktp_ref[0, g], lambda: vp_ref[...], mask_prev),
            (lambda g: ktc_ref[0, g], lambda: vc_ref[...], None),
            (lambda g: ktn_ref[0, g], lambda: vn_ref[...], mask_next),
            (lambda g: ktx_ref[0, g], lambda: vx_ref[0], None)]
    outs = _attend_a(q_ref[...], sink_ref, segs)
    for h in range(A_HEADS):
        o_ref[:, h * HD:(h + 1) * HD] = outs[h]


def _attn_a_ctx_call(sink, aq, akt_ctx, av):
    return pl.pallas_call(
        _attn_a_ctx_kernel,
        grid=(NB_CTX,),
        in_specs=[pl.BlockSpec(memory_space=pltpu.SMEM),
                  pl.BlockSpec((T_CTX, A_W), lambda b: (b, 0)),
                  pl.BlockSpec((1, A_KV, HD, T_CTX), lambda b: (b, 0, 0, 0)),
                  pl.BlockSpec((T_CTX, A_KVW), lambda b: (b, 0))],
        out_specs=pl.BlockSpec((T_CTX, A_W), lambda b: (b, 0)),
        out_shape=jax.ShapeDtypeStruct((N_CTX_TOK, A_W), F32),
        compiler_params=_cparams(1),
        name="attn_a_ctx",
    )(sink, aq, akt_ctx, av)


def _attn_a_lat_call(sink, aq, akt_lat, av, ktx, vx):
    nqb = T_LAT // WINDOW
    ctx_blocks = N_CTX_TOK // WINDOW
    row = lambda b, j: ctx_blocks + b * nqb + j
    prev = lambda j: jnp.maximum(j - 1, 0)
    nxt = lambda j: jnp.minimum(j + 1, nqb - 1)
    kt_spec = lambda f: pl.BlockSpec((1, A_KV, HD, WINDOW), lambda b, j: (b, 0, 0, f(j)))
    v_spec = lambda f: pl.BlockSpec((WINDOW, A_KVW), lambda b, j: (row(b, f(j)), 0))
    same = lambda j: j
    return pl.pallas_call(
        _attn_a_lat_kernel,
        grid=(NB_LAT, nqb),
        in_specs=[pl.BlockSpec(memory_space=pltpu.SMEM),
                  pl.BlockSpec((WINDOW, A_W), lambda b, j: (row(b, j), 0)),
                  kt_spec(prev), kt_spec(same), kt_spec(nxt),
                  v_spec(prev), v_spec(same), v_spec(nxt),
                  pl.BlockSpec((1, A_KV, HD, PAST), lambda b, j: (b, 0, 0, 0)),
                  pl.BlockSpec((1, PAST, A_KVW), lambda b, j: (b, 0, 0))],
        out_specs=pl.BlockSpec((WINDOW, A_W), lambda b, j: (b * nqb + j, 0)),
        out_shape=jax.ShapeDtypeStruct((N_LAT_TOK, A_W), F32),
        compiler_params=_cparams(2),
        name="attn_a_lat",
    )(sink, aq, akt_lat, akt_lat, akt_lat, av, av, av, ktx, vx)


B_TQ = 256


def _attn_b_body(lam_init, q, lam_ref, g_ref, segs):
    lp = lam_ref[...]
    lam = (jnp.exp(jnp.sum(lp[0:1, :] * lp[1:2, :], axis=1, keepdims=True))
           - jnp.exp(jnp.sum(lp[2:3, :] * lp[3:4, :], axis=1, keepdims=True)) + lam_init)
    outs = []
    for h in range(B_HEADS):
        maps = []
        for mi in range(2):
            c0 = h * B_DV + mi * B_DQ
            qm = (q[:, c0:c0 + B_DQ] * (B_DQ ** -0.5)).astype(BF16)
            ss = [jnp.dot(qm, kt[0, 2 * h + mi], preferred_element_type=F32) for kt, _ in segs]
            m = None
            for s in ss:
                sm = jnp.max(s, axis=-1, keepdims=True)
                m = sm if m is None else jnp.maximum(m, sm)
            l = None
            o = None
            for s, (_, v) in zip(ss, segs):
                p = jnp.exp(s - m)
                ps = jnp.sum(p, axis=-1, keepdims=True)
                l = ps if l is None else l + ps
                pv = jnp.dot(p.astype(BF16), v[0], preferred_element_type=F32)[:, h * B_DV:(h + 1) * B_DV]
                o = pv if o is None else o + pv
            maps.append(o / l)
        a = maps[0] - lam * maps[1]
        ms = jnp.mean(a * a, axis=-1, keepdims=True)
        outs.append(a * lax.rsqrt(ms + NORM_EPS) * g_ref[...] * (1.0 - lam_init))
    return outs


def _attn_b_ctx_kernel(lam_init, lam_ref, g_ref, q_ref, kt_ref, v_ref, o_ref):
    outs = _attn_b_body(lam_init, q_ref[...], lam_ref, g_ref, [(kt_ref, v_ref)])
    for h in range(B_HEADS):
        o_ref[:, h * B_DV:(h + 1) * B_DV] = outs[h]


def _attn_b_lat_kernel(lam_init, lam_ref, g_ref, q_ref, kt_ref, v_ref, ktx_ref, vx_ref, o_ref):
    outs = _attn_b_body(lam_init, q_ref[...], lam_ref, g_ref,
                        [(kt_ref, v_ref), (ktx_ref, vx_ref)])
    for h in range(B_HEADS):
        o_ref[:, h * B_DV:(h + 1) * B_DV] = outs[h]


def _attn_b_ctx_call(lam_init, lam_p, g, bq, bkt_ctx, bv_ctx):
    return pl.pallas_call(
        functools.partial(_attn_b_ctx_kernel, lam_init),
        grid=(NB_CTX,),
        in_specs=[pl.BlockSpec((4, B_DQ), lambda b: (0, 0)),
                  pl.BlockSpec((1, B_DV), lambda b: (0, 0)),
                  pl.BlockSpec((T_CTX, B_W), lambda b: (b, 0)),
                  pl.BlockSpec((1, 2 * B_HEADS, B_DQ, T_CTX), lambda b: (b, 0, 0, 0)),
                  pl.BlockSpec((1, T_CTX, B_W), lambda b: (b, 0, 0))],
        out_specs=pl.BlockSpec((T_CTX, B_W), lambda b: (b, 0)),
        out_shape=jax.ShapeDtypeStruct((N_CTX_TOK, B_W), F32),
        compiler_params=_cparams(1),
        name="attn_b_ctx",
    )(lam_p, g, bq, bkt_ctx, bv_ctx)


def _attn_b_lat_call(lam_init, lam_p, g, bq, bkt_lat, bv_lat, ktx, vx):
    nq = T_LAT // B_TQ
    ctx_blocks = N_CTX_TOK // B_TQ
    return pl.pallas_call(
        functools.partial(_attn_b_lat_kernel, lam_init),
        grid=(NB_LAT, nq),
        in_specs=[pl.BlockSpec((4, B_DQ), lambda b, j: (0, 0)),
                  pl.BlockSpec((1, B_DV), lambda b, j: (0, 0)),
                  pl.BlockSpec((B_TQ, B_W), lambda b, j: (ctx_blocks + b * nq + j, 0)),
                  pl.BlockSpec((1, 2 * B_HEADS, B_DQ, T_LAT), lambda b, j: (b, 0, 0, 0)),
                  pl.BlockSpec((1, T_LAT, B_W), lambda b, j: (b, 0, 0)),
                  pl.BlockSpec((1, 2 * B_HEADS, B_DQ, PAST), lambda b, j: (b, 0, 0, 0)),
                  pl.BlockSpec((1, PAST, B_W), lambda b, j: (b, 0, 0))],
        out_specs=pl.BlockSpec((B_TQ, B_W), lambda b, j: (b * nq + j, 0)),
        out_shape=jax.ShapeDtypeStruct((N_LAT_TOK, B_W), F32),
        compiler_params=_cparams(2),
        name="attn_b_lat",
    )(lam_p, g, bq, bkt_lat, bv_lat, ktx, vx)


HALO = 8


def _dot_f32(a, b):
    return jnp.dot(a, b, preferred_element_type=F32, precision=lax.Precision.HIGHEST)


def _rwkv_prep_kernel(rkv_ref, prev_ref, next_ref, cw_ref, ca_ref,
                      conv_ref, w0_ref, w2_ref, a0_ref, a2_ref, kk_ref, ka_ref, rk_ref, ones_ref,
                      r_ref, nkk_ref, v_ref, w_f, b_f, kd_f, w_b, b_b, kd_b, bonus_ref):
    i = pl.program_id(0)
    li = _lat_tile(i)
    is_ctx = i < N_CTX_TILES
    has_prev = jnp.logical_and(jnp.logical_not(is_ctx), li > 0).astype(F32)
    has_next = jnp.logical_and(jnp.logical_not(is_ctx), li < LAT_TILES_PER_SEQ - 1).astype(F32)
    x = rkv_ref[...]
    row = lax.broadcasted_iota(jnp.int32, x.shape, 0)
    xm = jnp.where(row == 0, prev_ref[HALO - 1:HALO, :] * has_prev, pltpu.roll(x, 1, axis=0))
    xp = jnp.where(row == TM - 1, next_ref[0:1, :] * has_next, pltpu.roll(x, TM - 1, axis=0))
    y = xm * conv_ref[0:1, :] + x * conv_ref[1:2, :] + xp * conv_ref[2:3, :]
    r, k, v = y[:, :C_W], y[:, C_W:2 * C_W], y[:, 2 * C_W:]
    ones = ones_ref[...]

    kk = k * kk_ref[...]
    kk = kk / jnp.maximum(jnp.sqrt(_dot_f32(kk * kk, ones)), 1e-12)
    lw = _dot_f32(jnp.tanh(cw_ref[...]), w2_ref[...])
    la = _dot_f32(ca_ref[...], a2_ref[...])
    r_ref[...] = r
    nkk_ref[...] = -kk
    v_ref[...] = v
    bonus = jnp.zeros_like(v)
    for d, (w_o, b_o, kd_o) in enumerate(((w_f, b_f, kd_f), (w_b, b_b, kd_b))):
        sl = slice(d * C_W, (d + 1) * C_W)
        w = jnp.exp(-DECAY_SCALE * _sigmoid(w0_ref[d:d + 1, :] + lw[:, sl]))
        a = _sigmoid(a0_ref[d:d + 1, :] + la[:, sl])
        kd = k * (1.0 + (a - 1.0) * ka_ref[...])
        w_o[...] = w
        b_o[...] = kk * a
        kd_o[...] = kd
        bonus = bonus + _dot_f32(r * kd * rk_ref[...], ones) * v
    bonus_ref[...] = bonus


def _rwkv_prep_call(rkv, cw, ca, p):
    nh = TM // HALO
    last = N_TOK // HALO - 1
    full = lambda shape: pl.BlockSpec(shape, lambda i: (0,) * len(shape))
    tile = lambda w: pl.BlockSpec((TM, w), lambda i: (i, 0))
    return pl.pallas_call(
        _rwkv_prep_kernel,
        grid=(N_TILES,),
        in_specs=[tile(3 * C_W),
                  pl.BlockSpec((HALO, 3 * C_W), lambda i: (jnp.maximum(i * nh - 1, 0), 0)),
                  pl.BlockSpec((HALO, 3 * C_W), lambda i: (jnp.minimum((i + 1) * nh, last), 0)),
                  tile(2 * C_RANK), tile(2 * C_RANK),
                  full((3, 3 * C_W)), full((2, C_W)), full((2 * C_RANK, 2 * C_W)),
                  full((2, C_W)), full((2 * C_RANK, 2 * C_W)),
                  full((1, C_W)), full((1, C_W)), full((1, C_W)), full((C_W, C_W))],
        out_specs=[tile(C_W)] * 10,
        out_shape=[jax.ShapeDtypeStruct((N_TOK, C_W), F32)] * 10,
        compiler_params=_cparams(1),
        name="rwkv_prep",
    )(rkv, rkv, rkv, cw, ca, p["conv"], p["w0"], p["w2"], p["a0"], p["a2"],
      p["kk"], p["ka"], p["rk"], p["ones"])


N_BLK = TM // LANES
N_PAIR = C_HEADS // 2


def _scan_kernel(r_f, nkk_f, v_f, w_f, b_f, kd_f, r_b, nkk_b, v_b, w_b, b_b, kd_b, s0f_ref, s0b_ref,
                 yf_ref, yb_ref, sf_ref, sb_ref,
                 st, rows, vt, rp, yp):
    k = pl.program_id(1)
    nk = pl.num_programs(1)
    lane = lax.broadcasted_iota(jnp.int32, (TM, LANES), 1)

    @pl.when(k == 0)
    def _():
        st[...] = jnp.zeros_like(st)
        for d, s0 in enumerate((s0f_ref, s0b_ref)):
            for h in range(C_HEADS):
                o = (h % 2) * C_N
                st[d, h, :, o:o + C_N] = s0[0, h]

    for d, srcs in enumerate(((nkk_f, w_f, b_f, kd_f), (nkk_b, w_b, b_b, kd_b))):
        for a, src in enumerate(srcs):
            for pr in range(N_PAIR):
                chunk = src[:, pr * LANES:(pr + 1) * LANES]
                rows[d, a, 2 * pr] = jnp.where(lane < C_N, chunk, 0.0)
                rows[d, a, 2 * pr + 1] = jnp.where(lane < C_N, 0.0, chunk)
    for d, src in enumerate((v_f, v_b)):
        for pr in range(N_PAIR):
            for blk in range(N_BLK):
                tr = src[blk * LANES:(blk + 1) * LANES, pr * LANES:(pr + 1) * LANES].T
                vt[d, 2 * pr, blk] = tr[0:C_N, :].astype(BF16)
                vt[d, 2 * pr + 1, blk] = tr[C_N:2 * C_N, :].astype(BF16)

    for d, src in enumerate((r_f, r_b)):
        for pr in range(N_PAIR):
            rp[d, pr] = src[:, pr * LANES:(pr + 1) * LANES]

    lane1 = lax.broadcasted_iota(jnp.int32, (1, LANES), 1)

    def emit_y(d, t):
        for pr in range(N_PAIR):
            both = jnp.concatenate([st[d, 2 * pr].astype(BF16), st[d, 2 * pr + 1].astype(BF16)], axis=0)
            r8 = jnp.broadcast_to(rp[d, pr, pl.ds(t, 1), :], (8, LANES)).astype(BF16)
            y8 = lax.dot_general(r8, both, (((1,), (1,)), ((), ())), preferred_element_type=F32)
            yp[d, pr, pl.ds(t, 1), :] = y8[0:1, :]

    def step(i, carry):
        for d in range(2):
            t = i if d == 0 else TM - 1 - i
            emit_y(d, jnp.maximum(i - 1, 0) if d == 0 else jnp.minimum(TM - i, TM - 1))
            blk = t // LANES
            onehot = lane1 == (t % LANES)
            for h in range(C_HEADS):
                s = st[d, h]
                nkk = rows[d, 0, h, pl.ds(t, 1), :]
                w = rows[d, 1, h, pl.ds(t, 1), :]
                b = rows[d, 2, h, pl.ds(t, 1), :]
                kd = rows[d, 3, h, pl.ds(t, 1), :]
                sa = jnp.sum(s * nkk, axis=1, keepdims=True)
                vsel = jnp.where(onehot, vt[d, h, blk], jnp.zeros((), BF16))
                vcol = lax.reduce(vsel, np.array(0, BF16), lax.add, (1,)).reshape(C_N, 1).astype(F32)
                st[d, h] = s * w + sa * b + vcol * kd
        return carry

    lax.fori_loop(0, TM, step, 0, unroll=4)
    emit_y(0, TM - 1)
    emit_y(1, 0)

    for d, y_ref in enumerate((yf_ref, yb_ref)):
        for pr in range(N_PAIR):
            y_ref[:, pr * LANES:(pr + 1) * LANES] = yp[d, pr]

    @pl.when(k == nk - 1)
    def _():
        for d, s_out in enumerate((sf_ref, sb_ref)):
            for h in range(C_HEADS):
                o = (h % 2) * C_N
                s_out[0, h] = st[d, h, :, o:o + C_N]


def _scan_call(name, n_seq, n_tiles, tile0, prep, s0f, s0b):
    r, nkk, v, w_f, b_f, kd_f, w_b, b_b, kd_b = prep
    fwd = pl.BlockSpec((TM, C_W), lambda s, k: (tile0 + s * n_tiles + k, 0))
    bwd = pl.BlockSpec((TM, C_W), lambda s, k: (tile0 + s * n_tiles + (n_tiles - 1 - k), 0))
    st_spec = pl.BlockSpec((1, C_HEADS, C_N, C_N), lambda s, k: (s, 0, 0, 0))
    yf_spec = pl.BlockSpec((TM, C_W), lambda s, k: (s * n_tiles + k, 0))
    yb_spec = pl.BlockSpec((TM, C_W), lambda s, k: (s * n_tiles + (n_tiles - 1 - k), 0))
    n_tok = n_seq * n_tiles * TM
    return pl.pallas_call(
        _scan_kernel,
        grid=(n_seq, n_tiles),
        in_specs=[fwd] * 6 + [bwd] * 6 + [st_spec, st_spec],
        out_specs=[yf_spec, yb_spec, st_spec, st_spec],
        out_shape=[jax.ShapeDtypeStruct((n_tok, C_W), F32)] * 2
                  + [jax.ShapeDtypeStruct((n_seq, C_HEADS, C_N, C_N), F32)] * 2,
        scratch_shapes=[pltpu.VMEM((2, C_HEADS, C_N, LANES), F32),
                        pltpu.VMEM((2, 4, C_HEADS, TM, LANES), F32),
                        pltpu.VMEM((2, C_HEADS, N_BLK, C_N, LANES), BF16),
                        pltpu.VMEM((2, N_PAIR, TM, LANES), F32),
                        pltpu.VMEM((2, N_PAIR, TM, LANES), F32)],
        compiler_params=_cparams(2),
        name=name,
    )(r, nkk, v, w_f, b_f, kd_f, r, nkk, v, w_b, b_b, kd_b, s0f, s0b)


def _outproj_kernel(x_ref, mod_ref, ao_ref, bo_ref, yf_ref, yb_ref, bonus_ref, cg_ref,
                    lng_ref, lnb_ref, g2_ref, mean_ref, wa_ref, wb_ref, wc_ref, o_ref):
    y = yf_ref[...] + yb_ref[...] + bonus_ref[...]
    mean_m = mean_ref[...]
    mu = _dot_f32(y, mean_m)
    dy = y - mu
    var = _dot_f32(dy * dy, mean_m)
    yn = dy * lax.rsqrt(var + GN_EPS) * lng_ref[...] + lnb_ref[...]
    gate = jnp.dot(_sigmoid(cg_ref[...]).astype(BF16), g2_ref[...], preferred_element_type=F32)
    co = yn * gate
    m = (jnp.dot(ao_ref[...].astype(BF16), wa_ref[...], preferred_element_type=F32)
         + jnp.dot(bo_ref[...].astype(BF16), wb_ref[...], preferred_element_type=F32)
         + jnp.dot(co.astype(BF16), wc_ref[...], preferred_element_type=F32))
    o_ref[...] = x_ref[...] + mod_ref[0, 2:3, :] * m


def _outproj_call(x, mod_l, ao, bo, yf, yb, bonus, cg, p):
    full = lambda shape: pl.BlockSpec(shape, lambda i: (0,) * len(shape))
    tile = lambda w: pl.BlockSpec((TM, w), lambda i: (i, 0))
    return pl.pallas_call(
        _outproj_kernel,
        grid=(N_TILES,),
        in_specs=[tile(D_MODEL),
                  pl.BlockSpec((1, 8, D_MODEL), lambda i: (_tile_group(i), 0, 0)),
                  tile(A_W), tile(B_W), tile(C_W), tile(C_W), tile(C_W), tile(C_G_RANK),
                  full((1, C_W)), full((1, C_W)), full((C_G_RANK, C_W)), full((C_W, C_W)),
                  full((A_W, D_MODEL)), full((B_W, D_MODEL)), full((C_W, D_MODEL))],
        out_specs=tile(D_MODEL),
        out_shape=jax.ShapeDtypeStruct((N_TOK, D_MODEL), F32),
        compiler_params=_cparams(1),
        name="outproj",
    )(x, mod_l, ao, bo, yf, yb, bonus, cg, p["lnx_g"], p["lnx_b"], p["g2"], p["mean"],
      p["wa"], p["wb"], p["wc"])


def _ffn_kernel(final, x_ref, mod_ref, g_ref, w1_ref, w3_ref, w2_ref, fg_ref, o_ref):
    x = x_ref[...]
    h = _modulated_norm(x, g_ref[...], mod_ref[0, 3:4, :], mod_ref[0, 4:5, :]).astype(BF16)
    u = jnp.dot(h, w1_ref[...], preferred_element_type=F32)
    t = jnp.dot(h, w3_ref[...], preferred_element_type=F32)
    act = (u * _sigmoid(u) * t).astype(BF16)
    y = x + mod_ref[0, 5:6, :] * jnp.dot(act, w2_ref[...], preferred_element_type=F32)
    if final:
        ms = jnp.mean(y * y, axis=-1, keepdims=True)
        y = y * lax.rsqrt(ms + NORM_EPS) * fg_ref[...]
    o_ref[...] = y


def _ffn_call(final, x, mod_l, g, w1, w3, w2, final_g):
    once = lambda shape: pl.BlockSpec(shape, lambda i: (0,) * len(shape),
                                      pipeline_mode=pl.Buffered(1))
    return pl.pallas_call(
        functools.partial(_ffn_kernel, final),
        grid=(N_TILES,),
        in_specs=[pl.BlockSpec((TM, D_MODEL), lambda i: (i, 0)),
                  pl.BlockSpec((1, 8, D_MODEL), lambda i: (_tile_group(i), 0, 0)),
                  once((1, D_MODEL)),
                  once((D_MODEL, D_FF)), once((D_MODEL, D_FF)), once((D_FF, D_MODEL)),
                  once((1, D_MODEL))],
        out_specs=pl.BlockSpec((TM, D_MODEL), lambda i: (i, 0)),
        out_shape=jax.ShapeDtypeStruct((N_TOK, D_MODEL), F32),
        compiler_params=_cparams(1),
        name="ffn",
    )(x, mod_l, g, w1, w3, w2, final_g)


def _block_diag2(m):
    z = jnp.zeros_like(m[0])
    return jnp.concatenate([jnp.concatenate([m[0], z], axis=1),
                            jnp.concatenate([z, m[1]], axis=1)], axis=0)


def _keys_t(k, nb, t, heads, dim):
    return k.reshape(nb, t, heads, dim).transpose(0, 2, 3, 1).astype(BF16)


def kernel(x_prompt, x_sample, cache_a_k, cache_a_v, cache_b_k, cache_b_v, state_c_fwd, state_c_bwd,
           c, c_ctx, ada_w, ada_b, norm1_g, norm2_g, w_in, a_sink, b_lambda, b_subln_g,
           c_conv, c_w0, c_w2, c_a0, c_a2, c_g2, c_kk, c_ka, c_rk, c_lnx_g, c_lnx_b,
           w_out, ffn_w1, ffn_w3, ffn_w2, final_g):
    x = jnp.concatenate([x_prompt.reshape(N_CTX_TOK, D_MODEL),
                         x_sample.reshape(N_LAT_TOK, D_MODEL)], axis=0)
    cvec8 = jnp.concatenate([c_ctx[None], c, jnp.zeros((8 - 1 - NB_LAT, D_MODEL), F32)], axis=0)
    mod = _ada_call(cvec8, ada_w, ada_b)
    mod = mod[:, :1 + NB_LAT].reshape(DEPTH, 1 + NB_LAT, 6, D_MODEL)
    mod = jnp.pad(mod, ((0, 0), (0, 0), (0, 2), (0, 0)))

    tabs = _rope_tables()
    head_id = np.arange(C_W) // C_N
    block_ones = jnp.asarray(head_id[:, None] == head_id[None, :], F32)
    final_g2 = final_g.reshape(1, D_MODEL)
    s0_ctx = jnp.zeros((NB_CTX, C_HEADS, C_N, C_N), F32)

    new_ak, new_av, new_bk, new_bv, new_sf, new_sb = [], [], [], [], [], []
    for l in range(DEPTH):
        lam_init = 0.8 - 0.6 * math.exp(-0.3 * l)
        aq, ak, av, bq, bk, bv, rkv, cw, ca, cg = _inproj_call(
            x, mod[l], norm1_g[l].reshape(1, D_MODEL), w_in[l].astype(BF16), tabs)

        new_ak.append(ak[:N_CTX_TOK].reshape(NB_CTX, T_CTX, A_KV, HD))
        new_av.append(av[:N_CTX_TOK].reshape(NB_CTX, T_CTX, A_KV, HD))
        new_bk.append(bk[:N_CTX_TOK].reshape(NB_CTX, T_CTX, B_HEADS, 2, B_DQ))
        new_bv.append(bv[:N_CTX_TOK].reshape(NB_CTX, T_CTX, B_HEADS, B_DV))

        sink = a_sink[l]
        ao_ctx = _attn_a_ctx_call(sink, aq, _keys_t(ak[:N_CTX_TOK], NB_CTX, T_CTX, A_KV, HD), av)
        ao_lat = _attn_a_lat_call(
            sink, aq, _keys_t(ak[N_CTX_TOK:], NB_LAT, T_LAT, A_KV, HD), av,
            _keys_t(cache_a_k[:, l].reshape(NB_LAT * PAST, A_KVW), NB_LAT, PAST, A_KV, HD),
            cache_a_v[:, l].reshape(NB_LAT, PAST, A_KVW))
        ao = jnp.concatenate([ao_ctx, ao_lat], axis=0)

        lam_p, sub_g = b_lambda[l], b_subln_g[l].reshape(1, B_DV)
        bo_ctx = _attn_b_ctx_call(
            lam_init, lam_p, sub_g, bq, _keys_t(bk[:N_CTX_TOK], NB_CTX, T_CTX, 2 * B_HEADS, B_DQ),
            bv[:N_CTX_TOK].reshape(NB_CTX, T_CTX, B_W).astype(BF16))
        bo_lat = _attn_b_lat_call(
            lam_init, lam_p, sub_g, bq, _keys_t(bk[N_CTX_TOK:], NB_LAT, T_LAT, 2 * B_HEADS, B_DQ),
            bv[N_CTX_TOK:].reshape(NB_LAT, T_LAT, B_W).astype(BF16),
            _keys_t(cache_b_k[:, l].reshape(NB_LAT * PAST, B_W), NB_LAT, PAST, 2 * B_HEADS, B_DQ),
            cache_b_v[:, l].reshape(NB_LAT, PAST, B_W).astype(BF16))
        bo = jnp.concatenate([bo_ctx, bo_lat], axis=0)

        prep = _rwkv_prep_call(rkv, cw, ca, dict(
            conv=c_conv[l], w0=c_w0[l], w2=_block_diag2(c_w2[l]), a0=c_a0[l], a2=_block_diag2(c_a2[l]),
            kk=c_kk[l].reshape(1, C_W), ka=c_ka[l].reshape(1, C_W), rk=c_rk[l].reshape(1, C_W),
            ones=block_ones))
        r, nkk, v, w_f, b_f, kd_f, w_b, b_b, kd_b, bonus = prep
        scan_in = (r, nkk, v, w_f, b_f, kd_f, w_b, b_b, kd_b)
        yf_c, yb_c, sf, sb = _scan_call("scan_ctx", NB_CTX, 1, 0, scan_in, s0_ctx, s0_ctx)
        yf_l, yb_l, _, _ = _scan_call("scan_lat", NB_LAT, LAT_TILES_PER_SEQ, N_CTX_TILES, scan_in,
                                      state_c_fwd[:, l], state_c_bwd[:, l])
        new_sf.append(sf)
        new_sb.append(sb)
        yf = jnp.concatenate([yf_c, yf_l], axis=0)
        yb = jnp.concatenate([yb_c, yb_l], axis=0)

        wo = w_out[l].astype(BF16)
        x = _outproj_call(x, mod[l], ao, bo, yf, yb, bonus, cg, dict(
            lnx_g=c_lnx_g[l].reshape(1, C_W), lnx_b=c_lnx_b[l].reshape(1, C_W),
            g2=c_g2[l].astype(BF16), mean=block_ones / C_N,
            wa=wo[:A_W], wb=wo[A_W:A_W + B_W], wc=wo[A_W + B_W:]))

        x = _ffn_call(l == DEPTH - 1, x, mod[l], norm2_g[l].reshape(1, D_MODEL),
                      ffn_w1[l].astype(BF16), ffn_w3[l].astype(BF16), ffn_w2[l].astype(BF16), final_g2)

    y_prompt = x[:N_CTX_TOK].reshape(NB_CTX, T_CTX, D_MODEL)
    y_sample = x[N_CTX_TOK:].reshape(NB_LAT, T_LAT, D_MODEL)
    return (y_prompt, y_sample,
            jnp.stack(new_ak, axis=1), jnp.stack(new_av, axis=1),
            jnp.stack(new_bk, axis=1), jnp.stack(new_bv, axis=1),
            jnp.stack(new_sf, axis=1), jnp.stack(new_sb, axis=1))
```

```python
import functools
import math

import numpy as np
import jax
import jax.numpy as jnp
from jax import lax
from jax.experimental import pallas as pl
from jax.experimental.pallas import tpu as pltpu

F32 = jnp.float32
BF16 = jnp.bfloat16

D_MODEL = 1024
DEPTH = 4
NB_CTX, T_CTX = 16, 256
NB_LAT, T_LAT = 2, 4096
PAST = 512
GRID_W = 64
HD = 64
A_HEADS, A_KV = 6, 2
A_W = A_HEADS * HD
A_KVW = A_KV * HD
WINDOW = 128
B_HEADS, B_DQ, B_DV = 4, 32, 64
B_W = B_HEADS * B_DV
C_HEADS, C_N = 6, 64
C_W = C_HEADS * C_N
C_RANK = 64
C_G_RANK = 128
D_FF = 2816
IN_COLS = 2944
ROPE_THETA = 10000.0
NORM_EPS = 1e-6
GN_EPS = 64e-5
DECAY_SCALE = 0.606531
NEG_INF = -1e30

TM = 256
N_CTX_TOK = NB_CTX * T_CTX
N_LAT_TOK = NB_LAT * T_LAT
N_TOK = N_CTX_TOK + N_LAT_TOK
N_CTX_TILES = N_CTX_TOK // TM
N_TILES = N_TOK // TM
LAT_TILES_PER_SEQ = T_LAT // TM
LANES = 128
VMEM_LIMIT = 56 * 1024 * 1024

O_AQ, O_AK, O_AV, O_BQ, O_BK, O_BV, O_RKV, O_CW, O_CA, O_CG = (
    0, 384, 512, 640, 896, 1152, 1408, 2560, 2688, 2816)


def _cparams(n_grid):
    return pltpu.CompilerParams(dimension_semantics=("arbitrary",) * n_grid,
                                vmem_limit_bytes=VMEM_LIMIT)


def _sigmoid(x):
    return 1.0 / (1.0 + jnp.exp(-x))


def _tile_group(i):
    return jnp.where(i < N_CTX_TILES, 0, 1 + (i - N_CTX_TILES) // LAT_TILES_PER_SEQ)


def _lat_tile(i):
    return jnp.maximum(i - N_CTX_TILES, 0) % LAT_TILES_PER_SEQ


ADA_TN = 1536


def _ada_kernel(c_ref, w_ref, b_ref, o_ref):
    c = c_ref[...]
    s = c * _sigmoid(c)
    o_ref[0] = jnp.dot(s.astype(BF16), w_ref[0].astype(BF16),
                       preferred_element_type=F32) + b_ref[0]


def _ada_call(cvec8, ada_w, ada_b):
    n = 6 * D_MODEL
    return pl.pallas_call(
        _ada_kernel,
        grid=(DEPTH, n // ADA_TN),
        in_specs=[pl.BlockSpec((8, D_MODEL), lambda l, j: (0, 0)),
                  pl.BlockSpec((1, D_MODEL, ADA_TN), lambda l, j: (l, 0, j)),
                  pl.BlockSpec((1, 1, ADA_TN), lambda l, j: (l, 0, j))],
        out_specs=pl.BlockSpec((1, 8, ADA_TN), lambda l, j: (l, 0, j)),
        out_shape=jax.ShapeDtypeStruct((DEPTH, 8, n), F32),
        compiler_params=_cparams(2),
        name="ada",
    )(cvec8, ada_w, ada_b.reshape(DEPTH, 1, n))


def _modulated_norm(x, g, shift, scale):
    ms = jnp.mean(x * x, axis=-1, keepdims=True)
    h = x * lax.rsqrt(ms + NORM_EPS) * g
    return h * (1.0 + scale) + shift


def _rope_chunk(x, cos, sin_lo, sin_hi, half):
    up = pltpu.roll(x, LANES - half, axis=1)
    dn = pltpu.roll(x, half, axis=1)
    return x * cos + up * sin_lo + dn * sin_hi


def _inproj_kernel(x_ref, mod_ref, g_ref, w_ref,
                   cos_a, sl_a, sh_a, cos_b, sl_b, sh_b,
                   aq_ref, ak_ref, av_ref, bq_ref, bk_ref, bv_ref,
                   rkv_ref, cw_ref, ca_ref, cg_ref):
    i = pl.program_id(0)
    h = _modulated_norm(x_ref[...], g_ref[...], mod_ref[0, 0:1, :], mod_ref[0, 1:2, :])
    z = jnp.dot(h.astype(BF16), w_ref[...], preferred_element_type=F32)
    av_ref[...] = z[:, O_AV:O_BQ]
    bv_ref[...] = z[:, O_BV:O_RKV]
    rkv_ref[...] = z[:, O_RKV:O_CW]
    cw_ref[...] = z[:, O_CW:O_CA]
    ca_ref[...] = z[:, O_CA:O_CG]
    cg_ref[...] = z[:, O_CG:IN_COLS]

    @pl.when(i < N_CTX_TILES)
    def _():
        aq_ref[...] = z[:, O_AQ:O_AK]
        ak_ref[...] = z[:, O_AK:O_AV]
        bq_ref[...] = z[:, O_BQ:O_BK]
        bk_ref[...] = z[:, O_BK:O_BV]

    @pl.when(i >= N_CTX_TILES)
    def _():
        ca_, la_, ha_ = cos_a[...], sl_a[...], sh_a[...]
        cb_, lb_, hb_ = cos_b[...], sl_b[...], sh_b[...]
        for j in range(A_W // LANES):
            o = O_AQ + j * LANES
            aq_ref[:, j * LANES:(j + 1) * LANES] = _rope_chunk(z[:, o:o + LANES], ca_, la_, ha_, 16)
        ak_ref[...] = _rope_chunk(z[:, O_AK:O_AV], ca_, la_, ha_, 16)
        for j in range(B_W // LANES):
            o = O_BQ + j * LANES
            bq_ref[:, j * LANES:(j + 1) * LANES] = _rope_chunk(z[:, o:o + LANES], cb_, lb_, hb_, 8)
            o = O_BK + j * LANES
            bk_ref[:, j * LANES:(j + 1) * LANES] = _rope_chunk(z[:, o:o + LANES], cb_, lb_, hb_, 8)


def _inproj_call(x, mod_l, g, w_bf16, tabs):
    widths = (A_W, A_KVW, A_KVW, B_W, B_W, B_W, 3 * C_W, 2 * C_RANK, 2 * C_RANK, C_G_RANK)
    tab_spec = pl.BlockSpec((TM, LANES), lambda i: (_lat_tile(i), 0))
    return pl.pallas_call(
        _inproj_kernel,
        grid=(N_TILES,),
        in_specs=[pl.BlockSpec((TM, D_MODEL), lambda i: (i, 0)),
                  pl.BlockSpec((1, 8, D_MODEL), lambda i: (_tile_group(i), 0, 0)),
                  pl.BlockSpec((1, D_MODEL), lambda i: (0, 0)),
                  pl.BlockSpec((D_MODEL, IN_COLS), lambda i: (0, 0))] + [tab_spec] * 6,
        out_specs=[pl.BlockSpec((TM, w), lambda i: (i, 0)) for w in widths],
        out_shape=[jax.ShapeDtypeStruct((N_TOK, w), F32) for w in widths],
        compiler_params=_cparams(1),
        name="inproj",
    )(x, mod_l, g, w_bf16, *tabs)


def _rope_tables():
    t = np.arange(T_LAT)
    rows, cols = t // GRID_W, t % GRID_W

    def build(width):
        half = width // 4
        d = width // 2
        inv = ROPE_THETA ** (-jnp.arange(0, d, 2, dtype=F32) / d)
        lane = np.arange(LANES) % width
        part = lane // d
        p = lane % d
        f = p % half
        pos = jnp.where(jnp.asarray(part)[None, :] == 0,
                        jnp.asarray(rows, F32)[:, None], jnp.asarray(cols, F32)[:, None])
        ang = pos * inv[jnp.asarray(f)][None, :]
        cos, sin = jnp.cos(ang), jnp.sin(ang)
        lo = jnp.asarray(p < half)[None, :]
        return cos, jnp.where(lo, -sin, 0.0), jnp.where(lo, 0.0, sin)

    return build(HD) + build(B_DQ)


def _attend_a(q, sink_ref, segs):
    outs = []
    for h in range(A_HEADS):
        g = h // (A_HEADS // A_KV)
        qh = (q[:, h * HD:(h + 1) * HD] * (HD ** -0.5)).astype(BF16)
        sink = sink_ref[h]
        ss = []
        m = None
        for kt, _, mask in segs:
            s = jnp.dot(qh, kt(g), preferred_element_type=F32)
            if mask is not None:
                s = jnp.where(mask, s, NEG_INF)
            ss.append(s)
            sm = jnp.max(s, axis=-1, keepdims=True)
            m = sm if m is None else jnp.maximum(m, sm)
        m = jnp.maximum(m, sink)
        l = jnp.exp(sink - m)
        o = None
        for s, (_, v, _) in zip(ss, segs):
            p = jnp.exp(s - m)
            l = l + jnp.sum(p, axis=-1, keepdims=True)
            pv = jnp.dot(p.astype(BF16), v().astype(BF16), preferred_element_type=F32)[:, g * HD:(g + 1) * HD]
            o = pv if o is None else o + pv
        outs.append(o / l)
    return outs


def _attn_a_ctx_kernel(sink_ref, q_ref, kt_ref, v_ref, o_ref):
    outs = _attend_a(q_ref[...], sink_ref, [(lambda g: kt_ref[0, g], lambda: v_ref[...], None)])
    for h in range(A_HEADS):
        o_ref[:, h * HD:(h + 1) * HD] = outs[h]


def _attn_a_lat_kernel(sink_ref, q_ref, ktp_ref, ktc_ref, ktn_ref, vp_ref, vc_ref, vn_ref,
                       ktx_ref, vx_ref, _ctx_rows_ref, o_ref):
    qb = pl.program_id(1)
    nqb = pl.num_programs(1)
    qi = lax.broadcasted_iota(jnp.int32, (WINDOW, WINDOW), 0)
    kj = lax.broadcasted_iota(jnp.int32, (WINDOW, WINDOW), 1)
    mask_prev = (kj >= qi) & (qb > 0)
    mask_next = (kj <= qi) & (qb < nqb - 1)
    segs = [(lambda g: ktp_ref[0, g], lambda: vp_ref[...], mask_prev),
            (lambda g: ktc_ref[0, g], lambda: vc_ref[...], None),
            (lambda g: ktn_ref[0, g], lambda: vn_ref[...], mask_next),
            (lambda g: ktx_ref[0, g], lambda: vx_ref[0], None)]
    outs = _attend_a(q_ref[...], sink_ref, segs)
    for h in range(A_HEADS):
        o_ref[:, h * HD:(h + 1) * HD] = outs[h]


def _attn_a_ctx_call(sink, aq, akt_ctx, av):
    return pl.pallas_call(
        _attn_a_ctx_kernel,
        grid=(NB_CTX,),
        in_specs=[pl.BlockSpec(memory_space=pltpu.SMEM),
                  pl.BlockSpec((T_CTX, A_W), lambda b: (b, 0)),
                  pl.BlockSpec((1, A_KV, HD, T_CTX), lambda b: (b, 0, 0, 0)),
                  pl.BlockSpec((T_CTX, A_KVW), lambda b: (b, 0))],
        out_specs=pl.BlockSpec((T_CTX, A_W), lambda b: (b, 0)),
        out_shape=jax.ShapeDtypeStruct((N_TOK, A_W), F32),
        compiler_params=_cparams(1),
        name="attn_a_ctx",
    )(sink, aq, akt_ctx, av)


def _attn_a_lat_call(sink, aq, akt_lat, av, ktx, vx, ao_ctx):
    nqb = T_LAT // WINDOW
    ctx_blocks = N_CTX_TOK // WINDOW
    row = lambda b, j: ctx_blocks + b * nqb + j
    prev = lambda j: jnp.maximum(j - 1, 0)
    nxt = lambda j: jnp.minimum(j + 1, nqb - 1)
    kt_spec = lambda f: pl.BlockSpec((1, A_KV, HD, WINDOW), lambda b, j: (b, 0, 0, f(j)))
    v_spec = lambda f: pl.BlockSpec((WINDOW, A_KVW), lambda b, j: (row(b, f(j)), 0))
    same = lambda j: j
    return pl.pallas_call(
        _attn_a_lat_kernel,
        grid=(NB_LAT, nqb),
        in_specs=[pl.BlockSpec(memory_space=pltpu.SMEM),
                  pl.BlockSpec((WINDOW, A_W), lambda b, j: (row(b, j), 0)),
                  kt_spec(prev), kt_spec(same), kt_spec(nxt),
                  v_spec(prev), v_spec(same), v_spec(nxt),
                  pl.BlockSpec((1, A_KV, HD, PAST), lambda b, j: (b, 0, 0, 0)),
                  pl.BlockSpec((1, PAST, A_KVW), lambda b, j: (b, 0, 0)),
                  pl.BlockSpec(memory_space=pl.ANY)],
        out_specs=pl.BlockSpec((WINDOW, A_W), lambda b, j: (row(b, j), 0)),
        out_shape=jax.ShapeDtypeStruct((N_TOK, A_W), F32),
        input_output_aliases={10: 0},
        compiler_params=_cparams(2),
        name="attn_a_lat",
    )(sink, aq, akt_lat, akt_lat, akt_lat, av, av, av, ktx, vx, ao_ctx)


B_TQ = 256


def _attn_b_body(lam_init, q, lam_ref, g_ref, segs):
    lp = lam_ref[...]
    lam = (jnp.exp(jnp.sum(lp[0:1, :] * lp[1:2, :], axis=1, keepdims=True))
           - jnp.exp(jnp.sum(lp[2:3, :] * lp[3:4, :], axis=1, keepdims=True)) + lam_init)
    outs = []
    for h in range(B_HEADS):
        maps = []
        for mi in range(2):
            c0 = h * B_DV + mi * B_DQ
            qm = (q[:, c0:c0 + B_DQ] * (B_DQ ** -0.5)).astype(BF16)
            ss = [jnp.dot(qm, kt[0, 2 * h + mi], preferred_element_type=F32) for kt, _ in segs]
            m = None
            for s in ss:
                sm = jnp.max(s, axis=-1, keepdims=True)
                m = sm if m is None else jnp.maximum(m, sm)
            l = None
            o = None
            for s, (_, v) in zip(ss, segs):
                p = jnp.exp(s - m)
                ps = jnp.sum(p, axis=-1, keepdims=True)
                l = ps if l is None else l + ps
                pv = jnp.dot(p.astype(BF16), v[0], preferred_element_type=F32)[:, h * B_DV:(h + 1) * B_DV]
                o = pv if o is None else o + pv
            maps.append(o / l)
        a = maps[0] - lam * maps[1]
        ms = jnp.mean(a * a, axis=-1, keepdims=True)
        outs.append(a * lax.rsqrt(ms + NORM_EPS) * g_ref[...] * (1.0 - lam_init))
    return outs


def _attn_b_ctx_kernel(lam_init, lam_ref, g_ref, q_ref, kt_ref, v_ref, o_ref):
    outs = _attn_b_body(lam_init, q_ref[...], lam_ref, g_ref, [(kt_ref, v_ref)])
    for h in range(B_HEADS):
        o_ref[:, h * B_DV:(h + 1) * B_DV] = outs[h]


def _attn_b_lat_kernel(lam_init, lam_ref, g_ref, q_ref, kt_ref, v_ref, ktx_ref, vx_ref,
                       _ctx_rows_ref, o_ref):
    outs = _attn_b_body(lam_init, q_ref[...], lam_ref, g_ref,
                        [(kt_ref, v_ref), (ktx_ref, vx_ref)])
    for h in range(B_HEADS):
        o_ref[:, h * B_DV:(h + 1) * B_DV] = outs[h]


def _attn_b_ctx_call(lam_init, lam_p, g, bq, bkt_ctx, bv_ctx):
    return pl.pallas_call(
        functools.partial(_attn_b_ctx_kernel, lam_init),
        grid=(NB_CTX,),
        in_specs=[pl.BlockSpec((4, B_DQ), lambda b: (0, 0)),
                  pl.BlockSpec((1, B_DV), lambda b: (0, 0)),
                  pl.BlockSpec((T_CTX, B_W), lambda b: (b, 0)),
                  pl.BlockSpec((1, 2 * B_HEADS, B_DQ, T_CTX), lambda b: (b, 0, 0, 0)),
                  pl.BlockSpec((1, T_CTX, B_W), lambda b: (b, 0, 0))],
        out_specs=pl.BlockSpec((T_CTX, B_W), lambda b: (b, 0)),
        out_shape=jax.ShapeDtypeStruct((N_TOK, B_W), F32),
        compiler_params=_cparams(1),
        name="attn_b_ctx",
    )(lam_p, g, bq, bkt_ctx, bv_ctx)


def _attn_b_lat_call(lam_init, lam_p, g, bq, bkt_lat, bv_lat, ktx, vx, bo_ctx):
    nq = T_LAT // B_TQ
    ctx_blocks = N_CTX_TOK // B_TQ
    return pl.pallas_call(
        functools.partial(_attn_b_lat_kernel, lam_init),
        grid=(NB_LAT, nq),
        in_specs=[pl.BlockSpec((4, B_DQ), lambda b, j: (0, 0)),
                  pl.BlockSpec((1, B_DV), lambda b, j: (0, 0)),
                  pl.BlockSpec((B_TQ, B_W), lambda b, j: (ctx_blocks + b * nq + j, 0)),
                  pl.BlockSpec((1, 2 * B_HEADS, B_DQ, T_LAT), lambda b, j: (b, 0, 0, 0)),
                  pl.BlockSpec((1, T_LAT, B_W), lambda b, j: (b, 0, 0)),
                  pl.BlockSpec((1, 2 * B_HEADS, B_DQ, PAST), lambda b, j: (b, 0, 0, 0)),
                  pl.BlockSpec((1, PAST, B_W), lambda b, j: (b, 0, 0)),
                  pl.BlockSpec(memory_space=pl.ANY)],
        out_specs=pl.BlockSpec((B_TQ, B_W), lambda b, j: (ctx_blocks + b * nq + j, 0)),
        out_shape=jax.ShapeDtypeStruct((N_TOK, B_W), F32),
        input_output_aliases={7: 0},
        compiler_params=_cparams(2),
        name="attn_b_lat",
    )(lam_p, g, bq, bkt_lat, bv_lat, ktx, vx, bo_ctx)


HALO = 8


def _head_sum(x, ones_bf16):
    hi = x.astype(BF16)
    lo = (x - hi.astype(F32)).astype(BF16)
    return (jnp.dot(hi, ones_bf16, preferred_element_type=F32)
            + jnp.dot(lo, ones_bf16, preferred_element_type=F32))


def _rwkv_prep_kernel(rkv_ref, prev_ref, next_ref, cw_ref, ca_ref,
                      conv_ref, w0_ref, w2_ref, a0_ref, a2_ref, kk_ref, ka_ref, rk_ref, ones_ref,
                      r_ref, nkk_ref, v_ref, w_f, b_f, kd_f, w_b, b_b, kd_b, bonus_ref):
    i = pl.program_id(0)
    li = _lat_tile(i)
    is_ctx = i < N_CTX_TILES
    has_prev = jnp.logical_and(jnp.logical_not(is_ctx), li > 0).astype(F32)
    has_next = jnp.logical_and(jnp.logical_not(is_ctx), li < LAT_TILES_PER_SEQ - 1).astype(F32)
    x = rkv_ref[...]
    row = lax.broadcasted_iota(jnp.int32, x.shape, 0)
    xm = jnp.where(row == 0, prev_ref[HALO - 1:HALO, :] * has_prev, pltpu.roll(x, 1, axis=0))
    xp = jnp.where(row == TM - 1, next_ref[0:1, :] * has_next, pltpu.roll(x, TM - 1, axis=0))
    y = xm * conv_ref[0:1, :] + x * conv_ref[1:2, :] + xp * conv_ref[2:3, :]
    r, k, v = y[:, :C_W], y[:, C_W:2 * C_W], y[:, 2 * C_W:]
    ones = ones_ref[...]

    kk = k * kk_ref[...]
    kk = kk / jnp.maximum(jnp.sqrt(_head_sum(kk * kk, ones)), 1e-12)
    lw = jnp.dot(jnp.tanh(cw_ref[...]).astype(BF16), w2_ref[...],
                 preferred_element_type=F32)
    la = jnp.dot(ca_ref[...].astype(BF16), a2_ref[...], preferred_element_type=F32)
    r_ref[...] = r
    nkk_ref[...] = -kk
    v_ref[...] = v
    bonus = jnp.zeros_like(v)
    for d, (w_o, b_o, kd_o) in enumerate(((w_f, b_f, kd_f), (w_b, b_b, kd_b))):
        sl = slice(d * C_W, (d + 1) * C_W)
        w = jnp.exp(-DECAY_SCALE * _sigmoid(w0_ref[d:d + 1, :] + lw[:, sl]))
        a = _sigmoid(a0_ref[d:d + 1, :] + la[:, sl])
        kd = k * (1.0 + (a - 1.0) * ka_ref[...])
        w_o[...] = w
        b_o[...] = kk * a
        kd_o[...] = kd
        bonus = bonus + _head_sum(r * kd * rk_ref[...], ones) * v
    bonus_ref[...] = bonus


def _rwkv_prep_call(rkv, cw, ca, p):
    nh = TM // HALO
    last = N_TOK // HALO - 1
    full = lambda shape: pl.BlockSpec(shape, lambda i: (0,) * len(shape))
    tile = lambda w: pl.BlockSpec((TM, w), lambda i: (i, 0))
    return pl.pallas_call(
        _rwkv_prep_kernel,
        grid=(N_TILES,),
        in_specs=[tile(3 * C_W),
                  pl.BlockSpec((HALO, 3 * C_W), lambda i: (jnp.maximum(i * nh - 1, 0), 0)),
                  pl.BlockSpec((HALO, 3 * C_W), lambda i: (jnp.minimum((i + 1) * nh, last), 0)),
                  tile(2 * C_RANK), tile(2 * C_RANK),
                  full((3, 3 * C_W)), full((2, C_W)), full((2 * C_RANK, 2 * C_W)),
                  full((2, C_W)), full((2 * C_RANK, 2 * C_W)),
                  full((1, C_W)), full((1, C_W)), full((1, C_W)), full((C_W, C_W))],
        out_specs=[tile(C_W)] * 10,
        out_shape=[jax.ShapeDtypeStruct((N_TOK, C_W), F32)] * 10,
        compiler_params=_cparams(1),
        name="rwkv_prep",
    )(rkv, rkv, rkv, cw, ca, p["conv"], p["w0"], p["w2"], p["a0"], p["a2"],
      p["kk"], p["ka"], p["rk"], p["ones"])


N_BLK = TM // LANES
N_PAIR = C_HEADS // 2


def _scan_kernel(r_f, nkk_f, v_f, w_f, b_f, kd_f, r_b, nkk_b, v_b, w_b, b_b, kd_b, s0f_ref, s0b_ref,
                 yf_ref, yb_ref, sf_ref, sb_ref,
                 st, rows, vt, rp, yp):
    k = pl.program_id(1)
    nk = pl.num_programs(1)
    lane = lax.broadcasted_iota(jnp.int32, (TM, LANES), 1)

    @pl.when(k == 0)
    def _():
        st[...] = jnp.zeros_like(st)
        for d, s0 in enumerate((s0f_ref, s0b_ref)):
            for h in range(C_HEADS):
                o = (h % 2) * C_N
                st[d, h, :, o:o + C_N] = s0[0, h]

    for d, srcs in enumerate(((nkk_f, w_f, b_f, kd_f), (nkk_b, w_b, b_b, kd_b))):
        for a, src in enumerate(srcs):
            for pr in range(N_PAIR):
                chunk = src[:, pr * LANES:(pr + 1) * LANES]
                rows[d, a, 2 * pr] = jnp.where(lane < C_N, chunk, 0.0)
                rows[d, a, 2 * pr + 1] = jnp.where(lane < C_N, 0.0, chunk)
    for d, src in enumerate((v_f, v_b)):
        for pr in range(N_PAIR):
            for blk in range(N_BLK):
                tr = src[blk * LANES:(blk + 1) * LANES, pr * LANES:(pr + 1) * LANES].T
                vt[d, 2 * pr, blk] = tr[0:C_N, :].astype(BF16)
                vt[d, 2 * pr + 1, blk] = tr[C_N:2 * C_N, :].astype(BF16)

    for d, src in enumerate((r_f, r_b)):
        for pr in range(N_PAIR):
            rp[d, pr] = src[:, pr * LANES:(pr + 1) * LANES]

    lane1 = lax.broadcasted_iota(jnp.int32, (1, LANES), 1)

    def emit_y(d, t):
        for pr in range(N_PAIR):
            both = jnp.concatenate([st[d, 2 * pr].astype(BF16), st[d, 2 * pr + 1].astype(BF16)], axis=0)
            r8 = jnp.broadcast_to(rp[d, pr, pl.ds(t, 1), :], (8, LANES)).astype(BF16)
            y8 = lax.dot_general(r8, both, (((1,), (1,)), ((), ())), preferred_element_type=F32)
            yp[d, pr, pl.ds(t, 1), :] = y8[0:1, :]

    def step(i, carry):
        for d in range(2):
            t = i if d == 0 else TM - 1 - i
            emit_y(d, jnp.maximum(i - 1, 0) if d == 0 else jnp.minimum(TM - i, TM - 1))
            blk = t // LANES
            onehot = lane1 == (t % LANES)
            for h in range(C_HEADS):
                s = st[d, h]
                nkk = rows[d, 0, h, pl.ds(t, 1), :]
                w = rows[d, 1, h, pl.ds(t, 1), :]
                b = rows[d, 2, h, pl.ds(t, 1), :]
                kd = rows[d, 3, h, pl.ds(t, 1), :]
                sa = jnp.sum(s * nkk, axis=1, keepdims=True)
                vsel = jnp.where(onehot, vt[d, h, blk], jnp.zeros((), BF16))
                vcol = lax.reduce(vsel, np.array(0, BF16), lax.add, (1,)).reshape(C_N, 1).astype(F32)
                st[d, h] = s * w + sa * b + vcol * kd
        return carry

    lax.fori_loop(0, TM, step, 0, unroll=8)
    emit_y(0, TM - 1)
    emit_y(1, 0)

    for d, y_ref in enumerate((yf_ref, yb_ref)):
        for pr in range(N_PAIR):
            y_ref[:, pr * LANES:(pr + 1) * LANES] = yp[d, pr]

    @pl.when(k == nk - 1)
    def _():
        for d, s_out in enumerate((sf_ref, sb_ref)):
            for h in range(C_HEADS):
                o = (h % 2) * C_N
                s_out[0, h] = st[d, h, :, o:o + C_N]


def _scan_kernel_aliased(*refs):
    _scan_kernel(*refs[:14], *refs[16:])


def _scan_call(name, n_seq, n_tiles, tile0, prep, s0f, s0b, y_other=None):
    r, nkk, v, w_f, b_f, kd_f, w_b, b_b, kd_b = prep
    fwd = pl.BlockSpec((TM, C_W), lambda s, k: (tile0 + s * n_tiles + k, 0))
    bwd = pl.BlockSpec((TM, C_W), lambda s, k: (tile0 + s * n_tiles + (n_tiles - 1 - k), 0))
    st_spec = pl.BlockSpec((1, C_HEADS, C_N, C_N), lambda s, k: (s, 0, 0, 0))
    extra_specs, extra_args, aliases, body = [], (), {}, _scan_kernel
    if y_other is not None:
        extra_specs = [pl.BlockSpec(memory_space=pl.ANY)] * 2
        extra_args, aliases, body = tuple(y_other), {14: 0, 15: 1}, _scan_kernel_aliased
    return pl.pallas_call(
        body,
        grid=(n_seq, n_tiles),
        in_specs=[fwd] * 6 + [bwd] * 6 + [st_spec, st_spec] + extra_specs,
        out_specs=[fwd, bwd, st_spec, st_spec],
        out_shape=[jax.ShapeDtypeStruct((N_TOK, C_W), F32)] * 2
                  + [jax.ShapeDtypeStruct((n_seq, C_HEADS, C_N, C_N), F32)] * 2,
        input_output_aliases=aliases,
        scratch_shapes=[pltpu.VMEM((2, C_HEADS, C_N, LANES), F32),
                        pltpu.VMEM((2, 4, C_HEADS, TM, LANES), F32),
                        pltpu.VMEM((2, C_HEADS, N_BLK, C_N, LANES), BF16),
                        pltpu.VMEM((2, N_PAIR, TM, LANES), F32),
                        pltpu.VMEM((2, N_PAIR, TM, LANES), F32)],
        compiler_params=_cparams(2),
        name=name,
    )(r, nkk, v, w_f, b_f, kd_f, r, nkk, v, w_b, b_b, kd_b, s0f, s0b, *extra_args)


def _outproj_kernel(x_ref, mod_ref, ao_ref, bo_ref, yf_ref, yb_ref, bonus_ref, cg_ref,
                    lng_ref, lnb_ref, g2_ref, mean_ref, wa_ref, wb_ref, wc_ref, o_ref):
    y = yf_ref[...] + yb_ref[...] + bonus_ref[...]
    ones = mean_ref[...]
    mu = _head_sum(y, ones) * (1.0 / C_N)
    dy = y - mu
    var = _head_sum(dy * dy, ones) * (1.0 / C_N)
    yn = dy * lax.rsqrt(var + GN_EPS) * lng_ref[...] + lnb_ref[...]
    gate = jnp.dot(_sigmoid(cg_ref[...]).astype(BF16), g2_ref[...], preferred_element_type=F32)
    co = yn * gate
    m = (jnp.dot(ao_ref[...].astype(BF16), wa_ref[...], preferred_element_type=F32)
         + jnp.dot(bo_ref[...].astype(BF16), wb_ref[...], preferred_element_type=F32)
         + jnp.dot(co.astype(BF16), wc_ref[...], preferred_element_type=F32))
    o_ref[...] = x_ref[...] + mod_ref[0, 2:3, :] * m


def _outproj_call(x, mod_l, ao, bo, yf, yb, bonus, cg, p):
    full = lambda shape: pl.BlockSpec(shape, lambda i: (0,) * len(shape))
    tile = lambda w: pl.BlockSpec((TM, w), lambda i: (i, 0))
    return pl.pallas_call(
        _outproj_kernel,
        grid=(N_TILES,),
        in_specs=[tile(D_MODEL),
                  pl.BlockSpec((1, 8, D_MODEL), lambda i: (_tile_group(i), 0, 0)),
                  tile(A_W), tile(B_W), tile(C_W), tile(C_W), tile(C_W), tile(C_G_RANK),
                  full((1, C_W)), full((1, C_W)), full((C_G_RANK, C_W)), full((C_W, C_W)),
                  full((A_W, D_MODEL)), full((B_W, D_MODEL)), full((C_W, D_MODEL))],
        out_specs=tile(D_MODEL),
        out_shape=jax.ShapeDtypeStruct((N_TOK, D_MODEL), F32),
        compiler_params=_cparams(1),
        name="outproj",
    )(x, mod_l, ao, bo, yf, yb, bonus, cg, p["lnx_g"], p["lnx_b"], p["g2"], p["mean"],
      p["wa"], p["wb"], p["wc"])


def _ffn_kernel(final, x_ref, mod_ref, g_ref, w1_ref, w3_ref, w2_ref, fg_ref, o_ref):
    x = x_ref[...]
    h = _modulated_norm(x, g_ref[...], mod_ref[0, 3:4, :], mod_ref[0, 4:5, :]).astype(BF16)
    u = jnp.dot(h, w1_ref[...], preferred_element_type=F32)
    t = jnp.dot(h, w3_ref[...], preferred_element_type=F32)
    act = (u * _sigmoid(u) * t).astype(BF16)
    y = x + mod_ref[0, 5:6, :] * jnp.dot(act, w2_ref[...], preferred_element_type=F32)
    if final:
        ms = jnp.mean(y * y, axis=-1, keepdims=True)
        y = y * lax.rsqrt(ms + NORM_EPS) * fg_ref[...]
    o_ref[...] = y


def _ffn_call(final, x, mod_l, g, w1, w3, w2, final_g):
    once = lambda shape: pl.BlockSpec(shape, lambda i: (0,) * len(shape),
                                      pipeline_mode=pl.Buffered(1))
    return pl.pallas_call(
        functools.partial(_ffn_kernel, final),
        grid=(N_TILES,),
        in_specs=[pl.BlockSpec((TM, D_MODEL), lambda i: (i, 0)),
                  pl.BlockSpec((1, 8, D_MODEL), lambda i: (_tile_group(i), 0, 0)),
                  once((1, D_MODEL)),
                  once((D_MODEL, D_FF)), once((D_MODEL, D_FF)), once((D_FF, D_MODEL)),
                  once((1, D_MODEL))],
        out_specs=pl.BlockSpec((TM, D_MODEL), lambda i: (i, 0)),
        out_shape=jax.ShapeDtypeStruct((N_TOK, D_MODEL), F32),
        compiler_params=_cparams(1),
        name="ffn",
    )(x, mod_l, g, w1, w3, w2, final_g)


def _block_diag2(m):
    z = jnp.zeros_like(m[0])
    return jnp.concatenate([jnp.concatenate([m[0], z], axis=1),
                            jnp.concatenate([z, m[1]], axis=1)], axis=0)


def _keys_t(k, nb, t, heads, dim):
    return k.reshape(nb, t, heads, dim).transpose(0, 2, 3, 1).astype(BF16)


def kernel(x_prompt, x_sample, cache_a_k, cache_a_v, cache_b_k, cache_b_v, state_c_fwd, state_c_bwd,
           c, c_ctx, ada_w, ada_b, norm1_g, norm2_g, w_in, a_sink, b_lambda, b_subln_g,
           c_conv, c_w0, c_w2, c_a0, c_a2, c_g2, c_kk, c_ka, c_rk, c_lnx_g, c_lnx_b,
           w_out, ffn_w1, ffn_w3, ffn_w2, final_g):
    x = jnp.concatenate([x_prompt.reshape(N_CTX_TOK, D_MODEL),
                         x_sample.reshape(N_LAT_TOK, D_MODEL)], axis=0)
    cvec8 = jnp.concatenate([c_ctx[None], c, jnp.zeros((8 - 1 - NB_LAT, D_MODEL), F32)], axis=0)
    mod = _ada_call(cvec8, ada_w, ada_b)
    mod = mod[:, :1 + NB_LAT].reshape(DEPTH, 1 + NB_LAT, 6, D_MODEL)
    mod = jnp.pad(mod, ((0, 0), (0, 0), (0, 2), (0, 0)))

    tabs = _rope_tables()
    head_id = np.arange(C_W) // C_N
    block_ones = jnp.asarray(head_id[:, None] == head_id[None, :], BF16)
    final_g2 = final_g.reshape(1, D_MODEL)
    s0_ctx = jnp.zeros((NB_CTX, C_HEADS, C_N, C_N), F32)

    new_ak, new_av, new_bk, new_bv, new_sf, new_sb = [], [], [], [], [], []
    for l in range(DEPTH):
        lam_init = 0.8 - 0.6 * math.exp(-0.3 * l)
        aq, ak, av, bq, bk, bv, rkv, cw, ca, cg = _inproj_call(
            x, mod[l], norm1_g[l].reshape(1, D_MODEL), w_in[l].astype(BF16), tabs)

        new_ak.append(ak[:N_CTX_TOK].reshape(NB_CTX, T_CTX, A_KV, HD))
        new_av.append(av[:N_CTX_TOK].reshape(NB_CTX, T_CTX, A_KV, HD))
        new_bk.append(bk[:N_CTX_TOK].reshape(NB_CTX, T_CTX, B_HEADS, 2, B_DQ))
        new_bv.append(bv[:N_CTX_TOK].reshape(NB_CTX, T_CTX, B_HEADS, B_DV))

        sink = a_sink[l]
        ao_ctx = _attn_a_ctx_call(sink, aq, _keys_t(ak[:N_CTX_TOK], NB_CTX, T_CTX, A_KV, HD), av)
        ao = _attn_a_lat_call(
            sink, aq, _keys_t(ak[N_CTX_TOK:], NB_LAT, T_LAT, A_KV, HD), av,
            _keys_t(cache_a_k[:, l].reshape(NB_LAT * PAST, A_KVW), NB_LAT, PAST, A_KV, HD),
            cache_a_v[:, l].reshape(NB_LAT, PAST, A_KVW), ao_ctx)

        lam_p, sub_g = b_lambda[l], b_subln_g[l].reshape(1, B_DV)
        bo_ctx = _attn_b_ctx_call(
            lam_init, lam_p, sub_g, bq, _keys_t(bk[:N_CTX_TOK], NB_CTX, T_CTX, 2 * B_HEADS, B_DQ),
            bv[:N_CTX_TOK].reshape(NB_CTX, T_CTX, B_W).astype(BF16))
        bo = _attn_b_lat_call(
            lam_init, lam_p, sub_g, bq, _keys_t(bk[N_CTX_TOK:], NB_LAT, T_LAT, 2 * B_HEADS, B_DQ),
            bv[N_CTX_TOK:].reshape(NB_LAT, T_LAT, B_W).astype(BF16),
            _keys_t(cache_b_k[:, l].reshape(NB_LAT * PAST, B_W), NB_LAT, PAST, 2 * B_HEADS, B_DQ),
            cache_b_v[:, l].reshape(NB_LAT, PAST, B_W).astype(BF16), bo_ctx)

        prep = _rwkv_prep_call(rkv, cw, ca, dict(
            conv=c_conv[l], w0=c_w0[l], w2=_block_diag2(c_w2[l]).astype(BF16),
            a0=c_a0[l], a2=_block_diag2(c_a2[l]).astype(BF16),
            kk=c_kk[l].reshape(1, C_W), ka=c_ka[l].reshape(1, C_W), rk=c_rk[l].reshape(1, C_W),
            ones=block_ones))
        r, nkk, v, w_f, b_f, kd_f, w_b, b_b, kd_b, bonus = prep
        scan_in = (r, nkk, v, w_f, b_f, kd_f, w_b, b_b, kd_b)
        yf_c, yb_c, sf, sb = _scan_call("scan_ctx", NB_CTX, 1, 0, scan_in, s0_ctx, s0_ctx)
        yf, yb, _, _ = _scan_call("scan_lat", NB_LAT, LAT_TILES_PER_SEQ, N_CTX_TILES, scan_in,
                                  state_c_fwd[:, l], state_c_bwd[:, l], y_other=(yf_c, yb_c))
        new_sf.append(sf)
        new_sb.append(sb)

        wo = w_out[l].astype(BF16)
        x = _outproj_call(x, mod[l], ao, bo, yf, yb, bonus, cg, dict(
            lnx_g=c_lnx_g[l].reshape(1, C_W), lnx_b=c_lnx_b[l].reshape(1, C_W),
            g2=c_g2[l].astype(BF16), mean=block_ones,
            wa=wo[:A_W], wb=wo[A_W:A_W + B_W], wc=wo[A_W + B_W:]))

        x = _ffn_call(l == DEPTH - 1, x, mod[l], norm2_g[l].reshape(1, D_MODEL),
                      ffn_w1[l].astype(BF16), ffn_w3[l].astype(BF16), ffn_w2[l].astype(BF16), final_g2)

    y_prompt = x[:N_CTX_TOK].reshape(NB_CTX, T_CTX, D_MODEL)
    y_sample = x[N_CTX_TOK:].reshape(NB_LAT, T_LAT, D_MODEL)
    return (y_prompt, y_sample,
            jnp.stack(new_ak, axis=1), jnp.stack(new_av, axis=1),
            jnp.stack(new_bk, axis=1), jnp.stack(new_bv, axis=1),
            jnp.stack(new_sf, axis=1), jnp.stack(new_sb, axis=1))
```

```python
import functools
import math

import numpy as np
import jax
import jax.numpy as jnp
from jax import lax
from jax.experimental import pallas as pl
from jax.experimental.pallas import tpu as pltpu

F32 = jnp.float32
BF16 = jnp.bfloat16

D_MODEL = 1024
DEPTH = 4
NB_CTX, T_CTX = 16, 256
NB_LAT, T_LAT = 2, 4096
PAST = 512
GRID_W = 64
HD = 64
A_HEADS, A_KV = 6, 2
A_W = A_HEADS * HD
A_KVW = A_KV * HD
WINDOW = 128
B_HEADS, B_DQ, B_DV = 4, 32, 64
B_W = B_HEADS * B_DV
C_HEADS, C_N = 6, 64
C_W = C_HEADS * C_N
C_RANK = 64
C_G_RANK = 128
D_FF = 2816
IN_COLS = 2944
ROPE_THETA = 10000.0
NORM_EPS = 1e-6
GN_EPS = 64e-5
DECAY_SCALE = 0.606531
NEG_INF = -1e30

TM = 256
N_CTX_TOK = NB_CTX * T_CTX
N_LAT_TOK = NB_LAT * T_LAT
N_TOK = N_CTX_TOK + N_LAT_TOK
N_CTX_TILES = N_CTX_TOK // TM
N_TILES = N_TOK // TM
LAT_TILES_PER_SEQ = T_LAT // TM
LANES = 128
VMEM_LIMIT = 56 * 1024 * 1024

O_AQ, O_AK, O_AV, O_BQ, O_BK, O_BV, O_RKV, O_CW, O_CA, O_CG = (
    0, 384, 512, 640, 896, 1152, 1408, 2560, 2688, 2816)


def _cparams(n_grid):
    return pltpu.CompilerParams(dimension_semantics=("arbitrary",) * n_grid,
                                vmem_limit_bytes=VMEM_LIMIT)


def _sigmoid(x):
    return 1.0 / (1.0 + jnp.exp(-x))


def _tile_group(i):
    return jnp.where(i < N_CTX_TILES, 0, 1 + (i - N_CTX_TILES) // LAT_TILES_PER_SEQ)


def _lat_tile(i):
    return jnp.maximum(i - N_CTX_TILES, 0) % LAT_TILES_PER_SEQ


ADA_TN = 1536


def _ada_kernel(c_ref, w_ref, b_ref, o_ref):
    c = c_ref[...]
    s = c * _sigmoid(c)
    o_ref[0] = jnp.dot(s.astype(BF16), w_ref[0].astype(BF16),
                       preferred_element_type=F32) + b_ref[0]


def _ada_call(cvec8, ada_w, ada_b):
    n = 6 * D_MODEL
    return pl.pallas_call(
        _ada_kernel,
        grid=(DEPTH, n // ADA_TN),
        in_specs=[pl.BlockSpec((8, D_MODEL), lambda l, j: (0, 0)),
                  pl.BlockSpec((1, D_MODEL, ADA_TN), lambda l, j: (l, 0, j)),
                  pl.BlockSpec((1, 1, ADA_TN), lambda l, j: (l, 0, j))],
        out_specs=pl.BlockSpec((1, 8, ADA_TN), lambda l, j: (l, 0, j)),
        out_shape=jax.ShapeDtypeStruct((DEPTH, 8, n), F32),
        compiler_params=_cparams(2),
        name="ada",
    )(cvec8, ada_w, ada_b.reshape(DEPTH, 1, n))


def _modulated_norm(x, g, shift, scale):
    ms = jnp.mean(x * x, axis=-1, keepdims=True)
    h = x * lax.rsqrt(ms + NORM_EPS) * g
    return h * (1.0 + scale) + shift


def _rope_chunk(x, cos, sin_lo, sin_hi, half):
    up = pltpu.roll(x, LANES - half, axis=1)
    dn = pltpu.roll(x, half, axis=1)
    return x * cos + up * sin_lo + dn * sin_hi


def _inproj_kernel(x_ref, mod_ref, g_ref, w_ref,
                   cos_a, sl_a, sh_a, cos_b, sl_b, sh_b,
                   aq_ref, ak_ref, av_ref, bq_ref, bk_ref, bv_ref,
                   rkv_ref, cw_ref, ca_ref, cg_ref):
    i = pl.program_id(0)
    h = _modulated_norm(x_ref[...], g_ref[...], mod_ref[0, 0:1, :], mod_ref[0, 1:2, :])
    z = jnp.dot(h.astype(BF16), w_ref[...], preferred_element_type=F32)
    av_ref[...] = z[:, O_AV:O_BQ]
    bv_ref[...] = z[:, O_BV:O_RKV]
    rkv_ref[...] = z[:, O_RKV:O_CW]
    cw_ref[...] = z[:, O_CW:O_CA]
    ca_ref[...] = z[:, O_CA:O_CG]
    cg_ref[...] = z[:, O_CG:IN_COLS]

    @pl.when(i < N_CTX_TILES)
    def _():
        aq_ref[...] = z[:, O_AQ:O_AK]
        ak_ref[...] = z[:, O_AK:O_AV]
        bq_ref[...] = z[:, O_BQ:O_BK]
        bk_ref[...] = z[:, O_BK:O_BV]

    @pl.when(i >= N_CTX_TILES)
    def _():
        ca_, la_, ha_ = cos_a[...], sl_a[...], sh_a[...]
        cb_, lb_, hb_ = cos_b[...], sl_b[...], sh_b[...]
        for j in range(A_W // LANES):
            o = O_AQ + j * LANES
            aq_ref[:, j * LANES:(j + 1) * LANES] = _rope_chunk(z[:, o:o + LANES], ca_, la_, ha_, 16)
        ak_ref[...] = _rope_chunk(z[:, O_AK:O_AV], ca_, la_, ha_, 16)
        for j in range(B_W // LANES):
            o = O_BQ + j * LANES
            bq_ref[:, j * LANES:(j + 1) * LANES] = _rope_chunk(z[:, o:o + LANES], cb_, lb_, hb_, 8)
            o = O_BK + j * LANES
            bk_ref[:, j * LANES:(j + 1) * LANES] = _rope_chunk(z[:, o:o + LANES], cb_, lb_, hb_, 8)


def _inproj_call(x, mod_l, g, w_bf16, tabs):
    widths = (A_W, A_KVW, A_KVW, B_W, B_W, B_W, 3 * C_W, 2 * C_RANK, 2 * C_RANK, C_G_RANK)
    tab_spec = pl.BlockSpec((TM, LANES), lambda i: (_lat_tile(i), 0))
    return pl.pallas_call(
        _inproj_kernel,
        grid=(N_TILES,),
        in_specs=[pl.BlockSpec((TM, D_MODEL), lambda i: (i, 0)),
                  pl.BlockSpec((1, 8, D_MODEL), lambda i: (_tile_group(i), 0, 0)),
                  pl.BlockSpec((1, D_MODEL), lambda i: (0, 0)),
                  pl.BlockSpec((D_MODEL, IN_COLS), lambda i: (0, 0))] + [tab_spec] * 6,
        out_specs=[pl.BlockSpec((TM, w), lambda i: (i, 0)) for w in widths],
        out_shape=[jax.ShapeDtypeStruct((N_TOK, w), F32) for w in widths],
        compiler_params=_cparams(1),
        name="inproj",
    )(x, mod_l, g, w_bf16, *tabs)


def _rope_tables():
    t = np.arange(T_LAT)
    rows, cols = t // GRID_W, t % GRID_W

    def build(width):
        half = width // 4
        d = width // 2
        inv = ROPE_THETA ** (-jnp.arange(0, d, 2, dtype=F32) / d)
        lane = np.arange(LANES) % width
        part = lane // d
        p = lane % d
        f = p % half
        pos = jnp.where(jnp.asarray(part)[None, :] == 0,
                        jnp.asarray(rows, F32)[:, None], jnp.asarray(cols, F32)[:, None])
        ang = pos * inv[jnp.asarray(f)][None, :]
        cos, sin = jnp.cos(ang), jnp.sin(ang)
        lo = jnp.asarray(p < half)[None, :]
        return cos, jnp.where(lo, -sin, 0.0), jnp.where(lo, 0.0, sin)

    return build(HD) + build(B_DQ)


def _attend_a(q, sink_ref, segs):
    outs = []
    for h in range(A_HEADS):
        g = h // (A_HEADS // A_KV)
        qh = (q[:, h * HD:(h + 1) * HD] * (HD ** -0.5)).astype(BF16)
        sink = sink_ref[h]
        ss = []
        m = None
        for kt, _, mask in segs:
            s = jnp.dot(qh, kt(g), preferred_element_type=F32)
            if mask is not None:
                s = jnp.where(mask, s, NEG_INF)
            ss.append(s)
            sm = jnp.max(s, axis=-1, keepdims=True)
            m = sm if m is None else jnp.maximum(m, sm)
        m = jnp.maximum(m, sink)
        l = jnp.exp(sink - m)
        o = None
        for s, (_, v, _) in zip(ss, segs):
            p = jnp.exp(s - m)
            l = l + jnp.sum(p, axis=-1, keepdims=True)
            pv = jnp.dot(p.astype(BF16), v().astype(BF16), preferred_element_type=F32)[:, g * HD:(g + 1) * HD]
            o = pv if o is None else o + pv
        outs.append(o / l)
    return outs


def _attn_a_ctx_kernel(sink_ref, q_ref, kt_ref, v_ref, o_ref):
    outs = _attend_a(q_ref[...], sink_ref, [(lambda g: kt_ref[0, g], lambda: v_ref[...], None)])
    for h in range(A_HEADS):
        o_ref[:, h * HD:(h + 1) * HD] = outs[h]


def _attn_a_lat_kernel(sink_ref, q_ref, ktp_ref, ktc_ref, ktn_ref, vp_ref, vc_ref, vn_ref,
                       ktx_ref, vx_ref, _ctx_rows_ref, o_ref):
    qb = pl.program_id(1)
    nqb = pl.num_programs(1)
    qi = lax.broadcasted_iota(jnp.int32, (WINDOW, WINDOW), 0)
    kj = lax.broadcasted_iota(jnp.int32, (WINDOW, WINDOW), 1)
    mask_prev = (kj >= qi) & (qb > 0)
    mask_next = (kj <= qi) & (qb < nqb - 1)
    segs = [(lambda g: ktp_ref[0, g], lambda: vp_ref[...], mask_prev),
            (lambda g: ktc_ref[0, g], lambda: vc_ref[...], None),
            (lambda g: ktn_ref[0, g], lambda: vn_ref[...], mask_next),
            (lambda g: ktx_ref[0, g], lambda: vx_ref[0], None)]
    outs = _attend_a(q_ref[...], sink_ref, segs)
    for h in range(A_HEADS):
        o_ref[:, h * HD:(h + 1) * HD] = outs[h]


def _attn_a_ctx_call(sink, aq, akt_ctx, av):
    return pl.pallas_call(
        _attn_a_ctx_kernel,
        grid=(NB_CTX,),
        in_specs=[pl.BlockSpec(memory_space=pltpu.SMEM),
                  pl.BlockSpec((T_CTX, A_W), lambda b: (b, 0)),
                  pl.BlockSpec((1, A_KV, HD, T_CTX), lambda b: (b, 0, 0, 0)),
                  pl.BlockSpec((T_CTX, A_KVW), lambda b: (b, 0))],
        out_specs=pl.BlockSpec((T_CTX, A_W), lambda b: (b, 0)),
        out_shape=jax.ShapeDtypeStruct((N_TOK, A_W), F32),
        compiler_params=_cparams(1),
        name="attn_a_ctx",
    )(sink, aq, akt_ctx, av)


def _attn_a_lat_call(sink, aq, akt_lat, av, ktx, vx, ao_ctx):
    nqb = T_LAT // WINDOW
    ctx_blocks = N_CTX_TOK // WINDOW
    row = lambda b, j: ctx_blocks + b * nqb + j
    prev = lambda j: jnp.maximum(j - 1, 0)
    nxt = lambda j: jnp.minimum(j + 1, nqb - 1)
    kt_spec = lambda f: pl.BlockSpec((1, A_KV, HD, WINDOW), lambda b, j: (b, 0, 0, f(j)))
    v_spec = lambda f: pl.BlockSpec((WINDOW, A_KVW), lambda b, j: (row(b, f(j)), 0))
    same = lambda j: j
    return pl.pallas_call(
        _attn_a_lat_kernel,
        grid=(NB_LAT, nqb),
        in_specs=[pl.BlockSpec(memory_space=pltpu.SMEM),
                  pl.BlockSpec((WINDOW, A_W), lambda b, j: (row(b, j), 0)),
                  kt_spec(prev), kt_spec(same), kt_spec(nxt),
                  v_spec(prev), v_spec(same), v_spec(nxt),
                  pl.BlockSpec((1, A_KV, HD, PAST), lambda b, j: (b, 0, 0, 0)),
                  pl.BlockSpec((1, PAST, A_KVW), lambda b, j: (b, 0, 0)),
                  pl.BlockSpec(memory_space=pl.ANY)],
        out_specs=pl.BlockSpec((WINDOW, A_W), lambda b, j: (row(b, j), 0)),
        out_shape=jax.ShapeDtypeStruct((N_TOK, A_W), F32),
        input_output_aliases={10: 0},
        compiler_params=_cparams(2),
        name="attn_a_lat",
    )(sink, aq, akt_lat, akt_lat, akt_lat, av, av, av, ktx, vx, ao_ctx)


B_TQ = 256


def _attn_b_body(lam_init, q, lam_ref, g_ref, segs):
    lp = lam_ref[...]
    lam = (jnp.exp(jnp.sum(lp[0:1, :] * lp[1:2, :], axis=1, keepdims=True))
           - jnp.exp(jnp.sum(lp[2:3, :] * lp[3:4, :], axis=1, keepdims=True)) + lam_init)
    outs = []
    for h in range(B_HEADS):
        maps = []
        for mi in range(2):
            c0 = h * B_DV + mi * B_DQ
            qm = (q[:, c0:c0 + B_DQ] * (B_DQ ** -0.5)).astype(BF16)
            ss = [jnp.dot(qm, kt[0, 2 * h + mi], preferred_element_type=F32) for kt, _ in segs]
            m = None
            for s in ss:
                sm = jnp.max(s, axis=-1, keepdims=True)
                m = sm if m is None else jnp.maximum(m, sm)
            l = None
            o = None
            for s, (_, v) in zip(ss, segs):
                p = jnp.exp(s - m)
                ps = jnp.sum(p, axis=-1, keepdims=True)
                l = ps if l is None else l + ps
                pv = jnp.dot(p.astype(BF16), v[0], preferred_element_type=F32)[:, h * B_DV:(h + 1) * B_DV]
                o = pv if o is None else o + pv
            maps.append(o / l)
        a = maps[0] - lam * maps[1]
        ms = jnp.mean(a * a, axis=-1, keepdims=True)
        outs.append(a * lax.rsqrt(ms + NORM_EPS) * g_ref[...] * (1.0 - lam_init))
    return outs


def _attn_b_ctx_kernel(lam_init, lam_ref, g_ref, q_ref, kt_ref, v_ref, o_ref):
    outs = _attn_b_body(lam_init, q_ref[...], lam_ref, g_ref, [(kt_ref, v_ref)])
    for h in range(B_HEADS):
        o_ref[:, h * B_DV:(h + 1) * B_DV] = outs[h]


def _attn_b_lat_kernel(lam_init, lam_ref, g_ref, q_ref, kt_ref, v_ref, ktx_ref, vx_ref,
                       _ctx_rows_ref, o_ref):
    outs = _attn_b_body(lam_init, q_ref[...], lam_ref, g_ref,
                        [(kt_ref, v_ref), (ktx_ref, vx_ref)])
    for h in range(B_HEADS):
        o_ref[:, h * B_DV:(h + 1) * B_DV] = outs[h]


def _attn_b_ctx_call(lam_init, lam_p, g, bq, bkt_ctx, bv_ctx):
    return pl.pallas_call(
        functools.partial(_attn_b_ctx_kernel, lam_init),
        grid=(NB_CTX,),
        in_specs=[pl.BlockSpec((4, B_DQ), lambda b: (0, 0)),
                  pl.BlockSpec((1, B_DV), lambda b: (0, 0)),
                  pl.BlockSpec((T_CTX, B_W), lambda b: (b, 0)),
                  pl.BlockSpec((1, 2 * B_HEADS, B_DQ, T_CTX), lambda b: (b, 0, 0, 0)),
                  pl.BlockSpec((1, T_CTX, B_W), lambda b: (b, 0, 0))],
        out_specs=pl.BlockSpec((T_CTX, B_W), lambda b: (b, 0)),
        out_shape=jax.ShapeDtypeStruct((N_TOK, B_W), F32),
        compiler_params=_cparams(1),
        name="attn_b_ctx",
    )(lam_p, g, bq, bkt_ctx, bv_ctx)


def _attn_b_lat_call(lam_init, lam_p, g, bq, bkt_lat, bv_lat, ktx, vx, bo_ctx):
    nq = T_LAT // B_TQ
    ctx_blocks = N_CTX_TOK // B_TQ
    return pl.pallas_call(
        functools.partial(_attn_b_lat_kernel, lam_init),
        grid=(NB_LAT, nq),
        in_specs=[pl.BlockSpec((4, B_DQ), lambda b, j: (0, 0)),
                  pl.BlockSpec((1, B_DV), lambda b, j: (0, 0)),
                  pl.BlockSpec((B_TQ, B_W), lambda b, j: (ctx_blocks + b * nq + j, 0)),
                  pl.BlockSpec((1, 2 * B_HEADS, B_DQ, T_LAT), lambda b, j: (b, 0, 0, 0)),
                  pl.BlockSpec((1, T_LAT, B_W), lambda b, j: (b, 0, 0)),
                  pl.BlockSpec((1, 2 * B_HEADS, B_DQ, PAST), lambda b, j: (b, 0, 0, 0)),
                  pl.BlockSpec((1, PAST, B_W), lambda b, j: (b, 0, 0)),
                  pl.BlockSpec(memory_space=pl.ANY)],
        out_specs=pl.BlockSpec((B_TQ, B_W), lambda b, j: (ctx_blocks + b * nq + j, 0)),
        out_shape=jax.ShapeDtypeStruct((N_TOK, B_W), F32),
        input_output_aliases={7: 0},
        compiler_params=_cparams(2),
        name="attn_b_lat",
    )(lam_p, g, bq, bkt_lat, bv_lat, ktx, vx, bo_ctx)


HALO = 8


def _head_sum(x, ones_bf16):
    hi = x.astype(BF16)
    lo = (x - hi.astype(F32)).astype(BF16)
    return (jnp.dot(hi, ones_bf16, preferred_element_type=F32)
            + jnp.dot(lo, ones_bf16, preferred_element_type=F32))


def _rwkv_prep_kernel(rkv_ref, prev_ref, next_ref, cw_ref, ca_ref,
                      conv_ref, w0_ref, w2_ref, a0_ref, a2_ref, kk_ref, ka_ref, rk_ref, ones_ref,
                      r_ref, nkk_ref, v_ref, w_f, b_f, kd_f, w_b, b_b, kd_b, bonus_ref):
    i = pl.program_id(0)
    li = _lat_tile(i)
    is_ctx = i < N_CTX_TILES
    has_prev = jnp.logical_and(jnp.logical_not(is_ctx), li > 0).astype(F32)
    has_next = jnp.logical_and(jnp.logical_not(is_ctx), li < LAT_TILES_PER_SEQ - 1).astype(F32)
    x = rkv_ref[...]
    row = lax.broadcasted_iota(jnp.int32, x.shape, 0)
    xm = jnp.where(row == 0, prev_ref[HALO - 1:HALO, :] * has_prev, pltpu.roll(x, 1, axis=0))
    xp = jnp.where(row == TM - 1, next_ref[0:1, :] * has_next, pltpu.roll(x, TM - 1, axis=0))
    y = xm * conv_ref[0:1, :] + x * conv_ref[1:2, :] + xp * conv_ref[2:3, :]
    r, k, v = y[:, :C_W], y[:, C_W:2 * C_W], y[:, 2 * C_W:]
    ones = ones_ref[...]

    kk = k * kk_ref[...]
    kk = kk / jnp.maximum(jnp.sqrt(_head_sum(kk * kk, ones)), 1e-12)
    lw = jnp.dot(jnp.tanh(cw_ref[...]).astype(BF16), w2_ref[...],
                 preferred_element_type=F32)
    la = jnp.dot(ca_ref[...].astype(BF16), a2_ref[...], preferred_element_type=F32)
    r_ref[...] = r
    nkk_ref[...] = -kk
    v_ref[...] = v
    bonus = jnp.zeros_like(v)
    for d, (w_o, b_o, kd_o) in enumerate(((w_f, b_f, kd_f), (w_b, b_b, kd_b))):
        sl = slice(d * C_W, (d + 1) * C_W)
        a = _sigmoid(a0_ref[d:d + 1, :] + la[:, sl])
        kd = k * (1.0 + (a - 1.0) * ka_ref[...])
        w_o[...] = -DECAY_SCALE * _sigmoid(w0_ref[d:d + 1, :] + lw[:, sl])
        b_o[...] = kk * a
        kd_o[...] = kd
        bonus = bonus + _head_sum(r * kd * rk_ref[...], ones) * v
    bonus_ref[...] = bonus


def _rwkv_prep_call(rkv, cw, ca, p):
    nh = TM // HALO
    last = N_TOK // HALO - 1
    full = lambda shape: pl.BlockSpec(shape, lambda i: (0,) * len(shape))
    tile = lambda w: pl.BlockSpec((TM, w), lambda i: (i, 0))
    return pl.pallas_call(
        _rwkv_prep_kernel,
        grid=(N_TILES,),
        in_specs=[tile(3 * C_W),
                  pl.BlockSpec((HALO, 3 * C_W), lambda i: (jnp.maximum(i * nh - 1, 0), 0)),
                  pl.BlockSpec((HALO, 3 * C_W), lambda i: (jnp.minimum((i + 1) * nh, last), 0)),
                  tile(2 * C_RANK), tile(2 * C_RANK),
                  full((3, 3 * C_W)), full((2, C_W)), full((2 * C_RANK, 2 * C_W)),
                  full((2, C_W)), full((2 * C_RANK, 2 * C_W)),
                  full((1, C_W)), full((1, C_W)), full((1, C_W)), full((C_W, C_W))],
        out_specs=[tile(C_W)] * 10,
        out_shape=[jax.ShapeDtypeStruct((N_TOK, C_W), F32)] * 10,
        compiler_params=_cparams(1),
        name="rwkv_prep",
    )(rkv, rkv, rkv, cw, ca, p["conv"], p["w0"], p["w2"], p["a0"], p["a2"],
      p["kk"], p["ka"], p["rk"], p["ones"])


N_BLK = TM // LANES
N_PAIR = C_HEADS // 2


def _scan_kernel(r_f, nkk_f, lw_f, b_f, e_f, r_b, nkk_b, lw_b, b_b, e_b,
                 s0f_ref, s0b_ref, ef_ref, eb_ref,
                 yf_ref, yb_ref, sf_ref, sb_ref,
                 st, rows, et, rp, yp):
    k = pl.program_id(1)
    nk = pl.num_programs(1)
    lane = lax.broadcasted_iota(jnp.int32, (TM, LANES), 1)

    @pl.when(k == 0)
    def _():
        st[...] = jnp.zeros_like(st)
        for d, s0 in enumerate((s0f_ref, s0b_ref)):
            for h in range(C_HEADS):
                o = (h % 2) * C_N
                st[d, h, :, o:o + C_N] = s0[0, h]

    for d, srcs in enumerate(((nkk_f, lw_f, b_f), (nkk_b, lw_b, b_b))):
        for a, src in enumerate(srcs):
            for pr in range(N_PAIR):
                chunk = src[:, pr * LANES:(pr + 1) * LANES]
                if a == 1:
                    chunk = jnp.exp(chunk)
                rows[d, a, 2 * pr] = jnp.where(lane < C_N, chunk, 0.0)
                rows[d, a, 2 * pr + 1] = jnp.where(lane < C_N, 0.0, chunk)
    for d, src in enumerate((e_f, e_b)):
        for pr in range(N_PAIR):
            for blk in range(N_BLK):
                tr = src[blk * LANES:(blk + 1) * LANES, pr * LANES:(pr + 1) * LANES].T
                for half in range(2):
                    tile = tr[half * C_N:(half + 1) * C_N, :]
                    swapped = pltpu.roll(tile, C_N, axis=1)
                    for q in range(2):
                        et[d, 2 * pr + half, 2 * blk + q] = tile if q != half else swapped

    for d, src in enumerate((r_f, r_b)):
        for pr in range(N_PAIR):
            rp[d, pr] = src[:, pr * LANES:(pr + 1) * LANES]

    lane1 = lax.broadcasted_iota(jnp.int32, (1, LANES), 1)

    def emit_y(d, t):
        for pr in range(N_PAIR):
            both = jnp.concatenate([st[d, 2 * pr].astype(BF16), st[d, 2 * pr + 1].astype(BF16)], axis=0)
            r8 = jnp.broadcast_to(rp[d, pr, pl.ds(t, 1), :], (8, LANES)).astype(BF16)
            y8 = lax.dot_general(r8, both, (((1,), (1,)), ((), ())), preferred_element_type=F32)
            yp[d, pr, pl.ds(t, 1), :] = y8[0:1, :]

    def step(i, carry):
        for d in range(2):
            t = i if d == 0 else TM - 1 - i
            emit_y(d, jnp.maximum(i - 1, 0) if d == 0 else jnp.minimum(TM - i, TM - 1))
            blk = t // C_N
            pick = (lane1 == (t % C_N) + C_N, lane1 == (t % C_N))
            for h in range(C_HEADS):
                s = st[d, h]
                nkk = rows[d, 0, h, pl.ds(t, 1), :]
                w = rows[d, 1, h, pl.ds(t, 1), :]
                b = rows[d, 2, h, pl.ds(t, 1), :]
                sa = jnp.sum(jnp.where(pick[h % 2], et[d, h, blk], s * nkk), axis=1, keepdims=True)
                st[d, h] = s * w + sa * b
        return carry

    lax.fori_loop(0, TM, step, 0, unroll=8)
    emit_y(0, TM - 1)
    emit_y(1, 0)

    for d, y_ref in enumerate((yf_ref, yb_ref)):
        for pr in range(N_PAIR):
            y_ref[:, pr * LANES:(pr + 1) * LANES] = yp[d, pr]

    @pl.when(k == nk - 1)
    def _():
        for d, (s_out, e_fin) in enumerate(((sf_ref, ef_ref), (sb_ref, eb_ref))):
            for h in range(C_HEADS):
                o = (h % 2) * C_N
                s_out[0, h] = st[d, h, :, o:o + C_N] + e_fin[0, h]


N_SCAN_IN = 14


def _scan_kernel_aliased(*refs):
    _scan_kernel(*refs[:N_SCAN_IN], *refs[N_SCAN_IN + 2:])


def _scan_call(name, n_seq, n_tiles, tile0, ins, s0f, s0b, e_fin_f, e_fin_b, y_other=None):
    r, nkk, lw_f, b_f, e_f, lw_b, b_b, e_b = ins
    fwd = pl.BlockSpec((TM, C_W), lambda s, k: (tile0 + s * n_tiles + k, 0))
    bwd = pl.BlockSpec((TM, C_W), lambda s, k: (tile0 + s * n_tiles + (n_tiles - 1 - k), 0))
    st_spec = pl.BlockSpec((1, C_HEADS, C_N, C_N), lambda s, k: (s, 0, 0, 0))
    extra_specs, extra_args, aliases, body = [], (), {}, _scan_kernel
    if y_other is not None:
        extra_specs = [pl.BlockSpec(memory_space=pl.ANY)] * 2
        extra_args, body = tuple(y_other), _scan_kernel_aliased
        aliases = {N_SCAN_IN: 0, N_SCAN_IN + 1: 1}
    return pl.pallas_call(
        body,
        grid=(n_seq, n_tiles),
        in_specs=[fwd] * 5 + [bwd] * 5 + [st_spec] * 4 + extra_specs,
        out_specs=[fwd, bwd, st_spec, st_spec],
        out_shape=[jax.ShapeDtypeStruct((N_TOK, C_W), F32)] * 2
                  + [jax.ShapeDtypeStruct((n_seq, C_HEADS, C_N, C_N), F32)] * 2,
        input_output_aliases=aliases,
        scratch_shapes=[pltpu.VMEM((2, C_HEADS, C_N, LANES), F32),
                        pltpu.VMEM((2, 3, C_HEADS, TM, LANES), F32),
                        pltpu.VMEM((2, C_HEADS, TM // C_N, C_N, LANES), F32),
                        pltpu.VMEM((2, N_PAIR, TM, LANES), F32),
                        pltpu.VMEM((2, N_PAIR, TM, LANES), F32)],
        compiler_params=_cparams(2),
        name=name,
    )(r, nkk, lw_f, b_f, e_f, r, nkk, lw_b, b_b, e_b, s0f, s0b, e_fin_f, e_fin_b, *extra_args)


SUB = C_N


def _cum_rows(tri_bf16, x):
    hi = x.astype(BF16)
    lo = (x - hi.astype(F32)).astype(BF16)
    return (jnp.dot(tri_bf16, hi, preferred_element_type=F32)
            + jnp.dot(tri_bf16, lo, preferred_element_type=F32))


_NT = (((1,), (1,)), ((), ()))
_TN = (((0,), (0,)), ((), ()))


def _decay_kernel(r_f, nkk_f, v_f, lw_f, kd_f, r_b, nkk_b, v_b, lw_b, kd_b,
                  ef_ref, yef_ref, eb_ref, yeb_ref, stf_ref, stb_ref, est):
    k = pl.program_id(1)
    nk = pl.num_programs(1)

    @pl.when(k == 0)
    def _():
        est[...] = jnp.zeros_like(est)

    ti = lax.broadcasted_iota(jnp.int32, (SUB, SUB), 0)
    si = lax.broadcasted_iota(jnp.int32, (SUB, SUB), 1)
    dirs = ((r_f, nkk_f, v_f, lw_f, kd_f, ef_ref, yef_ref), (r_b, nkk_b, v_b, lw_b, kd_b, eb_ref, yeb_ref))
    for d, (r_ref, nkk_ref, v_ref, lw_ref, kd_ref, e_out, ye_out) in enumerate(dirs):
        upto = (si <= ti) if d == 0 else (si >= ti)
        before = (si < ti) if d == 0 else (si > ti)
        both_masks = jnp.concatenate([before, upto], axis=0)
        tri = upto.astype(BF16)
        last = SUB - 1 if d == 0 else 0
        chunks = range(TM // SUB) if d == 0 else range(TM // SUB - 1, -1, -1)
        for c in chunks:
            rs = slice(c * SUB, (c + 1) * SUB)
            lw = lw_ref[rs, :]
            lg = _cum_rows(tri, lw)
            abar = nkk_ref[rs, :] * jnp.exp(lg - lw)
            rbar = r_ref[rs, :] * jnp.exp(lg)
            kbar = kd_ref[rs, :] * jnp.exp(-lg)
            g_end = jnp.exp(lg[last:last + 1, :])
            v = v_ref[rs, :]
            for h in range(C_HEADS):
                hs = slice(h * C_N, (h + 1) * C_N)
                lhs = jnp.concatenate([abar[:, hs], rbar[:, hs]], axis=0).astype(BF16)
                kh = kbar[:, hs].astype(BF16)
                vh = v[:, hs].astype(BF16)
                eh = est[d, h]
                sc = lax.dot_general(lhs, kh, _NT, preferred_element_type=F32)
                sc = jnp.where(both_masks, sc, 0.0).astype(BF16)
                out = (jnp.dot(sc, vh, preferred_element_type=F32)
                       + lax.dot_general(lhs, eh.astype(BF16), _NT, preferred_element_type=F32))
                e_out[rs, hs] = out[0:SUB, :]
                ye_out[rs, hs] = out[SUB:2 * SUB, :]
                est[d, h] = (eh + lax.dot_general(vh, kh, _TN, preferred_element_type=F32)) * g_end[:, hs]

    @pl.when(k == nk - 1)
    def _():
        stf_ref[0] = est[0]
        stb_ref[0] = est[1]


def _decay_call(name, n_seq, n_tiles, tile0, ins, other=None):
    r, nkk, v, lw_f, kd_f, lw_b, kd_b = ins
    fwd = pl.BlockSpec((TM, C_W), lambda s, k: (tile0 + s * n_tiles + k, 0))
    bwd = pl.BlockSpec((TM, C_W), lambda s, k: (tile0 + s * n_tiles + (n_tiles - 1 - k), 0))
    st_spec = pl.BlockSpec((1, C_HEADS, C_N, C_N), lambda s, k: (s, 0, 0, 0))
    extra_specs, extra_args, aliases, body = [], (), {}, _decay_kernel
    if other is not None:
        extra_specs = [pl.BlockSpec(memory_space=pl.ANY)] * 4
        extra_args, body = tuple(other), _decay_kernel_aliased
        aliases = {10 + j: j for j in range(4)}
    return pl.pallas_call(
        body,
        grid=(n_seq, n_tiles),
        in_specs=[fwd] * 5 + [bwd] * 5 + extra_specs,
        out_specs=[fwd, fwd, bwd, bwd, st_spec, st_spec],
        out_shape=[jax.ShapeDtypeStruct((N_TOK, C_W), F32)] * 4
                  + [jax.ShapeDtypeStruct((n_seq, C_HEADS, C_N, C_N), F32)] * 2,
        input_output_aliases=aliases,
        scratch_shapes=[pltpu.VMEM((2, C_HEADS, C_N, C_N), F32)],
        compiler_params=_cparams(2),
        name=name,
    )(r, nkk, v, lw_f, kd_f, r, nkk, v, lw_b, kd_b, *extra_args)


def _decay_kernel_aliased(*refs):
    _decay_kernel(*refs[:10], *refs[14:])


def _outproj_kernel(x_ref, mod_ref, ao_ref, bo_ref, yf_ref, yb_ref, yef_ref, yeb_ref, bonus_ref, cg_ref,
                    lng_ref, lnb_ref, g2_ref, mean_ref, wa_ref, wb_ref, wc_ref, o_ref):
    y = (yf_ref[...] + yef_ref[...]) + (yb_ref[...] + yeb_ref[...]) + bonus_ref[...]
    ones = mean_ref[...]
    mu = _head_sum(y, ones) * (1.0 / C_N)
    dy = y - mu
    var = _head_sum(dy * dy, ones) * (1.0 / C_N)
    yn = dy * lax.rsqrt(var + GN_EPS) * lng_ref[...] + lnb_ref[...]
    gate = jnp.dot(_sigmoid(cg_ref[...]).astype(BF16), g2_ref[...], preferred_element_type=F32)
    co = yn * gate
    m = (jnp.dot(ao_ref[...].astype(BF16), wa_ref[...], preferred_element_type=F32)
         + jnp.dot(bo_ref[...].astype(BF16), wb_ref[...], preferred_element_type=F32)
         + jnp.dot(co.astype(BF16), wc_ref[...], preferred_element_type=F32))
    o_ref[...] = x_ref[...] + mod_ref[0, 2:3, :] * m


def _outproj_call(x, mod_l, ao, bo, yf, yb, yef, yeb, bonus, cg, p):
    full = lambda shape: pl.BlockSpec(shape, lambda i: (0,) * len(shape))
    tile = lambda w: pl.BlockSpec((TM, w), lambda i: (i, 0))
    return pl.pallas_call(
        _outproj_kernel,
        grid=(N_TILES,),
        in_specs=[tile(D_MODEL),
                  pl.BlockSpec((1, 8, D_MODEL), lambda i: (_tile_group(i), 0, 0)),
                  tile(A_W), tile(B_W), tile(C_W), tile(C_W), tile(C_W), tile(C_W), tile(C_W),
                  tile(C_G_RANK),
                  full((1, C_W)), full((1, C_W)), full((C_G_RANK, C_W)), full((C_W, C_W)),
                  full((A_W, D_MODEL)), full((B_W, D_MODEL)), full((C_W, D_MODEL))],
        out_specs=tile(D_MODEL),
        out_shape=jax.ShapeDtypeStruct((N_TOK, D_MODEL), F32),
        compiler_params=_cparams(1),
        name="outproj",
    )(x, mod_l, ao, bo, yf, yb, yef, yeb, bonus, cg, p["lnx_g"], p["lnx_b"], p["g2"], p["mean"],
      p["wa"], p["wb"], p["wc"])


def _ffn_kernel(final, x_ref, mod_ref, g_ref, w1_ref, w3_ref, w2_ref, fg_ref, o_ref):
    x = x_ref[...]
    h = _modulated_norm(x, g_ref[...], mod_ref[0, 3:4, :], mod_ref[0, 4:5, :]).astype(BF16)
    u = jnp.dot(h, w1_ref[...], preferred_element_type=F32)
    t = jnp.dot(h, w3_ref[...], preferred_element_type=F32)
    act = (u * _sigmoid(u) * t).astype(BF16)
    y = x + mod_ref[0, 5:6, :] * jnp.dot(act, w2_ref[...], preferred_element_type=F32)
    if final:
        ms = jnp.mean(y * y, axis=-1, keepdims=True)
        y = y * lax.rsqrt(ms + NORM_EPS) * fg_ref[...]
    o_ref[...] = y


def _ffn_call(final, x, mod_l, g, w1, w3, w2, final_g):
    once = lambda shape: pl.BlockSpec(shape, lambda i: (0,) * len(shape),
                                      pipeline_mode=pl.Buffered(1))
    return pl.pallas_call(
        functools.partial(_ffn_kernel, final),
        grid=(N_TILES,),
        in_specs=[pl.BlockSpec((TM, D_MODEL), lambda i: (i, 0)),
                  pl.BlockSpec((1, 8, D_MODEL), lambda i: (_tile_group(i), 0, 0)),
                  once((1, D_MODEL)),
                  once((D_MODEL, D_FF)), once((D_MODEL, D_FF)), once((D_FF, D_MODEL)),
                  once((1, D_MODEL))],
        out_specs=pl.BlockSpec((TM, D_MODEL), lambda i: (i, 0)),
        out_shape=jax.ShapeDtypeStruct((N_TOK, D_MODEL), F32),
        compiler_params=_cparams(1),
        name="ffn",
    )(x, mod_l, g, w1, w3, w2, final_g)


def _block_diag2(m):
    z = jnp.zeros_like(m[0])
    return jnp.concatenate([jnp.concatenate([m[0], z], axis=1),
                            jnp.concatenate([z, m[1]], axis=1)], axis=0)


def _keys_t(k, nb, t, heads, dim):
    return k.reshape(nb, t, heads, dim).transpose(0, 2, 3, 1).astype(BF16)


def kernel(x_prompt, x_sample, cache_a_k, cache_a_v, cache_b_k, cache_b_v, state_c_fwd, state_c_bwd,
           c, c_ctx, ada_w, ada_b, norm1_g, norm2_g, w_in, a_sink, b_lambda, b_subln_g,
           c_conv, c_w0, c_w2, c_a0, c_a2, c_g2, c_kk, c_ka, c_rk, c_lnx_g, c_lnx_b,
           w_out, ffn_w1, ffn_w3, ffn_w2, final_g):
    x = jnp.concatenate([x_prompt.reshape(N_CTX_TOK, D_MODEL),
                         x_sample.reshape(N_LAT_TOK, D_MODEL)], axis=0)
    cvec8 = jnp.concatenate([c_ctx[None], c, jnp.zeros((8 - 1 - NB_LAT, D_MODEL), F32)], axis=0)
    mod = _ada_call(cvec8, ada_w, ada_b)
    mod = mod[:, :1 + NB_LAT].reshape(DEPTH, 1 + NB_LAT, 6, D_MODEL)
    mod = jnp.pad(mod, ((0, 0), (0, 0), (0, 2), (0, 0)))

    tabs = _rope_tables()
    head_id = np.arange(C_W) // C_N
    block_ones = jnp.asarray(head_id[:, None] == head_id[None, :], BF16)
    final_g2 = final_g.reshape(1, D_MODEL)
    s0_ctx = jnp.zeros((NB_CTX, C_HEADS, C_N, C_N), F32)

    new_ak, new_av, new_bk, new_bv, new_sf, new_sb = [], [], [], [], [], []
    for l in range(DEPTH):
        lam_init = 0.8 - 0.6 * math.exp(-0.3 * l)
        aq, ak, av, bq, bk, bv, rkv, cw, ca, cg = _inproj_call(
            x, mod[l], norm1_g[l].reshape(1, D_MODEL), w_in[l].astype(BF16), tabs)

        new_ak.append(ak[:N_CTX_TOK].reshape(NB_CTX, T_CTX, A_KV, HD))
        new_av.append(av[:N_CTX_TOK].reshape(NB_CTX, T_CTX, A_KV, HD))
        new_bk.append(bk[:N_CTX_TOK].reshape(NB_CTX, T_CTX, B_HEADS, 2, B_DQ))
        new_bv.append(bv[:N_CTX_TOK].reshape(NB_CTX, T_CTX, B_HEADS, B_DV))

        sink = a_sink[l]
        ao_ctx = _attn_a_ctx_call(sink, aq, _keys_t(ak[:N_CTX_TOK], NB_CTX, T_CTX, A_KV, HD), av)
        ao = _attn_a_lat_call(
            sink, aq, _keys_t(ak[N_CTX_TOK:], NB_LAT, T_LAT, A_KV, HD), av,
            _keys_t(cache_a_k[:, l].reshape(NB_LAT * PAST, A_KVW), NB_LAT, PAST, A_KV, HD),
            cache_a_v[:, l].reshape(NB_LAT, PAST, A_KVW), ao_ctx)

        lam_p, sub_g = b_lambda[l], b_subln_g[l].reshape(1, B_DV)
        bo_ctx = _attn_b_ctx_call(
            lam_init, lam_p, sub_g, bq, _keys_t(bk[:N_CTX_TOK], NB_CTX, T_CTX, 2 * B_HEADS, B_DQ),
            bv[:N_CTX_TOK].reshape(NB_CTX, T_CTX, B_W).astype(BF16))
        bo = _attn_b_lat_call(
            lam_init, lam_p, sub_g, bq, _keys_t(bk[N_CTX_TOK:], NB_LAT, T_LAT, 2 * B_HEADS, B_DQ),
            bv[N_CTX_TOK:].reshape(NB_LAT, T_LAT, B_W).astype(BF16),
            _keys_t(cache_b_k[:, l].reshape(NB_LAT * PAST, B_W), NB_LAT, PAST, 2 * B_HEADS, B_DQ),
            cache_b_v[:, l].reshape(NB_LAT, PAST, B_W).astype(BF16), bo_ctx)

        prep = _rwkv_prep_call(rkv, cw, ca, dict(
            conv=c_conv[l], w0=c_w0[l], w2=_block_diag2(c_w2[l]).astype(BF16),
            a0=c_a0[l], a2=_block_diag2(c_a2[l]).astype(BF16),
            kk=c_kk[l].reshape(1, C_W), ka=c_ka[l].reshape(1, C_W), rk=c_rk[l].reshape(1, C_W),
            ones=block_ones))
        r, nkk, v, lw_f, b_f, kd_f, lw_b, b_b, kd_b, bonus = prep
        decay_in = (r, nkk, v, lw_f, kd_f, lw_b, kd_b)
        *rows_c, ef_c, eb_c = _decay_call("decay_ctx", NB_CTX, 1, 0, decay_in)
        e_f, yef, e_b, yeb, ef_l, eb_l = _decay_call(
            "decay_lat", NB_LAT, LAT_TILES_PER_SEQ, N_CTX_TILES, decay_in, other=rows_c)
        scan_in = (r, nkk, lw_f, b_f, e_f, lw_b, b_b, e_b)
        yf_c, yb_c, sf, sb = _scan_call("scan_ctx", NB_CTX, 1, 0, scan_in, s0_ctx, s0_ctx, ef_c, eb_c)
        yf, yb, _, _ = _scan_call("scan_lat", NB_LAT, LAT_TILES_PER_SEQ, N_CTX_TILES, scan_in,
                                  state_c_fwd[:, l], state_c_bwd[:, l], ef_l, eb_l,
                                  y_other=(yf_c, yb_c))
        new_sf.append(sf)
        new_sb.append(sb)

        wo = w_out[l].astype(BF16)
        x = _outproj_call(x, mod[l], ao, bo, yf, yb, yef, yeb, bonus, cg, dict(
            lnx_g=c_lnx_g[l].reshape(1, C_W), lnx_b=c_lnx_b[l].reshape(1, C_W),
            g2=c_g2[l].astype(BF16), mean=block_ones,
            wa=wo[:A_W], wb=wo[A_W:A_W + B_W], wc=wo[A_W + B_W:]))

        x = _ffn_call(l == DEPTH - 1, x, mod[l], norm2_g[l].reshape(1, D_MODEL),
                      ffn_w1[l].astype(BF16), ffn_w3[l].astype(BF16), ffn_w2[l].astype(BF16), final_g2)

    y_prompt = x[:N_CTX_TOK].reshape(NB_CTX, T_CTX, D_MODEL)
    y_sample = x[N_CTX_TOK:].reshape(NB_LAT, T_LAT, D_MODEL)
    return (y_prompt, y_sample,
            jnp.stack(new_ak, axis=1), jnp.stack(new_av, axis=1),
            jnp.stack(new_bk, axis=1), jnp.stack(new_bv, axis=1),
            jnp.stack(new_sf, axis=1), jnp.stack(new_sb, axis=1))
```

```python
import functools
import math

import numpy as np
import jax
import jax.numpy as jnp
from jax import lax
from jax.experimental import pallas as pl
from jax.experimental.pallas import tpu as pltpu

F32 = jnp.float32
BF16 = jnp.bfloat16

D_MODEL = 1024
DEPTH = 4
NB_CTX, T_CTX = 16, 256
NB_LAT, T_LAT = 2, 4096
PAST = 512
GRID_W = 64
HD = 64
A_HEADS, A_KV = 6, 2
A_W = A_HEADS * HD
A_KVW = A_KV * HD
WINDOW = 128
B_HEADS, B_DQ, B_DV = 4, 32, 64
B_W = B_HEADS * B_DV
C_HEADS, C_N = 6, 64
C_W = C_HEADS * C_N
C_RANK = 64
C_G_RANK = 128
D_FF = 2816
IN_COLS = 2944
ROPE_THETA = 10000.0
NORM_EPS = 1e-6
GN_EPS = 64e-5
DECAY_SCALE = 0.606531
NEG_INF = -1e30

TM = 256
N_CTX_TOK = NB_CTX * T_CTX
N_LAT_TOK = NB_LAT * T_LAT
N_TOK = N_CTX_TOK + N_LAT_TOK
N_CTX_TILES = N_CTX_TOK // TM
N_TILES = N_TOK // TM
LAT_TILES_PER_SEQ = T_LAT // TM
LANES = 128
VMEM_LIMIT = 56 * 1024 * 1024

O_AQ, O_AK, O_AV, O_BQ, O_BK, O_BV, O_RKV, O_CW, O_CA, O_CG = (
    0, 384, 512, 640, 896, 1152, 1408, 2560, 2688, 2816)


def _cparams(n_grid):
    return pltpu.CompilerParams(dimension_semantics=("arbitrary",) * n_grid,
                                vmem_limit_bytes=VMEM_LIMIT)


def _sigmoid(x):
    return 1.0 / (1.0 + jnp.exp(-x))


def _tile_group(i):
    return jnp.where(i < N_CTX_TILES, 0, 1 + (i - N_CTX_TILES) // LAT_TILES_PER_SEQ)


def _lat_tile(i):
    return jnp.maximum(i - N_CTX_TILES, 0) % LAT_TILES_PER_SEQ


ADA_TN = 1536


def _ada_kernel(c_ref, w_ref, b_ref, o_ref):
    c = c_ref[...]
    s = c * _sigmoid(c)
    o_ref[0] = jnp.dot(s.astype(BF16), w_ref[0].astype(BF16),
                       preferred_element_type=F32) + b_ref[0]


def _ada_call(cvec8, ada_w, ada_b):
    n = 6 * D_MODEL
    return pl.pallas_call(
        _ada_kernel,
        grid=(DEPTH, n // ADA_TN),
        in_specs=[pl.BlockSpec((8, D_MODEL), lambda l, j: (0, 0)),
                  pl.BlockSpec((1, D_MODEL, ADA_TN), lambda l, j: (l, 0, j)),
                  pl.BlockSpec((1, 1, ADA_TN), lambda l, j: (l, 0, j))],
        out_specs=pl.BlockSpec((1, 8, ADA_TN), lambda l, j: (l, 0, j)),
        out_shape=jax.ShapeDtypeStruct((DEPTH, 8, n), F32),
        compiler_params=_cparams(2),
        name="ada",
    )(cvec8, ada_w, ada_b.reshape(DEPTH, 1, n))


def _modulated_norm(x, g, shift, scale):
    ms = jnp.mean(x * x, axis=-1, keepdims=True)
    h = x * lax.rsqrt(ms + NORM_EPS) * g
    return h * (1.0 + scale) + shift


def _rope_chunk(x, cos, sin_lo, sin_hi, half):
    up = pltpu.roll(x, LANES - half, axis=1)
    dn = pltpu.roll(x, half, axis=1)
    return x * cos + up * sin_lo + dn * sin_hi


def _inproj_kernel(x_ref, mod_ref, g_ref, w_ref,
                   cos_a, sl_a, sh_a, cos_b, sl_b, sh_b,
                   aq_ref, ak_ref, av_ref, bq_ref, bk_ref, bv_ref,
                   rkv_ref, cw_ref, ca_ref, cg_ref):
    i = pl.program_id(0)
    h = _modulated_norm(x_ref[...], g_ref[...], mod_ref[0, 0:1, :], mod_ref[0, 1:2, :])
    z = jnp.dot(h.astype(BF16), w_ref[...], preferred_element_type=F32)
    av_ref[...] = z[:, O_AV:O_BQ]
    bv_ref[...] = z[:, O_BV:O_RKV]
    rkv_ref[...] = z[:, O_RKV:O_CW]
    cw_ref[...] = z[:, O_CW:O_CA]
    ca_ref[...] = z[:, O_CA:O_CG]
    cg_ref[...] = z[:, O_CG:IN_COLS]

    @pl.when(i < N_CTX_TILES)
    def _():
        aq_ref[...] = z[:, O_AQ:O_AK]
        ak_ref[...] = z[:, O_AK:O_AV]
        bq_ref[...] = z[:, O_BQ:O_BK]
        bk_ref[...] = z[:, O_BK:O_BV]

    @pl.when(i >= N_CTX_TILES)
    def _():
        ca_, la_, ha_ = cos_a[...], sl_a[...], sh_a[...]
        cb_, lb_, hb_ = cos_b[...], sl_b[...], sh_b[...]
        for j in range(A_W // LANES):
            o = O_AQ + j * LANES
            aq_ref[:, j * LANES:(j + 1) * LANES] = _rope_chunk(z[:, o:o + LANES], ca_, la_, ha_, 16)
        ak_ref[...] = _rope_chunk(z[:, O_AK:O_AV], ca_, la_, ha_, 16)
        for j in range(B_W // LANES):
            o = O_BQ + j * LANES
            bq_ref[:, j * LANES:(j + 1) * LANES] = _rope_chunk(z[:, o:o + LANES], cb_, lb_, hb_, 8)
            o = O_BK + j * LANES
            bk_ref[:, j * LANES:(j + 1) * LANES] = _rope_chunk(z[:, o:o + LANES], cb_, lb_, hb_, 8)


def _inproj_call(x, mod_l, g, w_bf16, tabs):
    widths = (A_W, A_KVW, A_KVW, B_W, B_W, B_W, 3 * C_W, 2 * C_RANK, 2 * C_RANK, C_G_RANK)
    tab_spec = pl.BlockSpec((TM, LANES), lambda i: (_lat_tile(i), 0))
    return pl.pallas_call(
        _inproj_kernel,
        grid=(N_TILES,),
        in_specs=[pl.BlockSpec((TM, D_MODEL), lambda i: (i, 0)),
                  pl.BlockSpec((1, 8, D_MODEL), lambda i: (_tile_group(i), 0, 0)),
                  pl.BlockSpec((1, D_MODEL), lambda i: (0, 0)),
                  pl.BlockSpec((D_MODEL, IN_COLS), lambda i: (0, 0))] + [tab_spec] * 6,
        out_specs=[pl.BlockSpec((TM, w), lambda i: (i, 0)) for w in widths],
        out_shape=[jax.ShapeDtypeStruct((N_TOK, w), F32) for w in widths],
        compiler_params=_cparams(1),
        name="inproj",
    )(x, mod_l, g, w_bf16, *tabs)


def _rope_tables():
    t = np.arange(T_LAT)
    rows, cols = t // GRID_W, t % GRID_W

    def build(width):
        half = width // 4
        d = width // 2
        inv = ROPE_THETA ** (-jnp.arange(0, d, 2, dtype=F32) / d)
        lane = np.arange(LANES) % width
        part = lane // d
        p = lane % d
        f = p % half
        pos = jnp.where(jnp.asarray(part)[None, :] == 0,
                        jnp.asarray(rows, F32)[:, None], jnp.asarray(cols, F32)[:, None])
        ang = pos * inv[jnp.asarray(f)][None, :]
        cos, sin = jnp.cos(ang), jnp.sin(ang)
        lo = jnp.asarray(p < half)[None, :]
        return cos, jnp.where(lo, -sin, 0.0), jnp.where(lo, 0.0, sin)

    return build(HD) + build(B_DQ)


def _attend_a(q, sink_ref, segs):
    outs = []
    for h in range(A_HEADS):
        g = h // (A_HEADS // A_KV)
        qh = (q[:, h * HD:(h + 1) * HD] * (HD ** -0.5)).astype(BF16)
        sink = sink_ref[h]
        ss = []
        m = None
        for kt, _, mask in segs:
            s = jnp.dot(qh, kt(g), preferred_element_type=F32)
            if mask is not None:
                s = jnp.where(mask, s, NEG_INF)
            ss.append(s)
            sm = jnp.max(s, axis=-1, keepdims=True)
            m = sm if m is None else jnp.maximum(m, sm)
        m = jnp.maximum(m, sink)
        l = jnp.exp(sink - m)
        o = None
        for s, (_, v, _) in zip(ss, segs):
            p = jnp.exp(s - m)
            l = l + jnp.sum(p, axis=-1, keepdims=True)
            pv = jnp.dot(p.astype(BF16), v().astype(BF16), preferred_element_type=F32)[:, g * HD:(g + 1) * HD]
            o = pv if o is None else o + pv
        outs.append(o / l)
    return outs


def _attn_a_ctx_kernel(sink_ref, q_ref, kt_ref, v_ref, o_ref):
    outs = _attend_a(q_ref[...], sink_ref, [(lambda g: kt_ref[0, g], lambda: v_ref[...], None)])
    for h in range(A_HEADS):
        o_ref[:, h * HD:(h + 1) * HD] = outs[h]


def _attn_a_lat_kernel(sink_ref, q_ref, ktp_ref, ktc_ref, ktn_ref, vp_ref, vc_ref, vn_ref,
                       ktx_ref, vx_ref, o_ref):
    qb = pl.program_id(1)
    nqb = pl.num_programs(1)
    qi = lax.broadcasted_iota(jnp.int32, (WINDOW, WINDOW), 0)
    kj = lax.broadcasted_iota(jnp.int32, (WINDOW, WINDOW), 1)
    mask_prev = (kj >= qi) & (qb > 0)
    mask_next = (kj <= qi) & (qb < nqb - 1)
    segs = [(lambda g: ktp_ref[0, g], lambda: vp_ref[...], mask_prev),
            (lambda g: ktc_ref[0, g], lambda: vc_ref[...], None),
            (lambda g: ktn_ref[0, g], lambda: vn_ref[...], mask_next),
            (lambda g: ktx_ref[0, g], lambda: vx_ref[0], None)]
    outs = _attend_a(q_ref[...], sink_ref, segs)
    for h in range(A_HEADS):
        o_ref[:, h * HD:(h + 1) * HD] = outs[h]


def _attn_a_ctx_call(sink, aq, akt_ctx, av):
    return pl.pallas_call(
        _attn_a_ctx_kernel,
        grid=(NB_CTX,),
        in_specs=[pl.BlockSpec(memory_space=pltpu.SMEM),
                  pl.BlockSpec((T_CTX, A_W), lambda b: (b, 0)),
                  pl.BlockSpec((1, A_KV, HD, T_CTX), lambda b: (b, 0, 0, 0)),
                  pl.BlockSpec((T_CTX, A_KVW), lambda b: (b, 0))],
        out_specs=pl.BlockSpec((T_CTX, A_W), lambda b: (b, 0)),
        out_shape=jax.ShapeDtypeStruct((N_CTX_TOK, A_W), F32),
        compiler_params=_cparams(1),
        name="attn_a_ctx",
    )(sink, aq, akt_ctx, av)


def _attn_a_lat_call(sink, aq, akt_lat, av, ktx, vx):
    nqb = T_LAT // WINDOW
    ctx_blocks = N_CTX_TOK // WINDOW
    row = lambda b, j: ctx_blocks + b * nqb + j
    prev = lambda j: jnp.maximum(j - 1, 0)
    nxt = lambda j: jnp.minimum(j + 1, nqb - 1)
    kt_spec = lambda f: pl.BlockSpec((1, A_KV, HD, WINDOW), lambda b, j: (b, 0, 0, f(j)))
    v_spec = lambda f: pl.BlockSpec((WINDOW, A_KVW), lambda b, j: (row(b, f(j)), 0))
    same = lambda j: j
    return pl.pallas_call(
        _attn_a_lat_kernel,
        grid=(NB_LAT, nqb),
        in_specs=[pl.BlockSpec(memory_space=pltpu.SMEM),
                  pl.BlockSpec((WINDOW, A_W), lambda b, j: (row(b, j), 0)),
                  kt_spec(prev), kt_spec(same), kt_spec(nxt),
                  v_spec(prev), v_spec(same), v_spec(nxt),
                  pl.BlockSpec((1, A_KV, HD, PAST), lambda b, j: (b, 0, 0, 0)),
                  pl.BlockSpec((1, PAST, A_KVW), lambda b, j: (b, 0, 0))],
        out_specs=pl.BlockSpec((WINDOW, A_W), lambda b, j: (b * nqb + j, 0)),
        out_shape=jax.ShapeDtypeStruct((N_LAT_TOK, A_W), F32),
        compiler_params=_cparams(2),
        name="attn_a_lat",
    )(sink, aq, akt_lat, akt_lat, akt_lat, av, av, av, ktx, vx)


B_TQ = 256


def _attn_b_body(lam_init, q, lam_ref, g_ref, segs):
    lp = lam_ref[...]
    lam = (jnp.exp(jnp.sum(lp[0:1, :] * lp[1:2, :], axis=1, keepdims=True))
           - jnp.exp(jnp.sum(lp[2:3, :] * lp[3:4, :], axis=1, keepdims=True)) + lam_init)
    outs = []
    for h in range(B_HEADS):
        maps = []
        for mi in range(2):
            c0 = h * B_DV + mi * B_DQ
            qm = (q[:, c0:c0 + B_DQ] * (B_DQ ** -0.5)).astype(BF16)
            ss = [jnp.dot(qm, kt[0, 2 * h + mi], preferred_element_type=F32) for kt, _ in segs]
            m = None
            for s in ss:
                sm = jnp.max(s, axis=-1, keepdims=True)
                m = sm if m is None else jnp.maximum(m, sm)
            l = None
            o = None
            for s, (_, v) in zip(ss, segs):
                p = jnp.exp(s - m)
                ps = jnp.sum(p, axis=-1, keepdims=True)
                l = ps if l is None else l + ps
                pv = jnp.dot(p.astype(BF16), v[0], preferred_element_type=F32)[:, h * B_DV:(h + 1) * B_DV]
                o = pv if o is None else o + pv
            maps.append(o / l)
        a = maps[0] - lam * maps[1]
        ms = jnp.mean(a * a, axis=-1, keepdims=True)
        outs.append(a * lax.rsqrt(ms + NORM_EPS) * g_ref[...] * (1.0 - lam_init))
    return outs


def _attn_b_ctx_kernel(lam_init, lam_ref, g_ref, q_ref, kt_ref, v_ref, o_ref):
    outs = _attn_b_body(lam_init, q_ref[...], lam_ref, g_ref, [(kt_ref, v_ref)])
    for h in range(B_HEADS):
        o_ref[:, h * B_DV:(h + 1) * B_DV] = outs[h]


def _attn_b_lat_kernel(lam_init, lam_ref, g_ref, q_ref, kt_ref, v_ref, ktx_ref, vx_ref, o_ref):
    outs = _attn_b_body(lam_init, q_ref[...], lam_ref, g_ref,
                        [(kt_ref, v_ref), (ktx_ref, vx_ref)])
    for h in range(B_HEADS):
        o_ref[:, h * B_DV:(h + 1) * B_DV] = outs[h]


def _attn_b_ctx_call(lam_init, lam_p, g, bq, bkt_ctx, bv_ctx):
    return pl.pallas_call(
        functools.partial(_attn_b_ctx_kernel, lam_init),
        grid=(NB_CTX,),
        in_specs=[pl.BlockSpec((4, B_DQ), lambda b: (0, 0)),
                  pl.BlockSpec((1, B_DV), lambda b: (0, 0)),
                  pl.BlockSpec((T_CTX, B_W), lambda b: (b, 0)),
                  pl.BlockSpec((1, 2 * B_HEADS, B_DQ, T_CTX), lambda b: (b, 0, 0, 0)),
                  pl.BlockSpec((1, T_CTX, B_W), lambda b: (b, 0, 0))],
        out_specs=pl.BlockSpec((T_CTX, B_W), lambda b: (b, 0)),
        out_shape=jax.ShapeDtypeStruct((N_CTX_TOK, B_W), F32),
        compiler_params=_cparams(1),
        name="attn_b_ctx",
    )(lam_p, g, bq, bkt_ctx, bv_ctx)


def _attn_b_lat_call(lam_init, lam_p, g, bq, bkt_lat, bv_lat, ktx, vx):
    nq = T_LAT // B_TQ
    ctx_blocks = N_CTX_TOK // B_TQ
    return pl.pallas_call(
        functools.partial(_attn_b_lat_kernel, lam_init),
        grid=(NB_LAT, nq),
        in_specs=[pl.BlockSpec((4, B_DQ), lambda b, j: (0, 0)),
                  pl.BlockSpec((1, B_DV), lambda b, j: (0, 0)),
                  pl.BlockSpec((B_TQ, B_W), lambda b, j: (ctx_blocks + b * nq + j, 0)),
                  pl.BlockSpec((1, 2 * B_HEADS, B_DQ, T_LAT), lambda b, j: (b, 0, 0, 0)),
                  pl.BlockSpec((1, T_LAT, B_W), lambda b, j: (b, 0, 0)),
                  pl.BlockSpec((1, 2 * B_HEADS, B_DQ, PAST), lambda b, j: (b, 0, 0, 0)),
                  pl.BlockSpec((1, PAST, B_W), lambda b, j: (b, 0, 0))],
        out_specs=pl.BlockSpec((B_TQ, B_W), lambda b, j: (b * nq + j, 0)),
        out_shape=jax.ShapeDtypeStruct((N_LAT_TOK, B_W), F32),
        compiler_params=_cparams(2),
        name="attn_b_lat",
    )(lam_p, g, bq, bkt_lat, bv_lat, ktx, vx)


HALO = 8


def _head_sum(x, ones_bf16):
    hi = x.astype(BF16)
    lo = (x - hi.astype(F32)).astype(BF16)
    return (jnp.dot(hi, ones_bf16, preferred_element_type=F32)
            + jnp.dot(lo, ones_bf16, preferred_element_type=F32))


def _rwkv_prep_kernel(rkv_ref, prev_ref, next_ref, cw_ref, ca_ref,
                      conv_ref, w0_ref, w2_ref, a0_ref, a2_ref, kk_ref, ka_ref, rk_ref, ones_ref,
                      r_ref, nkk_ref, v_ref, w_f, b_f, kd_f, w_b, b_b, kd_b, bonus_ref):
    i = pl.program_id(0)
    li = _lat_tile(i)
    is_ctx = i < N_CTX_TILES
    has_prev = jnp.logical_and(jnp.logical_not(is_ctx), li > 0).astype(F32)
    has_next = jnp.logical_and(jnp.logical_not(is_ctx), li < LAT_TILES_PER_SEQ - 1).astype(F32)
    x = rkv_ref[...]
    row = lax.broadcasted_iota(jnp.int32, x.shape, 0)
    xm = jnp.where(row == 0, prev_ref[HALO - 1:HALO, :] * has_prev, pltpu.roll(x, 1, axis=0))
    xp = jnp.where(row == TM - 1, next_ref[0:1, :] * has_next, pltpu.roll(x, TM - 1, axis=0))
    y = xm * conv_ref[0:1, :] + x * conv_ref[1:2, :] + xp * conv_ref[2:3, :]
    r, k, v = y[:, :C_W], y[:, C_W:2 * C_W], y[:, 2 * C_W:]
    ones = ones_ref[...]

    kk = k * kk_ref[...]
    kk = kk / jnp.maximum(jnp.sqrt(_head_sum(kk * kk, ones)), 1e-12)
    lw = jnp.dot(jnp.tanh(cw_ref[...]).astype(BF16), w2_ref[...],
                 preferred_element_type=F32)
    la = jnp.dot(ca_ref[...].astype(BF16), a2_ref[...], preferred_element_type=F32)
    r_ref[...] = r
    nkk_ref[...] = -kk
    v_ref[...] = v
    bonus = jnp.zeros_like(v)
    for d, (w_o, b_o, kd_o) in enumerate(((w_f, b_f, kd_f), (w_b, b_b, kd_b))):
        sl = slice(d * C_W, (d + 1) * C_W)
        a = _sigmoid(a0_ref[d:d + 1, :] + la[:, sl])
        kd = k * (1.0 + (a - 1.0) * ka_ref[...])
        w_o[...] = -DECAY_SCALE * _sigmoid(w0_ref[d:d + 1, :] + lw[:, sl])
        b_o[...] = kk * a
        kd_o[...] = kd
        bonus = bonus + _head_sum(r * kd * rk_ref[...], ones) * v
    bonus_ref[...] = bonus


def _rwkv_prep_call(rkv, cw, ca, p):
    nh = TM // HALO
    last = N_TOK // HALO - 1
    full = lambda shape: pl.BlockSpec(shape, lambda i: (0,) * len(shape))
    tile = lambda w: pl.BlockSpec((TM, w), lambda i: (i, 0))
    return pl.pallas_call(
        _rwkv_prep_kernel,
        grid=(N_TILES,),
        in_specs=[tile(3 * C_W),
                  pl.BlockSpec((HALO, 3 * C_W), lambda i: (jnp.maximum(i * nh - 1, 0), 0)),
                  pl.BlockSpec((HALO, 3 * C_W), lambda i: (jnp.minimum((i + 1) * nh, last), 0)),
                  tile(2 * C_RANK), tile(2 * C_RANK),
                  full((3, 3 * C_W)), full((2, C_W)), full((2 * C_RANK, 2 * C_W)),
                  full((2, C_W)), full((2 * C_RANK, 2 * C_W)),
                  full((1, C_W)), full((1, C_W)), full((1, C_W)), full((C_W, C_W))],
        out_specs=[tile(C_W)] * 10,
        out_shape=[jax.ShapeDtypeStruct((N_TOK, C_W), F32)] * 10,
        compiler_params=_cparams(1),
        name="rwkv_prep",
    )(rkv, rkv, rkv, cw, ca, p["conv"], p["w0"], p["w2"], p["a0"], p["a2"],
      p["kk"], p["ka"], p["rk"], p["ones"])


N_BLK = TM // LANES
N_PAIR = C_HEADS // 2


def _scan_kernel(r_f, nkk_f, lw_f, b_f, e_f, r_b, nkk_b, lw_b, b_b, e_b,
                 s0f_ref, s0b_ref, ef_ref, eb_ref,
                 yf_ref, yb_ref, sf_ref, sb_ref,
                 st, rows, et, rp, yp):
    first, last = _seq_edges(pl.program_id(0))
    lane = lax.broadcasted_iota(jnp.int32, (TM, LANES), 1)

    @pl.when(first)
    def _():
        st[...] = jnp.zeros_like(st)
        for d, s0 in enumerate((s0f_ref, s0b_ref)):
            for h in range(C_HEADS):
                o = (h % 2) * C_N
                st[d, h, :, o:o + C_N] = s0[0, h]

    for d, srcs in enumerate(((nkk_f, lw_f, b_f), (nkk_b, lw_b, b_b))):
        for a, src in enumerate(srcs):
            for pr in range(N_PAIR):
                chunk = src[:, pr * LANES:(pr + 1) * LANES]
                if a == 1:
                    chunk = jnp.exp(chunk)
                rows[d, a, 2 * pr] = jnp.where(lane < C_N, chunk, 0.0)
                rows[d, a, 2 * pr + 1] = jnp.where(lane < C_N, 0.0, chunk)
    for d, src in enumerate((e_f, e_b)):
        for pr in range(N_PAIR):
            for blk in range(N_BLK):
                tr = src[blk * LANES:(blk + 1) * LANES, pr * LANES:(pr + 1) * LANES].T
                for half in range(2):
                    tile = tr[half * C_N:(half + 1) * C_N, :]
                    swapped = pltpu.roll(tile, C_N, axis=1)
                    for q in range(2):
                        et[d, 2 * pr + half, 2 * blk + q] = tile if q != half else swapped

    for d, src in enumerate((r_f, r_b)):
        for pr in range(N_PAIR):
            rp[d, pr] = src[:, pr * LANES:(pr + 1) * LANES]

    lane1 = lax.broadcasted_iota(jnp.int32, (1, LANES), 1)

    def emit_y(d, t):
        for pr in range(N_PAIR):
            both = jnp.concatenate([st[d, 2 * pr].astype(BF16), st[d, 2 * pr + 1].astype(BF16)], axis=0)
            r8 = jnp.broadcast_to(rp[d, pr, pl.ds(t, 1), :], (8, LANES)).astype(BF16)
            y8 = lax.dot_general(r8, both, (((1,), (1,)), ((), ())), preferred_element_type=F32)
            yp[d, pr, pl.ds(t, 1), :] = y8[0:1, :]

    def step(i, carry):
        for d in range(2):
            t = i if d == 0 else TM - 1 - i
            emit_y(d, jnp.maximum(i - 1, 0) if d == 0 else jnp.minimum(TM - i, TM - 1))
            blk = t // C_N
            pick = (lane1 == (t % C_N) + C_N, lane1 == (t % C_N))
            for h in range(C_HEADS):
                s = st[d, h]
                nkk = rows[d, 0, h, pl.ds(t, 1), :]
                w = rows[d, 1, h, pl.ds(t, 1), :]
                b = rows[d, 2, h, pl.ds(t, 1), :]
                sa = jnp.sum(jnp.where(pick[h % 2], et[d, h, blk], s * nkk), axis=1, keepdims=True)
                st[d, h] = s * w + sa * b
        return carry

    lax.fori_loop(0, TM, step, 0, unroll=8)
    emit_y(0, TM - 1)
    emit_y(1, 0)

    for d, y_ref in enumerate((yf_ref, yb_ref)):
        for pr in range(N_PAIR):
            y_ref[:, pr * LANES:(pr + 1) * LANES] = yp[d, pr]

    @pl.when(last)
    def _():
        for d, (s_out, e_fin) in enumerate(((sf_ref, ef_ref), (sb_ref, eb_ref))):
            for h in range(C_HEADS):
                o = (h % 2) * C_N
                s_out[0, h] = st[d, h, :, o:o + C_N] + e_fin[0, h]


N_SEQ = NB_CTX + NB_LAT


def _lat_seq(i):
    return jnp.maximum(i - N_CTX_TILES, 0) // LAT_TILES_PER_SEQ


def _seq_of_tile(i):
    return jnp.where(i < N_CTX_TILES, i, NB_CTX + _lat_seq(i))


def _bwd_tile(i):
    rev = N_CTX_TILES + _lat_seq(i) * LAT_TILES_PER_SEQ + (LAT_TILES_PER_SEQ - 1 - _lat_tile(i))
    return jnp.where(i < N_CTX_TILES, i, rev)


def _seq_edges(i):
    is_ctx = i < N_CTX_TILES
    return (jnp.logical_or(is_ctx, _lat_tile(i) == 0),
            jnp.logical_or(is_ctx, _lat_tile(i) == LAT_TILES_PER_SEQ - 1))


_FWD_ROWS = pl.BlockSpec((TM, C_W), lambda i: (i, 0))
_BWD_ROWS = pl.BlockSpec((TM, C_W), lambda i: (_bwd_tile(i), 0))
_SEQ_STATE = pl.BlockSpec((1, C_HEADS, C_N, C_N), lambda i: (_seq_of_tile(i), 0, 0, 0))


def _scan_call(ins, s0f, s0b, e_fin_f, e_fin_b):
    r, nkk, lw_f, b_f, e_f, lw_b, b_b, e_b = ins
    return pl.pallas_call(
        _scan_kernel,
        grid=(N_TILES,),
        in_specs=[_FWD_ROWS] * 5 + [_BWD_ROWS] * 5 + [_SEQ_STATE] * 4,
        out_specs=[_FWD_ROWS, _BWD_ROWS, _SEQ_STATE, _SEQ_STATE],
        out_shape=[jax.ShapeDtypeStruct((N_TOK, C_W), F32)] * 2
                  + [jax.ShapeDtypeStruct((N_SEQ, C_HEADS, C_N, C_N), F32)] * 2,
        scratch_shapes=[pltpu.VMEM((2, C_HEADS, C_N, LANES), F32),
                        pltpu.VMEM((2, 3, C_HEADS, TM, LANES), F32),
                        pltpu.VMEM((2, C_HEADS, TM // C_N, C_N, LANES), F32),
                        pltpu.VMEM((2, N_PAIR, TM, LANES), F32),
                        pltpu.VMEM((2, N_PAIR, TM, LANES), F32)],
        compiler_params=_cparams(1),
        name="scan",
    )(r, nkk, lw_f, b_f, e_f, r, nkk, lw_b, b_b, e_b, s0f, s0b, e_fin_f, e_fin_b)


SUB = C_N


def _cum_rows(tri_bf16, x):
    hi = x.astype(BF16)
    lo = (x - hi.astype(F32)).astype(BF16)
    return (jnp.dot(tri_bf16, hi, preferred_element_type=F32)
            + jnp.dot(tri_bf16, lo, preferred_element_type=F32))


_NT = (((1,), (1,)), ((), ()))
_TN = (((0,), (0,)), ((), ()))


def _decay_kernel(r_f, nkk_f, v_f, lw_f, kd_f, r_b, nkk_b, v_b, lw_b, kd_b,
                  ef_ref, yef_ref, eb_ref, yeb_ref, stf_ref, stb_ref, est):
    first, last = _seq_edges(pl.program_id(0))

    @pl.when(first)
    def _():
        est[...] = jnp.zeros_like(est)

    ti = lax.broadcasted_iota(jnp.int32, (SUB, SUB), 0)
    si = lax.broadcasted_iota(jnp.int32, (SUB, SUB), 1)
    dirs = ((r_f, nkk_f, v_f, lw_f, kd_f, ef_ref, yef_ref), (r_b, nkk_b, v_b, lw_b, kd_b, eb_ref, yeb_ref))
    for d, (r_ref, nkk_ref, v_ref, lw_ref, kd_ref, e_out, ye_out) in enumerate(dirs):
        upto = (si <= ti) if d == 0 else (si >= ti)
        before = (si < ti) if d == 0 else (si > ti)
        both_masks = jnp.concatenate([before, upto], axis=0)
        tri = upto.astype(BF16)
        end_row = SUB - 1 if d == 0 else 0
        chunks = range(TM // SUB) if d == 0 else range(TM // SUB - 1, -1, -1)
        for c in chunks:
            rs = slice(c * SUB, (c + 1) * SUB)
            lw = lw_ref[rs, :]
            lg = _cum_rows(tri, lw)
            abar = nkk_ref[rs, :] * jnp.exp(lg - lw)
            rbar = r_ref[rs, :] * jnp.exp(lg)
            kbar = kd_ref[rs, :] * jnp.exp(-lg)
            g_end = jnp.exp(lg[end_row:end_row + 1, :])
            v = v_ref[rs, :]
            for h in range(C_HEADS):
                hs = slice(h * C_N, (h + 1) * C_N)
                lhs = jnp.concatenate([abar[:, hs], rbar[:, hs]], axis=0).astype(BF16)
                kh = kbar[:, hs].astype(BF16)
                vh = v[:, hs].astype(BF16)
                eh = est[d, h]
                sc = lax.dot_general(lhs, kh, _NT, preferred_element_type=F32)
                sc = jnp.where(both_masks, sc, 0.0).astype(BF16)
                out = (jnp.dot(sc, vh, preferred_element_type=F32)
                       + lax.dot_general(lhs, eh.astype(BF16), _NT, preferred_element_type=F32))
                e_out[rs, hs] = out[0:SUB, :]
                ye_out[rs, hs] = out[SUB:2 * SUB, :]
                est[d, h] = (eh + lax.dot_general(vh, kh, _TN, preferred_element_type=F32)) * g_end[:, hs]

    @pl.when(last)
    def _():
        stf_ref[0] = est[0]
        stb_ref[0] = est[1]


def _decay_call(ins):
    r, nkk, v, lw_f, kd_f, lw_b, kd_b = ins
    return pl.pallas_call(
        _decay_kernel,
        grid=(N_TILES,),
        in_specs=[_FWD_ROWS] * 5 + [_BWD_ROWS] * 5,
        out_specs=[_FWD_ROWS, _FWD_ROWS, _BWD_ROWS, _BWD_ROWS, _SEQ_STATE, _SEQ_STATE],
        out_shape=[jax.ShapeDtypeStruct((N_TOK, C_W), F32)] * 4
                  + [jax.ShapeDtypeStruct((N_SEQ, C_HEADS, C_N, C_N), F32)] * 2,
        scratch_shapes=[pltpu.VMEM((2, C_HEADS, C_N, C_N), F32)],
        compiler_params=_cparams(1),
        name="decay",
    )(r, nkk, v, lw_f, kd_f, r, nkk, v, lw_b, kd_b)


def _outproj_kernel(x_ref, mod_ref, ao_ref, bo_ref, yf_ref, yb_ref, yef_ref, yeb_ref, bonus_ref, cg_ref,
                    lng_ref, lnb_ref, g2_ref, mean_ref, wa_ref, wb_ref, wc_ref, o_ref):
    y = (yf_ref[...] + yef_ref[...]) + (yb_ref[...] + yeb_ref[...]) + bonus_ref[...]
    ones = mean_ref[...]
    mu = _head_sum(y, ones) * (1.0 / C_N)
    dy = y - mu
    var = _head_sum(dy * dy, ones) * (1.0 / C_N)
    yn = dy * lax.rsqrt(var + GN_EPS) * lng_ref[...] + lnb_ref[...]
    gate = jnp.dot(_sigmoid(cg_ref[...]).astype(BF16), g2_ref[...], preferred_element_type=F32)
    co = yn * gate
    m = (jnp.dot(ao_ref[...].astype(BF16), wa_ref[...], preferred_element_type=F32)
         + jnp.dot(bo_ref[...].astype(BF16), wb_ref[...], preferred_element_type=F32)
         + jnp.dot(co.astype(BF16), wc_ref[...], preferred_element_type=F32))
    o_ref[...] = x_ref[...] + mod_ref[0, 2:3, :] * m


def _outproj_call(x, mod_l, ao, bo, yf, yb, yef, yeb, bonus, cg, p):
    full = lambda shape: pl.BlockSpec(shape, lambda i: (0,) * len(shape))
    tile = lambda w: pl.BlockSpec((TM, w), lambda i: (i, 0))
    return pl.pallas_call(
        _outproj_kernel,
        grid=(N_TILES,),
        in_specs=[tile(D_MODEL),
                  pl.BlockSpec((1, 8, D_MODEL), lambda i: (_tile_group(i), 0, 0)),
                  tile(A_W), tile(B_W), tile(C_W), tile(C_W), tile(C_W), tile(C_W), tile(C_W),
                  tile(C_G_RANK),
                  full((1, C_W)), full((1, C_W)), full((C_G_RANK, C_W)), full((C_W, C_W)),
                  full((A_W, D_MODEL)), full((B_W, D_MODEL)), full((C_W, D_MODEL))],
        out_specs=tile(D_MODEL),
        out_shape=jax.ShapeDtypeStruct((N_TOK, D_MODEL), F32),
        compiler_params=_cparams(1),
        name="outproj",
    )(x, mod_l, ao, bo, yf, yb, yef, yeb, bonus, cg, p["lnx_g"], p["lnx_b"], p["g2"], p["mean"],
      p["wa"], p["wb"], p["wc"])


def _ffn_kernel(final, x_ref, mod_ref, g_ref, w1_ref, w3_ref, w2_ref, fg_ref, o_ref):
    x = x_ref[...]
    h = _modulated_norm(x, g_ref[...], mod_ref[0, 3:4, :], mod_ref[0, 4:5, :]).astype(BF16)
    u = jnp.dot(h, w1_ref[...], preferred_element_type=F32)
    t = jnp.dot(h, w3_ref[...], preferred_element_type=F32)
    act = (u * _sigmoid(u) * t).astype(BF16)
    y = x + mod_ref[0, 5:6, :] * jnp.dot(act, w2_ref[...], preferred_element_type=F32)
    if final:
        ms = jnp.mean(y * y, axis=-1, keepdims=True)
        y = y * lax.rsqrt(ms + NORM_EPS) * fg_ref[...]
    o_ref[...] = y


def _ffn_call(final, x, mod_l, g, w1, w3, w2, final_g):
    once = lambda shape: pl.BlockSpec(shape, lambda i: (0,) * len(shape),
                                      pipeline_mode=pl.Buffered(1))
    return pl.pallas_call(
        functools.partial(_ffn_kernel, final),
        grid=(N_TILES,),
        in_specs=[pl.BlockSpec((TM, D_MODEL), lambda i: (i, 0)),
                  pl.BlockSpec((1, 8, D_MODEL), lambda i: (_tile_group(i), 0, 0)),
                  once((1, D_MODEL)),
                  once((D_MODEL, D_FF)), once((D_MODEL, D_FF)), once((D_FF, D_MODEL)),
                  once((1, D_MODEL))],
        out_specs=pl.BlockSpec((TM, D_MODEL), lambda i: (i, 0)),
        out_shape=jax.ShapeDtypeStruct((N_TOK, D_MODEL), F32),
        compiler_params=_cparams(1),
        name="ffn",
    )(x, mod_l, g, w1, w3, w2, final_g)


def _block_diag2(m):
    z = jnp.zeros_like(m[0])
    return jnp.concatenate([jnp.concatenate([m[0], z], axis=1),
                            jnp.concatenate([z, m[1]], axis=1)], axis=0)


def _keys_t(k, nb, t, heads, dim):
    return k.reshape(nb, t, heads, dim).transpose(0, 2, 3, 1).astype(BF16)


def kernel(x_prompt, x_sample, cache_a_k, cache_a_v, cache_b_k, cache_b_v, state_c_fwd, state_c_bwd,
           c, c_ctx, ada_w, ada_b, norm1_g, norm2_g, w_in, a_sink, b_lambda, b_subln_g,
           c_conv, c_w0, c_w2, c_a0, c_a2, c_g2, c_kk, c_ka, c_rk, c_lnx_g, c_lnx_b,
           w_out, ffn_w1, ffn_w3, ffn_w2, final_g):
    x = jnp.concatenate([x_prompt.reshape(N_CTX_TOK, D_MODEL),
                         x_sample.reshape(N_LAT_TOK, D_MODEL)], axis=0)
    cvec8 = jnp.concatenate([c_ctx[None], c, jnp.zeros((8 - 1 - NB_LAT, D_MODEL), F32)], axis=0)
    mod = _ada_call(cvec8, ada_w, ada_b)
    mod = mod[:, :1 + NB_LAT].reshape(DEPTH, 1 + NB_LAT, 6, D_MODEL)
    mod = jnp.pad(mod, ((0, 0), (0, 0), (0, 2), (0, 0)))

    tabs = _rope_tables()
    head_id = np.arange(C_W) // C_N
    block_ones = jnp.asarray(head_id[:, None] == head_id[None, :], BF16)
    final_g2 = final_g.reshape(1, D_MODEL)
    s0_ctx = jnp.zeros((NB_CTX, C_HEADS, C_N, C_N), F32)

    new_ak, new_av, new_bk, new_bv, new_sf, new_sb = [], [], [], [], [], []
    for l in range(DEPTH):
        lam_init = 0.8 - 0.6 * math.exp(-0.3 * l)
        aq, ak, av, bq, bk, bv, rkv, cw, ca, cg = _inproj_call(
            x, mod[l], norm1_g[l].reshape(1, D_MODEL), w_in[l].astype(BF16), tabs)

        new_ak.append(ak[:N_CTX_TOK].reshape(NB_CTX, T_CTX, A_KV, HD))
        new_av.append(av[:N_CTX_TOK].reshape(NB_CTX, T_CTX, A_KV, HD))
        new_bk.append(bk[:N_CTX_TOK].reshape(NB_CTX, T_CTX, B_HEADS, 2, B_DQ))
        new_bv.append(bv[:N_CTX_TOK].reshape(NB_CTX, T_CTX, B_HEADS, B_DV))

        sink = a_sink[l]
        ao_ctx = _attn_a_ctx_call(sink, aq, _keys_t(ak[:N_CTX_TOK], NB_CTX, T_CTX, A_KV, HD), av)
        ao_lat = _attn_a_lat_call(
            sink, aq, _keys_t(ak[N_CTX_TOK:], NB_LAT, T_LAT, A_KV, HD), av,
            _keys_t(cache_a_k[:, l].reshape(NB_LAT * PAST, A_KVW), NB_LAT, PAST, A_KV, HD),
            cache_a_v[:, l].reshape(NB_LAT, PAST, A_KVW))
        ao = jnp.concatenate([ao_ctx, ao_lat], axis=0)

        lam_p, sub_g = b_lambda[l], b_subln_g[l].reshape(1, B_DV)
        bo_ctx = _attn_b_ctx_call(
            lam_init, lam_p, sub_g, bq, _keys_t(bk[:N_CTX_TOK], NB_CTX, T_CTX, 2 * B_HEADS, B_DQ),
            bv[:N_CTX_TOK].reshape(NB_CTX, T_CTX, B_W).astype(BF16))
        bo_lat = _attn_b_lat_call(
            lam_init, lam_p, sub_g, bq, _keys_t(bk[N_CTX_TOK:], NB_LAT, T_LAT, 2 * B_HEADS, B_DQ),
            bv[N_CTX_TOK:].reshape(NB_LAT, T_LAT, B_W).astype(BF16),
            _keys_t(cache_b_k[:, l].reshape(NB_LAT * PAST, B_W), NB_LAT, PAST, 2 * B_HEADS, B_DQ),
            cache_b_v[:, l].reshape(NB_LAT, PAST, B_W).astype(BF16))
        bo = jnp.concatenate([bo_ctx, bo_lat], axis=0)

        prep = _rwkv_prep_call(rkv, cw, ca, dict(
            conv=c_conv[l], w0=c_w0[l], w2=_block_diag2(c_w2[l]).astype(BF16),
            a0=c_a0[l], a2=_block_diag2(c_a2[l]).astype(BF16),
            kk=c_kk[l].reshape(1, C_W), ka=c_ka[l].reshape(1, C_W), rk=c_rk[l].reshape(1, C_W),
            ones=block_ones))
        r, nkk, v, lw_f, b_f, kd_f, lw_b, b_b, kd_b, bonus = prep
        e_f, yef, e_b, yeb, e_fin_f, e_fin_b = _decay_call((r, nkk, v, lw_f, kd_f, lw_b, kd_b))
        yf, yb, sf, sb = _scan_call(
            (r, nkk, lw_f, b_f, e_f, lw_b, b_b, e_b),
            jnp.concatenate([s0_ctx, state_c_fwd[:, l]], axis=0),
            jnp.concatenate([s0_ctx, state_c_bwd[:, l]], axis=0), e_fin_f, e_fin_b)
        new_sf.append(sf[:NB_CTX])
        new_sb.append(sb[:NB_CTX])

        wo = w_out[l].astype(BF16)
        x = _outproj_call(x, mod[l], ao, bo, yf, yb, yef, yeb, bonus, cg, dict(
            lnx_g=c_lnx_g[l].reshape(1, C_W), lnx_b=c_lnx_b[l].reshape(1, C_W),
            g2=c_g2[l].astype(BF16), mean=block_ones,
            wa=wo[:A_W], wb=wo[A_W:A_W + B_W], wc=wo[A_W + B_W:]))

        x = _ffn_call(l == DEPTH - 1, x, mod[l], norm2_g[l].reshape(1, D_MODEL),
                      ffn_w1[l].astype(BF16), ffn_w3[l].astype(BF16), ffn_w2[l].astype(BF16), final_g2)

    y_prompt = x[:N_CTX_TOK].reshape(NB_CTX, T_CTX, D_MODEL)
    y_sample = x[N_CTX_TOK:].reshape(NB_LAT, T_LAT, D_MODEL)
    return (y_prompt, y_sample,
            jnp.stack(new_ak, axis=1), jnp.stack(new_av, axis=1),
            jnp.stack(new_bk, axis=1), jnp.stack(new_bv, axis=1),
            jnp.stack(new_sf, axis=1), jnp.stack(new_sb, axis=1))
```

```python
import functools
import math

import numpy as np
import jax
import jax.numpy as jnp
from jax import lax
from jax.experimental import pallas as pl
from jax.experimental.pallas import tpu as pltpu

F32 = jnp.float32
BF16 = jnp.bfloat16

D_MODEL = 1024
DEPTH = 4
NB_CTX, T_CTX = 16, 256
NB_LAT, T_LAT = 2, 4096
PAST = 512
GRID_W = 64
HD = 64
A_HEADS, A_KV = 6, 2
A_W = A_HEADS * HD
A_KVW = A_KV * HD
WINDOW = 128
B_HEADS, B_DQ, B_DV = 4, 32, 64
B_W = B_HEADS * B_DV
C_HEADS, C_N = 6, 64
C_W = C_HEADS * C_N
C_RANK = 64
C_G_RANK = 128
D_FF = 2816
IN_COLS = 2944
ROPE_THETA = 10000.0
NORM_EPS = 1e-6
GN_EPS = 64e-5
DECAY_SCALE = 0.606531
NEG_INF = -1e30

TM = 256
N_CTX_TOK = NB_CTX * T_CTX
N_LAT_TOK = NB_LAT * T_LAT
N_TOK = N_CTX_TOK + N_LAT_TOK
N_CTX_TILES = N_CTX_TOK // TM
N_TILES = N_TOK // TM
LAT_TILES_PER_SEQ = T_LAT // TM
LANES = 128
VMEM_LIMIT = 56 * 1024 * 1024

O_AQ, O_AK, O_AV, O_BQ, O_BK, O_BV, O_RKV, O_CW, O_CA, O_CG = (
    0, 384, 512, 640, 896, 1152, 1408, 2560, 2688, 2816)


def _cparams(n_grid):
    return pltpu.CompilerParams(dimension_semantics=("arbitrary",) * n_grid,
                                vmem_limit_bytes=VMEM_LIMIT)


def _sigmoid(x):
    return 1.0 / (1.0 + jnp.exp(-x))


def _tile_group(i):
    return jnp.where(i < N_CTX_TILES, 0, 1 + (i - N_CTX_TILES) // LAT_TILES_PER_SEQ)


def _lat_tile(i):
    return jnp.maximum(i - N_CTX_TILES, 0) % LAT_TILES_PER_SEQ


ADA_TN = 1536


def _ada_kernel(c_ref, w_ref, b_ref, o_ref):
    c = c_ref[...]
    s = c * _sigmoid(c)
    o_ref[0] = jnp.dot(s.astype(BF16), w_ref[0].astype(BF16),
                       preferred_element_type=F32) + b_ref[0]


def _ada_call(cvec8, ada_w, ada_b):
    n = 6 * D_MODEL
    return pl.pallas_call(
        _ada_kernel,
        grid=(DEPTH, n // ADA_TN),
        in_specs=[pl.BlockSpec((8, D_MODEL), lambda l, j: (0, 0)),
                  pl.BlockSpec((1, D_MODEL, ADA_TN), lambda l, j: (l, 0, j)),
                  pl.BlockSpec((1, 1, ADA_TN), lambda l, j: (l, 0, j))],
        out_specs=pl.BlockSpec((1, 8, ADA_TN), lambda l, j: (l, 0, j)),
        out_shape=jax.ShapeDtypeStruct((DEPTH, 8, n), F32),
        compiler_params=_cparams(2),
        name="ada",
    )(cvec8, ada_w, ada_b.reshape(DEPTH, 1, n))


def _modulated_norm(x, g, shift, scale):
    ms = jnp.mean(x * x, axis=-1, keepdims=True)
    h = x * lax.rsqrt(ms + NORM_EPS) * g
    return h * (1.0 + scale) + shift


def _rope_chunk(x, cos, sin_lo, sin_hi, half):
    up = pltpu.roll(x, LANES - half, axis=1)
    dn = pltpu.roll(x, half, axis=1)
    return x * cos + up * sin_lo + dn * sin_hi


def _inproj_kernel(x_ref, mod_ref, g_ref, w_ref,
                   cos_a, sl_a, sh_a, cos_b, sl_b, sh_b,
                   aq_ref, ak_ref, av_ref, bq_ref, bk_ref, bv_ref,
                   rkv_ref, cw_ref, ca_ref, cg_ref, akt_ref, bkt_ref, bvb_ref):
    i = pl.program_id(0)
    h = _modulated_norm(x_ref[...], g_ref[...], mod_ref[0, 0:1, :], mod_ref[0, 1:2, :])
    z = jnp.dot(h.astype(BF16), w_ref[...], preferred_element_type=F32)
    av_ref[...] = z[:, O_AV:O_BQ]
    bv_ref[...] = z[:, O_BV:O_RKV]
    bvb_ref[...] = z[:, O_BV:O_RKV].astype(BF16)
    rkv_ref[...] = z[:, O_RKV:O_CW]
    cw_ref[...] = z[:, O_CW:O_CA]
    ca_ref[...] = z[:, O_CA:O_CG]
    cg_ref[...] = z[:, O_CG:IN_COLS]

    def emit_keys(ak, bk):
        ak_ref[...] = ak
        bk_ref[...] = bk
        akt_ref[...] = ak.T.astype(BF16)
        bkt_ref[...] = bk.T.astype(BF16)

    @pl.when(i < N_CTX_TILES)
    def _():
        aq_ref[...] = z[:, O_AQ:O_AK]
        bq_ref[...] = z[:, O_BQ:O_BK]
        emit_keys(z[:, O_AK:O_AV], z[:, O_BK:O_BV])

    @pl.when(i >= N_CTX_TILES)
    def _():
        ca_, la_, ha_ = cos_a[...], sl_a[...], sh_a[...]
        cb_, lb_, hb_ = cos_b[...], sl_b[...], sh_b[...]
        for j in range(A_W // LANES):
            o = O_AQ + j * LANES
            aq_ref[:, j * LANES:(j + 1) * LANES] = _rope_chunk(z[:, o:o + LANES], ca_, la_, ha_, 16)
        bks = []
        for j in range(B_W // LANES):
            o = O_BQ + j * LANES
            bq_ref[:, j * LANES:(j + 1) * LANES] = _rope_chunk(z[:, o:o + LANES], cb_, lb_, hb_, 8)
            o = O_BK + j * LANES
            bks.append(_rope_chunk(z[:, o:o + LANES], cb_, lb_, hb_, 8))
        emit_keys(_rope_chunk(z[:, O_AK:O_AV], ca_, la_, ha_, 16), jnp.concatenate(bks, axis=1))


def _inproj_call(x, mod_l, g, w_bf16, tabs):
    widths = (A_W, A_KVW, A_KVW, B_W, B_W, B_W, 3 * C_W, 2 * C_RANK, 2 * C_RANK, C_G_RANK)
    tab_spec = pl.BlockSpec((TM, LANES), lambda i: (_lat_tile(i), 0))
    return pl.pallas_call(
        _inproj_kernel,
        grid=(N_TILES,),
        in_specs=[pl.BlockSpec((TM, D_MODEL), lambda i: (i, 0)),
                  pl.BlockSpec((1, 8, D_MODEL), lambda i: (_tile_group(i), 0, 0)),
                  pl.BlockSpec((1, D_MODEL), lambda i: (0, 0)),
                  pl.BlockSpec((D_MODEL, IN_COLS), lambda i: (0, 0))] + [tab_spec] * 6,
        out_specs=[pl.BlockSpec((TM, w), lambda i: (i, 0)) for w in widths]
                  + [pl.BlockSpec((A_KVW, TM), lambda i: (0, i)),
                     pl.BlockSpec((B_W, TM), lambda i: (0, i)),
                     pl.BlockSpec((TM, B_W), lambda i: (i, 0))],
        out_shape=[jax.ShapeDtypeStruct((N_TOK, w), F32) for w in widths]
                  + [jax.ShapeDtypeStruct((A_KVW, N_TOK), BF16),
                     jax.ShapeDtypeStruct((B_W, N_TOK), BF16),
                     jax.ShapeDtypeStruct((N_TOK, B_W), BF16)],
        compiler_params=_cparams(1),
        name="inproj",
    )(x, mod_l, g, w_bf16, *tabs)


def _rope_tables():
    t = np.arange(T_LAT)
    rows, cols = t // GRID_W, t % GRID_W

    def build(width):
        half = width // 4
        d = width // 2
        inv = ROPE_THETA ** (-jnp.arange(0, d, 2, dtype=F32) / d)
        lane = np.arange(LANES) % width
        part = lane // d
        p = lane % d
        f = p % half
        pos = jnp.where(jnp.asarray(part)[None, :] == 0,
                        jnp.asarray(rows, F32)[:, None], jnp.asarray(cols, F32)[:, None])
        ang = pos * inv[jnp.asarray(f)][None, :]
        cos, sin = jnp.cos(ang), jnp.sin(ang)
        lo = jnp.asarray(p < half)[None, :]
        return cos, jnp.where(lo, -sin, 0.0), jnp.where(lo, 0.0, sin)

    return build(HD) + build(B_DQ)


def _attend_a(q, sink_ref, segs):
    outs = []
    for h in range(A_HEADS):
        g = h // (A_HEADS // A_KV)
        qh = (q[:, h * HD:(h + 1) * HD] * (HD ** -0.5)).astype(BF16)
        sink = sink_ref[h]
        ss = []
        m = None
        for kt, _, mask in segs:
            s = jnp.dot(qh, kt(g), preferred_element_type=F32)
            if mask is not None:
                s = jnp.where(mask, s, NEG_INF)
            ss.append(s)
            sm = jnp.max(s, axis=-1, keepdims=True)
            m = sm if m is None else jnp.maximum(m, sm)
        m = jnp.maximum(m, sink)
        l = jnp.exp(sink - m)
        o = None
        for s, (_, v, _) in zip(ss, segs):
            p = jnp.exp(s - m)
            l = l + jnp.sum(p, axis=-1, keepdims=True)
            pv = jnp.dot(p.astype(BF16), v().astype(BF16), preferred_element_type=F32)[:, g * HD:(g + 1) * HD]
            o = pv if o is None else o + pv
        outs.append(o / l)
    return outs


def _attn_a_ctx_kernel(sink_ref, q_ref, kt_ref, v_ref, o_ref):
    outs = _attend_a(q_ref[...], sink_ref,
                     [(lambda g: kt_ref[g * HD:(g + 1) * HD, :], lambda: v_ref[...], None)])
    for h in range(A_HEADS):
        o_ref[:, h * HD:(h + 1) * HD] = outs[h]


def _attn_a_lat_kernel(sink_ref, q_ref, ktp_ref, ktc_ref, ktn_ref, vp_ref, vc_ref, vn_ref,
                       ktx_ref, vx_ref, o_ref):
    qb = pl.program_id(1)
    nqb = pl.num_programs(1)
    qi = lax.broadcasted_iota(jnp.int32, (WINDOW, WINDOW), 0)
    kj = lax.broadcasted_iota(jnp.int32, (WINDOW, WINDOW), 1)
    mask_prev = (kj >= qi) & (qb > 0)
    mask_next = (kj <= qi) & (qb < nqb - 1)
    head_rows = lambda ref: (lambda g: ref[g * HD:(g + 1) * HD, :])
    segs = [(head_rows(ktp_ref), lambda: vp_ref[...], mask_prev),
            (head_rows(ktc_ref), lambda: vc_ref[...], None),
            (head_rows(ktn_ref), lambda: vn_ref[...], mask_next),
            (lambda g: ktx_ref[0, g], lambda: vx_ref[0], None)]
    outs = _attend_a(q_ref[...], sink_ref, segs)
    for h in range(A_HEADS):
        o_ref[:, h * HD:(h + 1) * HD] = outs[h]


def _attn_a_ctx_call(sink, aq, akt, av):
    return pl.pallas_call(
        _attn_a_ctx_kernel,
        grid=(NB_CTX,),
        in_specs=[pl.BlockSpec(memory_space=pltpu.SMEM),
                  pl.BlockSpec((T_CTX, A_W), lambda b: (b, 0)),
                  pl.BlockSpec((A_KVW, T_CTX), lambda b: (0, b)),
                  pl.BlockSpec((T_CTX, A_KVW), lambda b: (b, 0))],
        out_specs=pl.BlockSpec((T_CTX, A_W), lambda b: (b, 0)),
        out_shape=jax.ShapeDtypeStruct((N_CTX_TOK, A_W), F32),
        compiler_params=_cparams(1),
        name="attn_a_ctx",
    )(sink, aq, akt, av)


def _attn_a_lat_call(sink, aq, akt, av, ktx, vx):
    nqb = T_LAT // WINDOW
    ctx_blocks = N_CTX_TOK // WINDOW
    row = lambda b, j: ctx_blocks + b * nqb + j
    prev = lambda j: jnp.maximum(j - 1, 0)
    nxt = lambda j: jnp.minimum(j + 1, nqb - 1)
    kt_spec = lambda f: pl.BlockSpec((A_KVW, WINDOW), lambda b, j: (0, row(b, f(j))))
    v_spec = lambda f: pl.BlockSpec((WINDOW, A_KVW), lambda b, j: (row(b, f(j)), 0))
    same = lambda j: j
    return pl.pallas_call(
        _attn_a_lat_kernel,
        grid=(NB_LAT, nqb),
        in_specs=[pl.BlockSpec(memory_space=pltpu.SMEM),
                  pl.BlockSpec((WINDOW, A_W), lambda b, j: (row(b, j), 0)),
                  kt_spec(prev), kt_spec(same), kt_spec(nxt),
                  v_spec(prev), v_spec(same), v_spec(nxt),
                  pl.BlockSpec((1, A_KV, HD, PAST), lambda b, j: (b, 0, 0, 0)),
                  pl.BlockSpec((1, PAST, A_KVW), lambda b, j: (b, 0, 0))],
        out_specs=pl.BlockSpec((WINDOW, A_W), lambda b, j: (b * nqb + j, 0)),
        out_shape=jax.ShapeDtypeStruct((N_LAT_TOK, A_W), F32),
        compiler_params=_cparams(2),
        name="attn_a_lat",
    )(sink, aq, akt, akt, akt, av, av, av, ktx, vx)


B_TQ = 256


def _attn_b_body(lam_init, q, lam_ref, g_ref, segs):
    lp = lam_ref[...]
    lam = (jnp.exp(jnp.sum(lp[0:1, :] * lp[1:2, :], axis=1, keepdims=True))
           - jnp.exp(jnp.sum(lp[2:3, :] * lp[3:4, :], axis=1, keepdims=True)) + lam_init)
    outs = []
    for h in range(B_HEADS):
        maps = []
        for mi in range(2):
            c0 = h * B_DV + mi * B_DQ
            qm = (q[:, c0:c0 + B_DQ] * (B_DQ ** -0.5)).astype(BF16)
            ss = [jnp.dot(qm, kt(2 * h + mi), preferred_element_type=F32) for kt, _ in segs]
            m = None
            for s in ss:
                sm = jnp.max(s, axis=-1, keepdims=True)
                m = sm if m is None else jnp.maximum(m, sm)
            l = None
            o = None
            for s, (_, v) in zip(ss, segs):
                p = jnp.exp(s - m)
                ps = jnp.sum(p, axis=-1, keepdims=True)
                l = ps if l is None else l + ps
                pv = jnp.dot(p.astype(BF16), v(), preferred_element_type=F32)[:, h * B_DV:(h + 1) * B_DV]
                o = pv if o is None else o + pv
            maps.append(o / l)
        a = maps[0] - lam * maps[1]
        ms = jnp.mean(a * a, axis=-1, keepdims=True)
        outs.append(a * lax.rsqrt(ms + NORM_EPS) * g_ref[...] * (1.0 - lam_init))
    return outs


def _own_keys(kt_ref, v_ref):
    return (lambda hm: kt_ref[hm * B_DQ:(hm + 1) * B_DQ, :], lambda: v_ref[...])


def _attn_b_ctx_kernel(lam_init, lam_ref, g_ref, q_ref, kt_ref, v_ref, o_ref):
    outs = _attn_b_body(lam_init, q_ref[...], lam_ref, g_ref, [_own_keys(kt_ref, v_ref)])
    for h in range(B_HEADS):
        o_ref[:, h * B_DV:(h + 1) * B_DV] = outs[h]


def _attn_b_lat_kernel(lam_init, lam_ref, g_ref, q_ref, kt_ref, v_ref, ktx_ref, vx_ref, o_ref):
    outs = _attn_b_body(lam_init, q_ref[...], lam_ref, g_ref,
                        [_own_keys(kt_ref, v_ref), (lambda hm: ktx_ref[0, hm], lambda: vx_ref[0])])
    for h in range(B_HEADS):
        o_ref[:, h * B_DV:(h + 1) * B_DV] = outs[h]


def _attn_b_ctx_call(lam_init, lam_p, g, bq, bkt, bvb):
    return pl.pallas_call(
        functools.partial(_attn_b_ctx_kernel, lam_init),
        grid=(NB_CTX,),
        in_specs=[pl.BlockSpec((4, B_DQ), lambda b: (0, 0)),
                  pl.BlockSpec((1, B_DV), lambda b: (0, 0)),
                  pl.BlockSpec((T_CTX, B_W), lambda b: (b, 0)),
                  pl.BlockSpec((B_W, T_CTX), lambda b: (0, b)),
                  pl.BlockSpec((T_CTX, B_W), lambda b: (b, 0))],
        out_specs=pl.BlockSpec((T_CTX, B_W), lambda b: (b, 0)),
        out_shape=jax.ShapeDtypeStruct((N_CTX_TOK, B_W), F32),
        compiler_params=_cparams(1),
        name="attn_b_ctx",
    )(lam_p, g, bq, bkt, bvb)


def _attn_b_lat_call(lam_init, lam_p, g, bq, bkt, bvb, ktx, vx):
    nq = T_LAT // B_TQ
    ctx_blocks = N_CTX_TOK // B_TQ
    ctx_seqs = N_CTX_TOK // T_LAT
    return pl.pallas_call(
        functools.partial(_attn_b_lat_kernel, lam_init),
        grid=(NB_LAT, nq),
        in_specs=[pl.BlockSpec((4, B_DQ), lambda b, j: (0, 0)),
                  pl.BlockSpec((1, B_DV), lambda b, j: (0, 0)),
                  pl.BlockSpec((B_TQ, B_W), lambda b, j: (ctx_blocks + b * nq + j, 0)),
                  pl.BlockSpec((B_W, T_LAT), lambda b, j: (0, ctx_seqs + b)),
                  pl.BlockSpec((T_LAT, B_W), lambda b, j: (ctx_seqs + b, 0)),
                  pl.BlockSpec((1, 2 * B_HEADS, B_DQ, PAST), lambda b, j: (b, 0, 0, 0)),
                  pl.BlockSpec((1, PAST, B_W), lambda b, j: (b, 0, 0))],
        out_specs=pl.BlockSpec((B_TQ, B_W), lambda b, j: (b * nq + j, 0)),
        out_shape=jax.ShapeDtypeStruct((N_LAT_TOK, B_W), F32),
        compiler_params=_cparams(2),
        name="attn_b_lat",
    )(lam_p, g, bq, bkt, bvb, ktx, vx)


HALO = 8


def _head_sum(x, ones_bf16):
    hi = x.astype(BF16)
    lo = (x - hi.astype(F32)).astype(BF16)
    return (jnp.dot(hi, ones_bf16, preferred_element_type=F32)
            + jnp.dot(lo, ones_bf16, preferred_element_type=F32))


def _rwkv_prep_kernel(rkv_ref, prev_ref, next_ref, cw_ref, ca_ref,
                      conv_ref, w0_ref, w2_ref, a0_ref, a2_ref, kk_ref, ka_ref, rk_ref, ones_ref,
                      r_ref, nkk_ref, v_ref, w_f, b_f, kd_f, w_b, b_b, kd_b, bonus_ref):
    i = pl.program_id(0)
    li = _lat_tile(i)
    is_ctx = i < N_CTX_TILES
    has_prev = jnp.logical_and(jnp.logical_not(is_ctx), li > 0).astype(F32)
    has_next = jnp.logical_and(jnp.logical_not(is_ctx), li < LAT_TILES_PER_SEQ - 1).astype(F32)
    x = rkv_ref[...]
    row = lax.broadcasted_iota(jnp.int32, x.shape, 0)
    xm = jnp.where(row == 0, prev_ref[HALO - 1:HALO, :] * has_prev, pltpu.roll(x, 1, axis=0))
    xp = jnp.where(row == TM - 1, next_ref[0:1, :] * has_next, pltpu.roll(x, TM - 1, axis=0))
    y = xm * conv_ref[0:1, :] + x * conv_ref[1:2, :] + xp * conv_ref[2:3, :]
    r, k, v = y[:, :C_W], y[:, C_W:2 * C_W], y[:, 2 * C_W:]
    ones = ones_ref[...]

    kk = k * kk_ref[...]
    kk = kk / jnp.maximum(jnp.sqrt(_head_sum(kk * kk, ones)), 1e-12)
    lw = jnp.dot(jnp.tanh(cw_ref[...]).astype(BF16), w2_ref[...],
                 preferred_element_type=F32)
    la = jnp.dot(ca_ref[...].astype(BF16), a2_ref[...], preferred_element_type=F32)
    r_ref[...] = r
    nkk_ref[...] = -kk
    v_ref[...] = v
    bonus = jnp.zeros_like(v)
    for d, (w_o, b_o, kd_o) in enumerate(((w_f, b_f, kd_f), (w_b, b_b, kd_b))):
        sl = slice(d * C_W, (d + 1) * C_W)
        a = _sigmoid(a0_ref[d:d + 1, :] + la[:, sl])
        kd = k * (1.0 + (a - 1.0) * ka_ref[...])
        w_o[...] = -DECAY_SCALE * _sigmoid(w0_ref[d:d + 1, :] + lw[:, sl])
        b_o[...] = kk * a
        kd_o[...] = kd
        bonus = bonus + _head_sum(r * kd * rk_ref[...], ones) * v
    bonus_ref[...] = bonus


def _rwkv_prep_call(rkv, cw, ca, p):
    nh = TM // HALO
    last = N_TOK // HALO - 1
    full = lambda shape: pl.BlockSpec(shape, lambda i: (0,) * len(shape))
    tile = lambda w: pl.BlockSpec((TM, w), lambda i: (i, 0))
    return pl.pallas_call(
        _rwkv_prep_kernel,
        grid=(N_TILES,),
        in_specs=[tile(3 * C_W),
                  pl.BlockSpec((HALO, 3 * C_W), lambda i: (jnp.maximum(i * nh - 1, 0), 0)),
                  pl.BlockSpec((HALO, 3 * C_W), lambda i: (jnp.minimum((i + 1) * nh, last), 0)),
                  tile(2 * C_RANK), tile(2 * C_RANK),
                  full((3, 3 * C_W)), full((2, C_W)), full((2 * C_RANK, 2 * C_W)),
                  full((2, C_W)), full((2 * C_RANK, 2 * C_W)),
                  full((1, C_W)), full((1, C_W)), full((1, C_W)), full((C_W, C_W))],
        out_specs=[tile(C_W)] * 10,
        out_shape=[jax.ShapeDtypeStruct((N_TOK, C_W), F32)] * 10,
        compiler_params=_cparams(1),
        name="rwkv_prep",
    )(rkv, rkv, rkv, cw, ca, p["conv"], p["w0"], p["w2"], p["a0"], p["a2"],
      p["kk"], p["ka"], p["rk"], p["ones"])


N_BLK = TM // LANES
N_PAIR = C_HEADS // 2


def _scan_kernel(r_f, nkk_f, lw_f, b_f, e_f, r_b, nkk_b, lw_b, b_b, e_b,
                 s0f_ref, s0b_ref, ef_ref, eb_ref,
                 yf_ref, yb_ref, sf_ref, sb_ref,
                 st, rows, et, rp, yp):
    first, last = _seq_edges(pl.program_id(0))
    lane = lax.broadcasted_iota(jnp.int32, (TM, LANES), 1)

    @pl.when(first)
    def _():
        st[...] = jnp.zeros_like(st)
        for d, s0 in enumerate((s0f_ref, s0b_ref)):
            for h in range(C_HEADS):
                o = (h % 2) * C_N
                st[d, h, :, o:o + C_N] = s0[0, h]

    for d, srcs in enumerate(((nkk_f, lw_f, b_f), (nkk_b, lw_b, b_b))):
        for a, src in enumerate(srcs):
            for pr in range(N_PAIR):
                chunk = src[:, pr * LANES:(pr + 1) * LANES]
                if a == 1:
                    chunk = jnp.exp(chunk)
                rows[d, a, 2 * pr] = jnp.where(lane < C_N, chunk, 0.0)
                rows[d, a, 2 * pr + 1] = jnp.where(lane < C_N, 0.0, chunk)
    for d, src in enumerate((e_f, e_b)):
        for pr in range(N_PAIR):
            for blk in range(N_BLK):
                tr = src[blk * LANES:(blk + 1) * LANES, pr * LANES:(pr + 1) * LANES].T
                for half in range(2):
                    tile = tr[half * C_N:(half + 1) * C_N, :]
                    swapped = pltpu.roll(tile, C_N, axis=1)
                    for q in range(2):
                        et[d, 2 * pr + half, 2 * blk + q] = tile if q != half else swapped

    for d, src in enumerate((r_f, r_b)):
        for pr in range(N_PAIR):
            rp[d, pr] = src[:, pr * LANES:(pr + 1) * LANES]

    lane1 = lax.broadcasted_iota(jnp.int32, (1, LANES), 1)

    def emit_y(d, t):
        for pr in range(N_PAIR):
            both = jnp.concatenate([st[d, 2 * pr].astype(BF16), st[d, 2 * pr + 1].astype(BF16)], axis=0)
            r8 = jnp.broadcast_to(rp[d, pr, pl.ds(t, 1), :], (8, LANES)).astype(BF16)
            y8 = lax.dot_general(r8, both, (((1,), (1,)), ((), ())), preferred_element_type=F32)
            yp[d, pr, pl.ds(t, 1), :] = y8[0:1, :]

    def step(i, carry):
        for d in range(2):
            t = i if d == 0 else TM - 1 - i
            emit_y(d, jnp.maximum(i - 1, 0) if d == 0 else jnp.minimum(TM - i, TM - 1))
            blk = t // C_N
            pick = (lane1 == (t % C_N) + C_N, lane1 == (t % C_N))
            for h in range(C_HEADS):
                s = st[d, h]
                nkk = rows[d, 0, h, pl.ds(t, 1), :]
                w = rows[d, 1, h, pl.ds(t, 1), :]
                b = rows[d, 2, h, pl.ds(t, 1), :]
                sa = jnp.sum(jnp.where(pick[h % 2], et[d, h, blk], s * nkk), axis=1, keepdims=True)
                st[d, h] = s * w + sa * b
        return carry

    lax.fori_loop(0, TM, step, 0, unroll=8)
    emit_y(0, TM - 1)
    emit_y(1, 0)

    for d, y_ref in enumerate((yf_ref, yb_ref)):
        for pr in range(N_PAIR):
            y_ref[:, pr * LANES:(pr + 1) * LANES] = yp[d, pr]

    @pl.when(last)
    def _():
        for d, (s_out, e_fin) in enumerate(((sf_ref, ef_ref), (sb_ref, eb_ref))):
            for h in range(C_HEADS):
                o = (h % 2) * C_N
                s_out[0, h] = st[d, h, :, o:o + C_N] + e_fin[0, h]


N_SEQ = NB_CTX + NB_LAT


def _lat_seq(i):
    return jnp.maximum(i - N_CTX_TILES, 0) // LAT_TILES_PER_SEQ


def _seq_of_tile(i):
    return jnp.where(i < N_CTX_TILES, i, NB_CTX + _lat_seq(i))


def _bwd_tile(i):
    rev = N_CTX_TILES + _lat_seq(i) * LAT_TILES_PER_SEQ + (LAT_TILES_PER_SEQ - 1 - _lat_tile(i))
    return jnp.where(i < N_CTX_TILES, i, rev)


def _seq_edges(i):
    is_ctx = i < N_CTX_TILES
    return (jnp.logical_or(is_ctx, _lat_tile(i) == 0),
            jnp.logical_or(is_ctx, _lat_tile(i) == LAT_TILES_PER_SEQ - 1))


_FWD_ROWS = pl.BlockSpec((TM, C_W), lambda i: (i, 0))
_BWD_ROWS = pl.BlockSpec((TM, C_W), lambda i: (_bwd_tile(i), 0))
_SEQ_STATE = pl.BlockSpec((1, C_HEADS, C_N, C_N), lambda i: (_seq_of_tile(i), 0, 0, 0))


def _scan_call(ins, s0f, s0b, e_fin_f, e_fin_b):
    r, nkk, lw_f, b_f, e_f, lw_b, b_b, e_b = ins
    return pl.pallas_call(
        _scan_kernel,
        grid=(N_TILES,),
        in_specs=[_FWD_ROWS] * 5 + [_BWD_ROWS] * 5 + [_SEQ_STATE] * 4,
        out_specs=[_FWD_ROWS, _BWD_ROWS, _SEQ_STATE, _SEQ_STATE],
        out_shape=[jax.ShapeDtypeStruct((N_TOK, C_W), F32)] * 2
                  + [jax.ShapeDtypeStruct((N_SEQ, C_HEADS, C_N, C_N), F32)] * 2,
        scratch_shapes=[pltpu.VMEM((2, C_HEADS, C_N, LANES), F32),
                        pltpu.VMEM((2, 3, C_HEADS, TM, LANES), F32),
                        pltpu.VMEM((2, C_HEADS, TM // C_N, C_N, LANES), F32),
                        pltpu.VMEM((2, N_PAIR, TM, LANES), F32),
                        pltpu.VMEM((2, N_PAIR, TM, LANES), F32)],
        compiler_params=_cparams(1),
        name="scan",
    )(r, nkk, lw_f, b_f, e_f, r, nkk, lw_b, b_b, e_b, s0f, s0b, e_fin_f, e_fin_b)


SUB = C_N


def _cum_rows(tri_bf16, x):
    hi = x.astype(BF16)
    lo = (x - hi.astype(F32)).astype(BF16)
    return (jnp.dot(tri_bf16, hi, preferred_element_type=F32)
            + jnp.dot(tri_bf16, lo, preferred_element_type=F32))


_NT = (((1,), (1,)), ((), ()))
_TN = (((0,), (0,)), ((), ()))


def _decay_kernel(r_f, nkk_f, v_f, lw_f, kd_f, r_b, nkk_b, v_b, lw_b, kd_b,
                  ef_ref, yef_ref, eb_ref, yeb_ref, stf_ref, stb_ref, est):
    first, last = _seq_edges(pl.program_id(0))

    @pl.when(first)
    def _():
        est[...] = jnp.zeros_like(est)

    ti = lax.broadcasted_iota(jnp.int32, (SUB, SUB), 0)
    si = lax.broadcasted_iota(jnp.int32, (SUB, SUB), 1)
    dirs = ((r_f, nkk_f, v_f, lw_f, kd_f, ef_ref, yef_ref), (r_b, nkk_b, v_b, lw_b, kd_b, eb_ref, yeb_ref))
    for d, (r_ref, nkk_ref, v_ref, lw_ref, kd_ref, e_out, ye_out) in enumerate(dirs):
        upto = (si <= ti) if d == 0 else (si >= ti)
        before = (si < ti) if d == 0 else (si > ti)
        both_masks = jnp.concatenate([before, upto], axis=0)
        tri = upto.astype(BF16)
        end_row = SUB - 1 if d == 0 else 0
        chunks = range(TM // SUB) if d == 0 else range(TM // SUB - 1, -1, -1)
        for c in chunks:
            rs = slice(c * SUB, (c + 1) * SUB)
            lw = lw_ref[rs, :]
            lg = _cum_rows(tri, lw)
            abar = nkk_ref[rs, :] * jnp.exp(lg - lw)
            rbar = r_ref[rs, :] * jnp.exp(lg)
            kbar = kd_ref[rs, :] * jnp.exp(-lg)
            g_end = jnp.exp(lg[end_row:end_row + 1, :])
            v = v_ref[rs, :]
            for h in range(C_HEADS):
                hs = slice(h * C_N, (h + 1) * C_N)
                lhs = jnp.concatenate([abar[:, hs], rbar[:, hs]], axis=0).astype(BF16)
                kh = kbar[:, hs].astype(BF16)
                vh = v[:, hs].astype(BF16)
                eh = est[d, h]
                sc = lax.dot_general(lhs, kh, _NT, preferred_element_type=F32)
                sc = jnp.where(both_masks, sc, 0.0).astype(BF16)
                out = (jnp.dot(sc, vh, preferred_element_type=F32)
                       + lax.dot_general(lhs, eh.astype(BF16), _NT, preferred_element_type=F32))
                e_out[rs, hs] = out[0:SUB, :]
                ye_out[rs, hs] = out[SUB:2 * SUB, :]
                est[d, h] = (eh + lax.dot_general(vh, kh, _TN, preferred_element_type=F32)) * g_end[:, hs]

    @pl.when(last)
    def _():
        stf_ref[0] = est[0]
        stb_ref[0] = est[1]


def _decay_call(ins):
    r, nkk, v, lw_f, kd_f, lw_b, kd_b = ins
    return pl.pallas_call(
        _decay_kernel,
        grid=(N_TILES,),
        in_specs=[_FWD_ROWS] * 5 + [_BWD_ROWS] * 5,
        out_specs=[_FWD_ROWS, _FWD_ROWS, _BWD_ROWS, _BWD_ROWS, _SEQ_STATE, _SEQ_STATE],
        out_shape=[jax.ShapeDtypeStruct((N_TOK, C_W), F32)] * 4
                  + [jax.ShapeDtypeStruct((N_SEQ, C_HEADS, C_N, C_N), F32)] * 2,
        scratch_shapes=[pltpu.VMEM((2, C_HEADS, C_N, C_N), F32)],
        compiler_params=_cparams(1),
        name="decay",
    )(r, nkk, v, lw_f, kd_f, r, nkk, v, lw_b, kd_b)


def _outproj_kernel(x_ref, mod_ref, ao_ref, bo_ref, yf_ref, yb_ref, yef_ref, yeb_ref, bonus_ref, cg_ref,
                    lng_ref, lnb_ref, g2_ref, mean_ref, wa_ref, wb_ref, wc_ref, o_ref):
    y = (yf_ref[...] + yef_ref[...]) + (yb_ref[...] + yeb_ref[...]) + bonus_ref[...]
    ones = mean_ref[...]
    mu = _head_sum(y, ones) * (1.0 / C_N)
    dy = y - mu
    var = _head_sum(dy * dy, ones) * (1.0 / C_N)
    yn = dy * lax.rsqrt(var + GN_EPS) * lng_ref[...] + lnb_ref[...]
    gate = jnp.dot(_sigmoid(cg_ref[...]).astype(BF16), g2_ref[...], preferred_element_type=F32)
    co = yn * gate
    m = (jnp.dot(ao_ref[...].astype(BF16), wa_ref[...], preferred_element_type=F32)
         + jnp.dot(bo_ref[...].astype(BF16), wb_ref[...], preferred_element_type=F32)
         + jnp.dot(co.astype(BF16), wc_ref[...], preferred_element_type=F32))
    o_ref[...] = x_ref[...] + mod_ref[0, 2:3, :] * m


def _outproj_call(x, mod_l, ao, bo, yf, yb, yef, yeb, bonus, cg, p):
    full = lambda shape: pl.BlockSpec(shape, lambda i: (0,) * len(shape))
    tile = lambda w: pl.BlockSpec((TM, w), lambda i: (i, 0))
    return pl.pallas_call(
        _outproj_kernel,
        grid=(N_TILES,),
        in_specs=[tile(D_MODEL),
                  pl.BlockSpec((1, 8, D_MODEL), lambda i: (_tile_group(i), 0, 0)),
                  tile(A_W), tile(B_W), tile(C_W), tile(C_W), tile(C_W), tile(C_W), tile(C_W),
                  tile(C_G_RANK),
                  full((1, C_W)), full((1, C_W)), full((C_G_RANK, C_W)), full((C_W, C_W)),
                  full((A_W, D_MODEL)), full((B_W, D_MODEL)), full((C_W, D_MODEL))],
        out_specs=tile(D_MODEL),
        out_shape=jax.ShapeDtypeStruct((N_TOK, D_MODEL), F32),
        compiler_params=_cparams(1),
        name="outproj",
    )(x, mod_l, ao, bo, yf, yb, yef, yeb, bonus, cg, p["lnx_g"], p["lnx_b"], p["g2"], p["mean"],
      p["wa"], p["wb"], p["wc"])


def _ffn_kernel(final, x_ref, mod_ref, g_ref, w1_ref, w3_ref, w2_ref, fg_ref, o_ref):
    x = x_ref[...]
    h = _modulated_norm(x, g_ref[...], mod_ref[0, 3:4, :], mod_ref[0, 4:5, :]).astype(BF16)
    u = jnp.dot(h, w1_ref[...], preferred_element_type=F32)
    t = jnp.dot(h, w3_ref[...], preferred_element_type=F32)
    act = (u * _sigmoid(u) * t).astype(BF16)
    y = x + mod_ref[0, 5:6, :] * jnp.dot(act, w2_ref[...], preferred_element_type=F32)
    if final:
        ms = jnp.mean(y * y, axis=-1, keepdims=True)
        y = y * lax.rsqrt(ms + NORM_EPS) * fg_ref[...]
    o_ref[...] = y


def _ffn_call(final, x, mod_l, g, w1, w3, w2, final_g):
    once = lambda shape: pl.BlockSpec(shape, lambda i: (0,) * len(shape),
                                      pipeline_mode=pl.Buffered(1))
    return pl.pallas_call(
        functools.partial(_ffn_kernel, final),
        grid=(N_TILES,),
        in_specs=[pl.BlockSpec((TM, D_MODEL), lambda i: (i, 0)),
                  pl.BlockSpec((1, 8, D_MODEL), lambda i: (_tile_group(i), 0, 0)),
                  once((1, D_MODEL)),
                  once((D_MODEL, D_FF)), once((D_MODEL, D_FF)), once((D_FF, D_MODEL)),
                  once((1, D_MODEL))],
        out_specs=pl.BlockSpec((TM, D_MODEL), lambda i: (i, 0)),
        out_shape=jax.ShapeDtypeStruct((N_TOK, D_MODEL), F32),
        compiler_params=_cparams(1),
        name="ffn",
    )(x, mod_l, g, w1, w3, w2, final_g)


def _block_diag2(m):
    z = jnp.zeros_like(m[0])
    return jnp.concatenate([jnp.concatenate([m[0], z], axis=1),
                            jnp.concatenate([z, m[1]], axis=1)], axis=0)


def _keys_t(k, nb, t, heads, dim):
    return k.reshape(nb, t, heads, dim).transpose(0, 2, 3, 1).astype(BF16)


def kernel(x_prompt, x_sample, cache_a_k, cache_a_v, cache_b_k, cache_b_v, state_c_fwd, state_c_bwd,
           c, c_ctx, ada_w, ada_b, norm1_g, norm2_g, w_in, a_sink, b_lambda, b_subln_g,
           c_conv, c_w0, c_w2, c_a0, c_a2, c_g2, c_kk, c_ka, c_rk, c_lnx_g, c_lnx_b,
           w_out, ffn_w1, ffn_w3, ffn_w2, final_g):
    x = jnp.concatenate([x_prompt.reshape(N_CTX_TOK, D_MODEL),
                         x_sample.reshape(N_LAT_TOK, D_MODEL)], axis=0)
    cvec8 = jnp.concatenate([c_ctx[None], c, jnp.zeros((8 - 1 - NB_LAT, D_MODEL), F32)], axis=0)
    mod = _ada_call(cvec8, ada_w, ada_b)
    mod = mod[:, :1 + NB_LAT].reshape(DEPTH, 1 + NB_LAT, 6, D_MODEL)
    mod = jnp.pad(mod, ((0, 0), (0, 0), (0, 2), (0, 0)))

    tabs = _rope_tables()
    head_id = np.arange(C_W) // C_N
    block_ones = jnp.asarray(head_id[:, None] == head_id[None, :], BF16)
    final_g2 = final_g.reshape(1, D_MODEL)
    s0_ctx = jnp.zeros((NB_CTX, C_HEADS, C_N, C_N), F32)

    new_ak, new_av, new_bk, new_bv, new_sf, new_sb = [], [], [], [], [], []
    for l in range(DEPTH):
        lam_init = 0.8 - 0.6 * math.exp(-0.3 * l)
        aq, ak, av, bq, bk, bv, rkv, cw, ca, cg, akt, bkt, bvb = _inproj_call(
            x, mod[l], norm1_g[l].reshape(1, D_MODEL), w_in[l].astype(BF16), tabs)

        new_ak.append(ak[:N_CTX_TOK].reshape(NB_CTX, T_CTX, A_KV, HD))
        new_av.append(av[:N_CTX_TOK].reshape(NB_CTX, T_CTX, A_KV, HD))
        new_bk.append(bk[:N_CTX_TOK].reshape(NB_CTX, T_CTX, B_HEADS, 2, B_DQ))
        new_bv.append(bv[:N_CTX_TOK].reshape(NB_CTX, T_CTX, B_HEADS, B_DV))

        sink = a_sink[l]
        ao_ctx = _attn_a_ctx_call(sink, aq, akt, av)
        ao_lat = _attn_a_lat_call(
            sink, aq, akt, av,
            _keys_t(cache_a_k[:, l].reshape(NB_LAT * PAST, A_KVW), NB_LAT, PAST, A_KV, HD),
            cache_a_v[:, l].reshape(NB_LAT, PAST, A_KVW))
        ao = jnp.concatenate([ao_ctx, ao_lat], axis=0)

        lam_p, sub_g = b_lambda[l], b_subln_g[l].reshape(1, B_DV)
        bo_ctx = _attn_b_ctx_call(lam_init, lam_p, sub_g, bq, bkt, bvb)
        bo_lat = _attn_b_lat_call(
            lam_init, lam_p, sub_g, bq, bkt, bvb,
            _keys_t(cache_b_k[:, l].reshape(NB_LAT * PAST, B_W), NB_LAT, PAST, 2 * B_HEADS, B_DQ),
            cache_b_v[:, l].reshape(NB_LAT, PAST, B_W).astype(BF16))
        bo = jnp.concatenate([bo_ctx, bo_lat], axis=0)

        prep = _rwkv_prep_call(rkv, cw, ca, dict(
            conv=c_conv[l], w0=c_w0[l], w2=_block_diag2(c_w2[l]).astype(BF16),
            a0=c_a0[l], a2=_block_diag2(c_a2[l]).astype(BF16),
            kk=c_kk[l].reshape(1, C_W), ka=c_ka[l].reshape(1, C_W), rk=c_rk[l].reshape(1, C_W),
            ones=block_ones))
        r, nkk, v, lw_f, b_f, kd_f, lw_b, b_b, kd_b, bonus = prep
        e_f, yef, e_b, yeb, e_fin_f, e_fin_b = _decay_call((r, nkk, v, lw_f, kd_f, lw_b, kd_b))
        yf, yb, sf, sb = _scan_call(
            (r, nkk, lw_f, b_f, e_f, lw_b, b_b, e_b),
            jnp.concatenate([s0_ctx, state_c_fwd[:, l]], axis=0),
            jnp.concatenate([s0_ctx, state_c_bwd[:, l]], axis=0), e_fin_f, e_fin_b)
        new_sf.append(sf[:NB_CTX])
        new_sb.append(sb[:NB_CTX])

        wo = w_out[l].astype(BF16)
        x = _outproj_call(x, mod[l], ao, bo, yf, yb, yef, yeb, bonus, cg, dict(
            lnx_g=c_lnx_g[l].reshape(1, C_W), lnx_b=c_lnx_b[l].reshape(1, C_W),
            g2=c_g2[l].astype(BF16), mean=block_ones,
            wa=wo[:A_W], wb=wo[A_W:A_W + B_W], wc=wo[A_W + B_W:]))

        x = _ffn_call(l == DEPTH - 1, x, mod[l], norm2_g[l].reshape(1, D_MODEL),
                      ffn_w1[l].astype(BF16), ffn_w3[l].astype(BF16), ffn_w2[l].astype(BF16), final_g2)

    y_prompt = x[:N_CTX_TOK].reshape(NB_CTX, T_CTX, D_MODEL)
    y_sample = x[N_CTX_TOK:].reshape(NB_LAT, T_LAT, D_MODEL)
    return (y_prompt, y_sample,
            jnp.stack(new_ak, axis=1), jnp.stack(new_av, axis=1),
            jnp.stack(new_bk, axis=1), jnp.stack(new_bv, axis=1),
            jnp.stack(new_sf, axis=1), jnp.stack(new_sb, axis=1))
```

```python
import functools
import math

import numpy as np
import jax
import jax.numpy as jnp
from jax import lax
from jax.experimental import pallas as pl
from jax.experimental.pallas import tpu as pltpu

F32 = jnp.float32
BF16 = jnp.bfloat16

D_MODEL = 1024
DEPTH = 4
NB_CTX, T_CTX = 16, 256
NB_LAT, T_LAT = 2, 4096
PAST = 512
GRID_W = 64
HD = 64
A_HEADS, A_KV = 6, 2
A_W = A_HEADS * HD
A_KVW = A_KV * HD
WINDOW = 128
B_HEADS, B_DQ, B_DV = 4, 32, 64
B_W = B_HEADS * B_DV
C_HEADS, C_N = 6, 64
C_W = C_HEADS * C_N
C_RANK = 64
C_G_RANK = 128
D_FF = 2816
IN_COLS = 2944
ROPE_THETA = 10000.0
NORM_EPS = 1e-6
GN_EPS = 64e-5
DECAY_SCALE = 0.606531
NEG_INF = -1e30

TM = 256
N_CTX_TOK = NB_CTX * T_CTX
N_LAT_TOK = NB_LAT * T_LAT
N_TOK = N_CTX_TOK + N_LAT_TOK
N_CTX_TILES = N_CTX_TOK // TM
N_TILES = N_TOK // TM
LAT_TILES_PER_SEQ = T_LAT // TM
LANES = 128
VMEM_LIMIT = 56 * 1024 * 1024

O_AQ, O_AK, O_AV, O_BQ, O_BK, O_BV, O_RKV, O_CW, O_CA, O_CG = (
    0, 384, 512, 640, 896, 1152, 1408, 2560, 2688, 2816)


def _cparams(n_grid):
    return pltpu.CompilerParams(dimension_semantics=("arbitrary",) * n_grid,
                                vmem_limit_bytes=VMEM_LIMIT)


def _sigmoid(x):
    return 1.0 / (1.0 + jnp.exp(-x))


def _tile_group(i):
    return jnp.where(i < N_CTX_TILES, 0, 1 + (i - N_CTX_TILES) // LAT_TILES_PER_SEQ)


def _lat_tile(i):
    return jnp.maximum(i - N_CTX_TILES, 0) % LAT_TILES_PER_SEQ


ADA_TN = 1536


def _ada_kernel(c_ref, w_ref, b_ref, o_ref):
    c = c_ref[...]
    s = c * _sigmoid(c)
    o_ref[0] = jnp.dot(s.astype(BF16), w_ref[0].astype(BF16),
                       preferred_element_type=F32) + b_ref[0]


def _ada_call(cvec8, ada_w, ada_b):
    n = 6 * D_MODEL
    return pl.pallas_call(
        _ada_kernel,
        grid=(DEPTH, n // ADA_TN),
        in_specs=[pl.BlockSpec((8, D_MODEL), lambda l, j: (0, 0)),
                  pl.BlockSpec((1, D_MODEL, ADA_TN), lambda l, j: (l, 0, j)),
                  pl.BlockSpec((1, 1, ADA_TN), lambda l, j: (l, 0, j))],
        out_specs=pl.BlockSpec((1, 8, ADA_TN), lambda l, j: (l, 0, j)),
        out_shape=jax.ShapeDtypeStruct((DEPTH, 8, n), F32),
        compiler_params=_cparams(2),
        name="ada",
    )(cvec8, ada_w, ada_b.reshape(DEPTH, 1, n))


def _modulated_norm(x, g, shift, scale):
    ms = jnp.mean(x * x, axis=-1, keepdims=True)
    h = x * lax.rsqrt(ms + NORM_EPS) * g
    return h * (1.0 + scale) + shift


def _rope_chunk(x, cos, sin_lo, sin_hi, half):
    up = pltpu.roll(x, LANES - half, axis=1)
    dn = pltpu.roll(x, half, axis=1)
    return x * cos + up * sin_lo + dn * sin_hi


def _inproj_kernel(x_ref, mod_ref, g_ref, w_ref,
                   cos_a, sl_a, sh_a, cos_b, sl_b, sh_b,
                   aq_ref, ak_ref, av_ref, bq_ref, bk_ref, bv_ref,
                   rkv_ref, cw_ref, ca_ref, cg_ref, akt_ref, bkt_ref, bvb_ref):
    i = pl.program_id(0)
    h = _modulated_norm(x_ref[...], g_ref[...], mod_ref[0, 0:1, :], mod_ref[0, 1:2, :])
    z = jnp.dot(h.astype(BF16), w_ref[...], preferred_element_type=F32)
    av_ref[...] = z[:, O_AV:O_BQ]
    bv_ref[...] = z[:, O_BV:O_RKV]
    bvb_ref[...] = z[:, O_BV:O_RKV].astype(BF16)
    rkv_ref[...] = z[:, O_RKV:O_CW]
    cw_ref[...] = z[:, O_CW:O_CA]
    ca_ref[...] = z[:, O_CA:O_CG]
    cg_ref[...] = z[:, O_CG:IN_COLS]

    def emit_keys(ak, bk):
        ak_ref[...] = ak
        bk_ref[...] = bk
        akt_ref[...] = ak.T.astype(BF16)
        bkt_ref[...] = bk.T.astype(BF16)

    @pl.when(i < N_CTX_TILES)
    def _():
        aq_ref[...] = z[:, O_AQ:O_AK]
        bq_ref[...] = z[:, O_BQ:O_BK]
        emit_keys(z[:, O_AK:O_AV], z[:, O_BK:O_BV])

    @pl.when(i >= N_CTX_TILES)
    def _():
        ca_, la_, ha_ = cos_a[...], sl_a[...], sh_a[...]
        cb_, lb_, hb_ = cos_b[...], sl_b[...], sh_b[...]
        for j in range(A_W // LANES):
            o = O_AQ + j * LANES
            aq_ref[:, j * LANES:(j + 1) * LANES] = _rope_chunk(z[:, o:o + LANES], ca_, la_, ha_, 16)
        bks = []
        for j in range(B_W // LANES):
            o = O_BQ + j * LANES
            bq_ref[:, j * LANES:(j + 1) * LANES] = _rope_chunk(z[:, o:o + LANES], cb_, lb_, hb_, 8)
            o = O_BK + j * LANES
            bks.append(_rope_chunk(z[:, o:o + LANES], cb_, lb_, hb_, 8))
        emit_keys(_rope_chunk(z[:, O_AK:O_AV], ca_, la_, ha_, 16), jnp.concatenate(bks, axis=1))


def _inproj_call(x, mod_l, g, w_bf16, tabs):
    widths = (A_W, A_KVW, A_KVW, B_W, B_W, B_W, 3 * C_W, 2 * C_RANK, 2 * C_RANK, C_G_RANK)
    tab_spec = pl.BlockSpec((TM, LANES), lambda i: (_lat_tile(i), 0))
    return pl.pallas_call(
        _inproj_kernel,
        grid=(N_TILES,),
        in_specs=[pl.BlockSpec((TM, D_MODEL), lambda i: (i, 0)),
                  pl.BlockSpec((1, 8, D_MODEL), lambda i: (_tile_group(i), 0, 0)),
                  pl.BlockSpec((1, D_MODEL), lambda i: (0, 0)),
                  pl.BlockSpec((D_MODEL, IN_COLS), lambda i: (0, 0))] + [tab_spec] * 6,
        out_specs=[pl.BlockSpec((TM, w), lambda i: (i, 0)) for w in widths]
                  + [pl.BlockSpec((A_KVW, TM), lambda i: (0, i)),
                     pl.BlockSpec((B_W, TM), lambda i: (0, i)),
                     pl.BlockSpec((TM, B_W), lambda i: (i, 0))],
        out_shape=[jax.ShapeDtypeStruct((N_TOK, w), F32) for w in widths]
                  + [jax.ShapeDtypeStruct((A_KVW, N_TOK), BF16),
                     jax.ShapeDtypeStruct((B_W, N_TOK), BF16),
                     jax.ShapeDtypeStruct((N_TOK, B_W), BF16)],
        compiler_params=_cparams(1),
        name="inproj",
    )(x, mod_l, g, w_bf16, *tabs)


def _rope_tables():
    t = np.arange(T_LAT)
    rows, cols = t // GRID_W, t % GRID_W

    def build(width):
        half = width // 4
        d = width // 2
        inv = ROPE_THETA ** (-jnp.arange(0, d, 2, dtype=F32) / d)
        lane = np.arange(LANES) % width
        part = lane // d
        p = lane % d
        f = p % half
        pos = jnp.where(jnp.asarray(part)[None, :] == 0,
                        jnp.asarray(rows, F32)[:, None], jnp.asarray(cols, F32)[:, None])
        ang = pos * inv[jnp.asarray(f)][None, :]
        cos, sin = jnp.cos(ang), jnp.sin(ang)
        lo = jnp.asarray(p < half)[None, :]
        return cos, jnp.where(lo, -sin, 0.0), jnp.where(lo, 0.0, sin)

    return build(HD) + build(B_DQ)


def _attend_a(q, sink_ref, segs):
    outs = []
    for h in range(A_HEADS):
        g = h // (A_HEADS // A_KV)
        qh = (q[:, h * HD:(h + 1) * HD] * (HD ** -0.5)).astype(BF16)
        sink = sink_ref[h]
        ss = []
        m = None
        for kt, _, mask in segs:
            s = jnp.dot(qh, kt(g), preferred_element_type=F32)
            if mask is not None:
                s = jnp.where(mask, s, NEG_INF)
            ss.append(s)
            sm = jnp.max(s, axis=-1, keepdims=True)
            m = sm if m is None else jnp.maximum(m, sm)
        m = jnp.maximum(m, sink)
        l = jnp.exp(sink - m)
        o = None
        for s, (_, v, _) in zip(ss, segs):
            p = jnp.exp(s - m)
            l = l + jnp.sum(p, axis=-1, keepdims=True)
            pv = jnp.dot(p.astype(BF16), v().astype(BF16), preferred_element_type=F32)[:, g * HD:(g + 1) * HD]
            o = pv if o is None else o + pv
        outs.append(o / l)
    return outs


def _attn_a_ctx_kernel(sink_ref, q_ref, kt_ref, v_ref, o_ref):
    outs = _attend_a(q_ref[...], sink_ref,
                     [(lambda g: kt_ref[g * HD:(g + 1) * HD, :], lambda: v_ref[...], None)])
    for h in range(A_HEADS):
        o_ref[:, h * HD:(h + 1) * HD] = outs[h]


def _attn_a_lat_kernel(sink_ref, q_ref, ktp_ref, ktc_ref, ktn_ref, vp_ref, vc_ref, vn_ref,
                       ktx_ref, vx_ref, o_ref):
    qb = pl.program_id(1)
    nqb = pl.num_programs(1)
    qi = lax.broadcasted_iota(jnp.int32, (WINDOW, WINDOW), 0)
    kj = lax.broadcasted_iota(jnp.int32, (WINDOW, WINDOW), 1)
    mask_prev = (kj >= qi) & (qb > 0)
    mask_next = (kj <= qi) & (qb < nqb - 1)
    head_rows = lambda ref: (lambda g: ref[g * HD:(g + 1) * HD, :])
    segs = [(head_rows(ktp_ref), lambda: vp_ref[...], mask_prev),
            (head_rows(ktc_ref), lambda: vc_ref[...], None),
            (head_rows(ktn_ref), lambda: vn_ref[...], mask_next),
            (lambda g: ktx_ref[0, g], lambda: vx_ref[0], None)]
    outs = _attend_a(q_ref[...], sink_ref, segs)
    for h in range(A_HEADS):
        o_ref[:, h * HD:(h + 1) * HD] = outs[h]


def _attn_a_ctx_call(sink, aq, akt, av):
    return pl.pallas_call(
        _attn_a_ctx_kernel,
        grid=(NB_CTX,),
        in_specs=[pl.BlockSpec(memory_space=pltpu.SMEM),
                  pl.BlockSpec((T_CTX, A_W), lambda b: (b, 0)),
                  pl.BlockSpec((A_KVW, T_CTX), lambda b: (0, b)),
                  pl.BlockSpec((T_CTX, A_KVW), lambda b: (b, 0))],
        out_specs=pl.BlockSpec((T_CTX, A_W), lambda b: (b, 0)),
        out_shape=jax.ShapeDtypeStruct((N_CTX_TOK, A_W), F32),
        compiler_params=_cparams(1),
        name="attn_a_ctx",
    )(sink, aq, akt, av)


def _attn_a_lat_call(sink, aq, akt, av, ktx, vx):
    nqb = T_LAT // WINDOW
    ctx_blocks = N_CTX_TOK // WINDOW
    row = lambda b, j: ctx_blocks + b * nqb + j
    prev = lambda j: jnp.maximum(j - 1, 0)
    nxt = lambda j: jnp.minimum(j + 1, nqb - 1)
    kt_spec = lambda f: pl.BlockSpec((A_KVW, WINDOW), lambda b, j: (0, row(b, f(j))))
    v_spec = lambda f: pl.BlockSpec((WINDOW, A_KVW), lambda b, j: (row(b, f(j)), 0))
    same = lambda j: j
    return pl.pallas_call(
        _attn_a_lat_kernel,
        grid=(NB_LAT, nqb),
        in_specs=[pl.BlockSpec(memory_space=pltpu.SMEM),
                  pl.BlockSpec((WINDOW, A_W), lambda b, j: (row(b, j), 0)),
                  kt_spec(prev), kt_spec(same), kt_spec(nxt),
                  v_spec(prev), v_spec(same), v_spec(nxt),
                  pl.BlockSpec((1, A_KV, HD, PAST), lambda b, j: (b, 0, 0, 0)),
                  pl.BlockSpec((1, PAST, A_KVW), lambda b, j: (b, 0, 0))],
        out_specs=pl.BlockSpec((WINDOW, A_W), lambda b, j: (b * nqb + j, 0)),
        out_shape=jax.ShapeDtypeStruct((N_LAT_TOK, A_W), F32),
        compiler_params=_cparams(2),
        name="attn_a_lat",
    )(sink, aq, akt, akt, akt, av, av, av, ktx, vx)


B_TQ = 256


def _attn_b_body(lam_init, q, lam_ref, g_ref, segs):
    lp = lam_ref[...]
    lam = (jnp.exp(jnp.sum(lp[0:1, :] * lp[1:2, :], axis=1, keepdims=True))
           - jnp.exp(jnp.sum(lp[2:3, :] * lp[3:4, :], axis=1, keepdims=True)) + lam_init)
    outs = []
    for h in range(B_HEADS):
        maps = []
        for mi in range(2):
            c0 = h * B_DV + mi * B_DQ
            qm = (q[:, c0:c0 + B_DQ] * (B_DQ ** -0.5)).astype(BF16)
            ss = [jnp.dot(qm, kt(2 * h + mi), preferred_element_type=F32) for kt, _ in segs]
            m = None
            for s in ss:
                sm = jnp.max(s, axis=-1, keepdims=True)
                m = sm if m is None else jnp.maximum(m, sm)
            l = None
            o = None
            for s, (_, v) in zip(ss, segs):
                p = jnp.exp(s - m)
                ps = jnp.sum(p, axis=-1, keepdims=True)
                l = ps if l is None else l + ps
                pv = jnp.dot(p.astype(BF16), v(), preferred_element_type=F32)[:, h * B_DV:(h + 1) * B_DV]
                o = pv if o is None else o + pv
            maps.append(o / l)
        a = maps[0] - lam * maps[1]
        ms = jnp.mean(a * a, axis=-1, keepdims=True)
        outs.append(a * lax.rsqrt(ms + NORM_EPS) * g_ref[...] * (1.0 - lam_init))
    return outs


def _own_keys(kt_ref, v_ref):
    return (lambda hm: kt_ref[hm * B_DQ:(hm + 1) * B_DQ, :], lambda: v_ref[...])


def _attn_b_ctx_kernel(lam_init, lam_ref, g_ref, q_ref, kt_ref, v_ref, o_ref):
    outs = _attn_b_body(lam_init, q_ref[...], lam_ref, g_ref, [_own_keys(kt_ref, v_ref)])
    for h in range(B_HEADS):
        o_ref[:, h * B_DV:(h + 1) * B_DV] = outs[h]


def _attn_b_lat_kernel(lam_init, lam_ref, g_ref, q_ref, kt_ref, v_ref, ktx_ref, vx_ref, o_ref):
    outs = _attn_b_body(lam_init, q_ref[...], lam_ref, g_ref,
                        [_own_keys(kt_ref, v_ref), (lambda hm: ktx_ref[0, hm], lambda: vx_ref[0])])
    for h in range(B_HEADS):
        o_ref[:, h * B_DV:(h + 1) * B_DV] = outs[h]


def _attn_b_ctx_call(lam_init, lam_p, g, bq, bkt, bvb):
    return pl.pallas_call(
        functools.partial(_attn_b_ctx_kernel, lam_init),
        grid=(NB_CTX,),
        in_specs=[pl.BlockSpec((4, B_DQ), lambda b: (0, 0)),
                  pl.BlockSpec((1, B_DV), lambda b: (0, 0)),
                  pl.BlockSpec((T_CTX, B_W), lambda b: (b, 0)),
                  pl.BlockSpec((B_W, T_CTX), lambda b: (0, b)),
                  pl.BlockSpec((T_CTX, B_W), lambda b: (b, 0))],
        out_specs=pl.BlockSpec((T_CTX, B_W), lambda b: (b, 0)),
        out_shape=jax.ShapeDtypeStruct((N_CTX_TOK, B_W), F32),
        compiler_params=_cparams(1),
        name="attn_b_ctx",
    )(lam_p, g, bq, bkt, bvb)


def _attn_b_lat_call(lam_init, lam_p, g, bq, bkt, bvb, ktx, vx):
    nq = T_LAT // B_TQ
    ctx_blocks = N_CTX_TOK // B_TQ
    ctx_seqs = N_CTX_TOK // T_LAT
    return pl.pallas_call(
        functools.partial(_attn_b_lat_kernel, lam_init),
        grid=(NB_LAT, nq),
        in_specs=[pl.BlockSpec((4, B_DQ), lambda b, j: (0, 0)),
                  pl.BlockSpec((1, B_DV), lambda b, j: (0, 0)),
                  pl.BlockSpec((B_TQ, B_W), lambda b, j: (ctx_blocks + b * nq + j, 0)),
                  pl.BlockSpec((B_W, T_LAT), lambda b, j: (0, ctx_seqs + b)),
                  pl.BlockSpec((T_LAT, B_W), lambda b, j: (ctx_seqs + b, 0)),
                  pl.BlockSpec((1, 2 * B_HEADS, B_DQ, PAST), lambda b, j: (b, 0, 0, 0)),
                  pl.BlockSpec((1, PAST, B_W), lambda b, j: (b, 0, 0))],
        out_specs=pl.BlockSpec((B_TQ, B_W), lambda b, j: (b * nq + j, 0)),
        out_shape=jax.ShapeDtypeStruct((N_LAT_TOK, B_W), F32),
        compiler_params=_cparams(2),
        name="attn_b_lat",
    )(lam_p, g, bq, bkt, bvb, ktx, vx)


HALO = 8


def _head_sum(x, ones_bf16):
    hi = x.astype(BF16)
    lo = (x - hi.astype(F32)).astype(BF16)
    return (jnp.dot(hi, ones_bf16, preferred_element_type=F32)
            + jnp.dot(lo, ones_bf16, preferred_element_type=F32))


def _rwkv_prep_kernel(rkv_ref, prev_ref, next_ref, cw_ref, ca_ref,
                      conv_ref, w0_ref, w2_ref, a0_ref, a2_ref, kk_ref, ka_ref, rk_ref, ones_ref,
                      r_ref, nkk_ref, v_ref, w_f, b_f, kd_f, w_b, b_b, kd_b, bonus_ref):
    i = pl.program_id(0)
    li = _lat_tile(i)
    is_ctx = i < N_CTX_TILES
    has_prev = jnp.logical_and(jnp.logical_not(is_ctx), li > 0).astype(F32)
    has_next = jnp.logical_and(jnp.logical_not(is_ctx), li < LAT_TILES_PER_SEQ - 1).astype(F32)
    x = rkv_ref[...]
    row = lax.broadcasted_iota(jnp.int32, x.shape, 0)
    xm = jnp.where(row == 0, prev_ref[HALO - 1:HALO, :] * has_prev, pltpu.roll(x, 1, axis=0))
    xp = jnp.where(row == TM - 1, next_ref[0:1, :] * has_next, pltpu.roll(x, TM - 1, axis=0))
    y = xm * conv_ref[0:1, :] + x * conv_ref[1:2, :] + xp * conv_ref[2:3, :]
    r, k, v = y[:, :C_W], y[:, C_W:2 * C_W], y[:, 2 * C_W:]
    ones = ones_ref[...]

    kk = k * kk_ref[...]
    kk = kk / jnp.maximum(jnp.sqrt(_head_sum(kk * kk, ones)), 1e-12)
    lw = jnp.dot(jnp.tanh(cw_ref[...]).astype(BF16), w2_ref[...],
                 preferred_element_type=F32)
    la = jnp.dot(ca_ref[...].astype(BF16), a2_ref[...], preferred_element_type=F32)
    r_ref[...] = r
    nkk_ref[...] = -kk
    v_ref[...] = v
    bonus = jnp.zeros_like(v)
    for d, (w_o, b_o, kd_o) in enumerate(((w_f, b_f, kd_f), (w_b, b_b, kd_b))):
        sl = slice(d * C_W, (d + 1) * C_W)
        a = _sigmoid(a0_ref[d:d + 1, :] + la[:, sl])
        kd = k * (1.0 + (a - 1.0) * ka_ref[...])
        w_o[...] = -DECAY_SCALE * _sigmoid(w0_ref[d:d + 1, :] + lw[:, sl])
        b_o[...] = kk * a
        kd_o[...] = kd
        bonus = bonus + _head_sum(r * kd * rk_ref[...], ones) * v
    bonus_ref[...] = bonus


def _rwkv_prep_call(rkv, cw, ca, p):
    nh = TM // HALO
    last = N_TOK // HALO - 1
    full = lambda shape: pl.BlockSpec(shape, lambda i: (0,) * len(shape))
    tile = lambda w: pl.BlockSpec((TM, w), lambda i: (i, 0))
    return pl.pallas_call(
        _rwkv_prep_kernel,
        grid=(N_TILES,),
        in_specs=[tile(3 * C_W),
                  pl.BlockSpec((HALO, 3 * C_W), lambda i: (jnp.maximum(i * nh - 1, 0), 0)),
                  pl.BlockSpec((HALO, 3 * C_W), lambda i: (jnp.minimum((i + 1) * nh, last), 0)),
                  tile(2 * C_RANK), tile(2 * C_RANK),
                  full((3, 3 * C_W)), full((2, C_W)), full((2 * C_RANK, 2 * C_W)),
                  full((2, C_W)), full((2 * C_RANK, 2 * C_W)),
                  full((1, C_W)), full((1, C_W)), full((1, C_W)), full((C_W, C_W))],
        out_specs=[tile(C_W)] * 10,
        out_shape=[jax.ShapeDtypeStruct((N_TOK, C_W), F32)] * 10,
        compiler_params=_cparams(1),
        name="rwkv_prep",
    )(rkv, rkv, rkv, cw, ca, p["conv"], p["w0"], p["w2"], p["a0"], p["a2"],
      p["kk"], p["ka"], p["rk"], p["ones"])


N_BLK = TM // LANES
N_PAIR = C_HEADS // 2


def _scan_kernel(r_f, nkk_f, lw_f, b_f, e_f, r_b, nkk_b, lw_b, b_b, e_b,
                 s0f_ref, s0b_ref, ef_ref, eb_ref,
                 yf_ref, yb_ref, sf_ref, sb_ref,
                 st, rows, et, rp, yp):
    first, last = _seq_edges(pl.program_id(0))
    lane = lax.broadcasted_iota(jnp.int32, (TM, LANES), 1)

    @pl.when(first)
    def _():
        st[...] = jnp.zeros_like(st)
        for d, s0 in enumerate((s0f_ref, s0b_ref)):
            for h in range(C_HEADS):
                o = (h % 2) * C_N
                st[d, h, :, o:o + C_N] = s0[0, h]

    for d, srcs in enumerate(((nkk_f, lw_f, b_f), (nkk_b, lw_b, b_b))):
        for a, src in enumerate(srcs):
            for pr in range(N_PAIR):
                chunk = src[:, pr * LANES:(pr + 1) * LANES]
                if a == 1:
                    chunk = jnp.exp(chunk)
                rows[d, a, 2 * pr] = jnp.where(lane < C_N, chunk, 0.0)
                rows[d, a, 2 * pr + 1] = jnp.where(lane < C_N, 0.0, chunk)
    for d, src in enumerate((e_f, e_b)):
        for pr in range(N_PAIR):
            for blk in range(N_BLK):
                tr = src[blk * LANES:(blk + 1) * LANES, pr * LANES:(pr + 1) * LANES].T
                for half in range(2):
                    tile = tr[half * C_N:(half + 1) * C_N, :]
                    swapped = pltpu.roll(tile, C_N, axis=1)
                    for q in range(2):
                        et[d, 2 * pr + half, 2 * blk + q] = tile if q != half else swapped

    for d, src in enumerate((r_f, r_b)):
        for pr in range(N_PAIR):
            rp[d, pr] = src[:, pr * LANES:(pr + 1) * LANES]

    lane1 = lax.broadcasted_iota(jnp.int32, (1, LANES), 1)

    def emit_y(d, t):
        for pr in range(N_PAIR):
            both = jnp.concatenate([st[d, 2 * pr].astype(BF16), st[d, 2 * pr + 1].astype(BF16)], axis=0)
            r8 = jnp.broadcast_to(rp[d, pr, pl.ds(t, 1), :], (8, LANES)).astype(BF16)
            y8 = lax.dot_general(r8, both, (((1,), (1,)), ((), ())), preferred_element_type=F32)
            yp[d, pr, pl.ds(t, 1), :] = y8[0:1, :]

    def step(i, carry):
        for d in range(2):
            t = i if d == 0 else TM - 1 - i
            emit_y(d, jnp.maximum(i - 1, 0) if d == 0 else jnp.minimum(TM - i, TM - 1))
            blk = t // C_N
            pick = (lane1 == (t % C_N) + C_N, lane1 == (t % C_N))
            for h in range(C_HEADS):
                s = st[d, h]
                nkk = rows[d, 0, h, pl.ds(t, 1), :]
                w = rows[d, 1, h, pl.ds(t, 1), :]
                b = rows[d, 2, h, pl.ds(t, 1), :]
                sa = jnp.sum(jnp.where(pick[h % 2], et[d, h, blk], s * nkk), axis=1, keepdims=True)
                st[d, h] = s * w + sa * b
        return carry

    lax.fori_loop(0, TM, step, 0, unroll=8)
    emit_y(0, TM - 1)
    emit_y(1, 0)

    for d, y_ref in enumerate((yf_ref, yb_ref)):
        for pr in range(N_PAIR):
            y_ref[:, pr * LANES:(pr + 1) * LANES] = yp[d, pr]

    @pl.when(last)
    def _():
        for d, (s_out, e_fin) in enumerate(((sf_ref, ef_ref), (sb_ref, eb_ref))):
            for h in range(C_HEADS):
                o = (h % 2) * C_N
                s_out[0, h] = st[d, h, :, o:o + C_N] + e_fin[0, h]


N_SEQ = NB_CTX + NB_LAT


def _lat_seq(i):
    return jnp.maximum(i - N_CTX_TILES, 0) // LAT_TILES_PER_SEQ


def _seq_of_tile(i):
    return jnp.where(i < N_CTX_TILES, i, NB_CTX + _lat_seq(i))


def _bwd_tile(i):
    rev = N_CTX_TILES + _lat_seq(i) * LAT_TILES_PER_SEQ + (LAT_TILES_PER_SEQ - 1 - _lat_tile(i))
    return jnp.where(i < N_CTX_TILES, i, rev)


def _seq_edges(i):
    is_ctx = i < N_CTX_TILES
    return (jnp.logical_or(is_ctx, _lat_tile(i) == 0),
            jnp.logical_or(is_ctx, _lat_tile(i) == LAT_TILES_PER_SEQ - 1))


_FWD_ROWS = pl.BlockSpec((TM, C_W), lambda i: (i, 0))
_BWD_ROWS = pl.BlockSpec((TM, C_W), lambda i: (_bwd_tile(i), 0))
_SEQ_STATE = pl.BlockSpec((1, C_HEADS, C_N, C_N), lambda i: (_seq_of_tile(i), 0, 0, 0))


def _scan_call(ins, s0f, s0b, e_fin_f, e_fin_b):
    r, nkk, lw_f, b_f, e_f, lw_b, b_b, e_b = ins
    return pl.pallas_call(
        _scan_kernel,
        grid=(N_TILES,),
        in_specs=[_FWD_ROWS] * 5 + [_BWD_ROWS] * 5 + [_SEQ_STATE] * 4,
        out_specs=[_FWD_ROWS, _BWD_ROWS, _SEQ_STATE, _SEQ_STATE],
        out_shape=[jax.ShapeDtypeStruct((N_TOK, C_W), F32)] * 2
                  + [jax.ShapeDtypeStruct((N_SEQ, C_HEADS, C_N, C_N), F32)] * 2,
        scratch_shapes=[pltpu.VMEM((2, C_HEADS, C_N, LANES), F32),
                        pltpu.VMEM((2, 3, C_HEADS, TM, LANES), F32),
                        pltpu.VMEM((2, C_HEADS, TM // C_N, C_N, LANES), F32),
                        pltpu.VMEM((2, N_PAIR, TM, LANES), F32),
                        pltpu.VMEM((2, N_PAIR, TM, LANES), F32)],
        compiler_params=_cparams(1),
        name="scan",
    )(r, nkk, lw_f, b_f, e_f, r, nkk, lw_b, b_b, e_b, s0f, s0b, e_fin_f, e_fin_b)


SUB = C_N


def _cum_rows(tri_bf16, x):
    hi = x.astype(BF16)
    lo = (x - hi.astype(F32)).astype(BF16)
    return (jnp.dot(tri_bf16, hi, preferred_element_type=F32)
            + jnp.dot(tri_bf16, lo, preferred_element_type=F32))


_NT = (((1,), (1,)), ((), ()))
_TN = (((0,), (0,)), ((), ()))


def _decay_kernel(r_f, nkk_f, v_f, lw_f, kd_f, r_b, nkk_b, v_b, lw_b, kd_b,
                  ef_ref, yef_ref, eb_ref, yeb_ref, stf_ref, stb_ref, est):
    first, last = _seq_edges(pl.program_id(0))

    @pl.when(first)
    def _():
        est[...] = jnp.zeros_like(est)

    ti = lax.broadcasted_iota(jnp.int32, (SUB, SUB), 0)
    si = lax.broadcasted_iota(jnp.int32, (SUB, SUB), 1)
    dirs = ((r_f, nkk_f, v_f, lw_f, kd_f, ef_ref, yef_ref), (r_b, nkk_b, v_b, lw_b, kd_b, eb_ref, yeb_ref))
    for d, (r_ref, nkk_ref, v_ref, lw_ref, kd_ref, e_out, ye_out) in enumerate(dirs):
        upto = (si <= ti) if d == 0 else (si >= ti)
        before = (si < ti) if d == 0 else (si > ti)
        both_masks = jnp.concatenate([before, upto], axis=0)
        tri = upto.astype(BF16)
        end_row = SUB - 1 if d == 0 else 0
        chunks = range(TM // SUB) if d == 0 else range(TM // SUB - 1, -1, -1)
        for c in chunks:
            rs = slice(c * SUB, (c + 1) * SUB)
            lw = lw_ref[rs, :]
            lg = _cum_rows(tri, lw)
            abar = nkk_ref[rs, :] * jnp.exp(lg - lw)
            rbar = r_ref[rs, :] * jnp.exp(lg)
            kbar = kd_ref[rs, :] * jnp.exp(-lg)
            g_end = jnp.exp(lg[end_row:end_row + 1, :])
            v = v_ref[rs, :]
            for h in range(C_HEADS):
                hs = slice(h * C_N, (h + 1) * C_N)
                lhs = jnp.concatenate([abar[:, hs], rbar[:, hs]], axis=0).astype(BF16)
                kh = kbar[:, hs].astype(BF16)
                vh = v[:, hs].astype(BF16)
                eh = est[d, h]
                sc = lax.dot_general(lhs, kh, _NT, preferred_element_type=F32)
                sc = jnp.where(both_masks, sc, 0.0).astype(BF16)
                out = (jnp.dot(sc, vh, preferred_element_type=F32)
                       + lax.dot_general(lhs, eh.astype(BF16), _NT, preferred_element_type=F32))
                e_out[rs, hs] = out[0:SUB, :]
                ye_out[rs, hs] = out[SUB:2 * SUB, :]
                est[d, h] = (eh + lax.dot_general(vh, kh, _TN, preferred_element_type=F32)) * g_end[:, hs]

    @pl.when(last)
    def _():
        stf_ref[0] = est[0]
        stb_ref[0] = est[1]


def _decay_call(ins):
    r, nkk, v, lw_f, kd_f, lw_b, kd_b = ins
    return pl.pallas_call(
        _decay_kernel,
        grid=(N_TILES,),
        in_specs=[_FWD_ROWS] * 5 + [_BWD_ROWS] * 5,
        out_specs=[_FWD_ROWS, _FWD_ROWS, _BWD_ROWS, _BWD_ROWS, _SEQ_STATE, _SEQ_STATE],
        out_shape=[jax.ShapeDtypeStruct((N_TOK, C_W), F32)] * 4
                  + [jax.ShapeDtypeStruct((N_SEQ, C_HEADS, C_N, C_N), F32)] * 2,
        scratch_shapes=[pltpu.VMEM((2, C_HEADS, C_N, C_N), F32)],
        compiler_params=_cparams(1),
        name="decay",
    )(r, nkk, v, lw_f, kd_f, r, nkk, v, lw_b, kd_b)


def _outproj_kernel(x_ref, mod_ref, aoc_ref, aol_ref, boc_ref, bol_ref,
                    yf_ref, yb_ref, yef_ref, yeb_ref, bonus_ref, cg_ref,
                    lng_ref, lnb_ref, g2_ref, mean_ref, wa_ref, wb_ref, wc_ref, o_ref):
    is_ctx = pl.program_id(0) < N_CTX_TILES
    ao = jnp.where(is_ctx, aoc_ref[...], aol_ref[...])
    bo = jnp.where(is_ctx, boc_ref[...], bol_ref[...])
    y = (yf_ref[...] + yef_ref[...]) + (yb_ref[...] + yeb_ref[...]) + bonus_ref[...]
    ones = mean_ref[...]
    mu = _head_sum(y, ones) * (1.0 / C_N)
    dy = y - mu
    var = _head_sum(dy * dy, ones) * (1.0 / C_N)
    yn = dy * lax.rsqrt(var + GN_EPS) * lng_ref[...] + lnb_ref[...]
    gate = jnp.dot(_sigmoid(cg_ref[...]).astype(BF16), g2_ref[...], preferred_element_type=F32)
    co = yn * gate
    m = (jnp.dot(ao.astype(BF16), wa_ref[...], preferred_element_type=F32)
         + jnp.dot(bo.astype(BF16), wb_ref[...], preferred_element_type=F32)
         + jnp.dot(co.astype(BF16), wc_ref[...], preferred_element_type=F32))
    o_ref[...] = x_ref[...] + mod_ref[0, 2:3, :] * m


def _outproj_call(x, mod_l, ao_ctx, ao_lat, bo_ctx, bo_lat, yf, yb, yef, yeb, bonus, cg, p):
    full = lambda shape: pl.BlockSpec(shape, lambda i: (0,) * len(shape))
    tile = lambda w: pl.BlockSpec((TM, w), lambda i: (i, 0))
    ctx_tile = lambda w: pl.BlockSpec((TM, w), lambda i: (jnp.minimum(i, N_CTX_TILES - 1), 0))
    lat_tile = lambda w: pl.BlockSpec((TM, w), lambda i: (jnp.maximum(i - N_CTX_TILES, 0), 0))
    return pl.pallas_call(
        _outproj_kernel,
        grid=(N_TILES,),
        in_specs=[tile(D_MODEL),
                  pl.BlockSpec((1, 8, D_MODEL), lambda i: (_tile_group(i), 0, 0)),
                  ctx_tile(A_W), lat_tile(A_W), ctx_tile(B_W), lat_tile(B_W),
                  tile(C_W), tile(C_W), tile(C_W), tile(C_W), tile(C_W),
                  tile(C_G_RANK),
                  full((1, C_W)), full((1, C_W)), full((C_G_RANK, C_W)), full((C_W, C_W)),
                  full((A_W, D_MODEL)), full((B_W, D_MODEL)), full((C_W, D_MODEL))],
        out_specs=tile(D_MODEL),
        out_shape=jax.ShapeDtypeStruct((N_TOK, D_MODEL), F32),
        compiler_params=_cparams(1),
        name="outproj",
    )(x, mod_l, ao_ctx, ao_lat, bo_ctx, bo_lat, yf, yb, yef, yeb, bonus, cg,
      p["lnx_g"], p["lnx_b"], p["g2"], p["mean"],
      p["wa"], p["wb"], p["wc"])


def _ffn_kernel(final, x_ref, mod_ref, g_ref, w1_ref, w3_ref, w2_ref, fg_ref, o_ref):
    x = x_ref[...]
    h = _modulated_norm(x, g_ref[...], mod_ref[0, 3:4, :], mod_ref[0, 4:5, :]).astype(BF16)
    u = jnp.dot(h, w1_ref[...], preferred_element_type=F32)
    t = jnp.dot(h, w3_ref[...], preferred_element_type=F32)
    act = (u * _sigmoid(u) * t).astype(BF16)
    y = x + mod_ref[0, 5:6, :] * jnp.dot(act, w2_ref[...], preferred_element_type=F32)
    if final:
        ms = jnp.mean(y * y, axis=-1, keepdims=True)
        y = y * lax.rsqrt(ms + NORM_EPS) * fg_ref[...]
    o_ref[...] = y


def _ffn_call(final, x, mod_l, g, w1, w3, w2, final_g):
    once = lambda shape: pl.BlockSpec(shape, lambda i: (0,) * len(shape),
                                      pipeline_mode=pl.Buffered(1))
    return pl.pallas_call(
        functools.partial(_ffn_kernel, final),
        grid=(N_TILES,),
        in_specs=[pl.BlockSpec((TM, D_MODEL), lambda i: (i, 0)),
                  pl.BlockSpec((1, 8, D_MODEL), lambda i: (_tile_group(i), 0, 0)),
                  once((1, D_MODEL)),
                  once((D_MODEL, D_FF)), once((D_MODEL, D_FF)), once((D_FF, D_MODEL)),
                  once((1, D_MODEL))],
        out_specs=pl.BlockSpec((TM, D_MODEL), lambda i: (i, 0)),
        out_shape=jax.ShapeDtypeStruct((N_TOK, D_MODEL), F32),
        compiler_params=_cparams(1),
        name="ffn",
    )(x, mod_l, g, w1, w3, w2, final_g)


def _block_diag2(m):
    z = jnp.zeros_like(m[0])
    return jnp.concatenate([jnp.concatenate([m[0], z], axis=1),
                            jnp.concatenate([z, m[1]], axis=1)], axis=0)


def _keys_t(k, nb, t, heads, dim):
    return k.reshape(nb, t, heads, dim).transpose(0, 2, 3, 1).astype(BF16)


def kernel(x_prompt, x_sample, cache_a_k, cache_a_v, cache_b_k, cache_b_v, state_c_fwd, state_c_bwd,
           c, c_ctx, ada_w, ada_b, norm1_g, norm2_g, w_in, a_sink, b_lambda, b_subln_g,
           c_conv, c_w0, c_w2, c_a0, c_a2, c_g2, c_kk, c_ka, c_rk, c_lnx_g, c_lnx_b,
           w_out, ffn_w1, ffn_w3, ffn_w2, final_g):
    x = jnp.concatenate([x_prompt.reshape(N_CTX_TOK, D_MODEL),
                         x_sample.reshape(N_LAT_TOK, D_MODEL)], axis=0)
    cvec8 = jnp.concatenate([c_ctx[None], c, jnp.zeros((8 - 1 - NB_LAT, D_MODEL), F32)], axis=0)
    mod = _ada_call(cvec8, ada_w, ada_b)
    mod = mod[:, :1 + NB_LAT].reshape(DEPTH, 1 + NB_LAT, 6, D_MODEL)
    mod = jnp.pad(mod, ((0, 0), (0, 0), (0, 2), (0, 0)))

    tabs = _rope_tables()
    head_id = np.arange(C_W) // C_N
    block_ones = jnp.asarray(head_id[:, None] == head_id[None, :], BF16)
    final_g2 = final_g.reshape(1, D_MODEL)
    s0_ctx = jnp.zeros((NB_CTX, C_HEADS, C_N, C_N), F32)

    new_ak, new_av, new_bk, new_bv, new_sf, new_sb = [], [], [], [], [], []
    for l in range(DEPTH):
        lam_init = 0.8 - 0.6 * math.exp(-0.3 * l)
        aq, ak, av, bq, bk, bv, rkv, cw, ca, cg, akt, bkt, bvb = _inproj_call(
            x, mod[l], norm1_g[l].reshape(1, D_MODEL), w_in[l].astype(BF16), tabs)

        new_ak.append(ak[:N_CTX_TOK].reshape(NB_CTX, T_CTX, A_KV, HD))
        new_av.append(av[:N_CTX_TOK].reshape(NB_CTX, T_CTX, A_KV, HD))
        new_bk.append(bk[:N_CTX_TOK].reshape(NB_CTX, T_CTX, B_HEADS, 2, B_DQ))
        new_bv.append(bv[:N_CTX_TOK].reshape(NB_CTX, T_CTX, B_HEADS, B_DV))

        sink = a_sink[l]
        ao_ctx = _attn_a_ctx_call(sink, aq, akt, av)
        ao_lat = _attn_a_lat_call(
            sink, aq, akt, av,
            _keys_t(cache_a_k[:, l].reshape(NB_LAT * PAST, A_KVW), NB_LAT, PAST, A_KV, HD),
            cache_a_v[:, l].reshape(NB_LAT, PAST, A_KVW))

        lam_p, sub_g = b_lambda[l], b_subln_g[l].reshape(1, B_DV)
        bo_ctx = _attn_b_ctx_call(lam_init, lam_p, sub_g, bq, bkt, bvb)
        bo_lat = _attn_b_lat_call(
            lam_init, lam_p, sub_g, bq, bkt, bvb,
            _keys_t(cache_b_k[:, l].reshape(NB_LAT * PAST, B_W), NB_LAT, PAST, 2 * B_HEADS, B_DQ),
            cache_b_v[:, l].reshape(NB_LAT, PAST, B_W).astype(BF16))

        prep = _rwkv_prep_call(rkv, cw, ca, dict(
            conv=c_conv[l], w0=c_w0[l], w2=_block_diag2(c_w2[l]).astype(BF16),
            a0=c_a0[l], a2=_block_diag2(c_a2[l]).astype(BF16),
            kk=c_kk[l].reshape(1, C_W), ka=c_ka[l].reshape(1, C_W), rk=c_rk[l].reshape(1, C_W),
            ones=block_ones))
        r, nkk, v, lw_f, b_f, kd_f, lw_b, b_b, kd_b, bonus = prep
        e_f, yef, e_b, yeb, e_fin_f, e_fin_b = _decay_call((r, nkk, v, lw_f, kd_f, lw_b, kd_b))
        yf, yb, sf, sb = _scan_call(
            (r, nkk, lw_f, b_f, e_f, lw_b, b_b, e_b),
            jnp.concatenate([s0_ctx, state_c_fwd[:, l]], axis=0),
            jnp.concatenate([s0_ctx, state_c_bwd[:, l]], axis=0), e_fin_f, e_fin_b)
        new_sf.append(sf[:NB_CTX])
        new_sb.append(sb[:NB_CTX])

        wo = w_out[l].astype(BF16)
        x = _outproj_call(x, mod[l], ao_ctx, ao_lat, bo_ctx, bo_lat, yf, yb, yef, yeb, bonus, cg, dict(
            lnx_g=c_lnx_g[l].reshape(1, C_W), lnx_b=c_lnx_b[l].reshape(1, C_W),
            g2=c_g2[l].astype(BF16), mean=block_ones,
            wa=wo[:A_W], wb=wo[A_W:A_W + B_W], wc=wo[A_W + B_W:]))

        x = _ffn_call(l == DEPTH - 1, x, mod[l], norm2_g[l].reshape(1, D_MODEL),
                      ffn_w1[l].astype(BF16), ffn_w3[l].astype(BF16), ffn_w2[l].astype(BF16), final_g2)

    y_prompt = x[:N_CTX_TOK].reshape(NB_CTX, T_CTX, D_MODEL)
    y_sample = x[N_CTX_TOK:].reshape(NB_LAT, T_LAT, D_MODEL)
    return (y_prompt, y_sample,
            jnp.stack(new_ak, axis=1), jnp.stack(new_av, axis=1),
            jnp.stack(new_bk, axis=1), jnp.stack(new_bv, axis=1),
            jnp.stack(new_sf, axis=1), jnp.stack(new_sb, axis=1))
```

```python
import functools
import math

import numpy as np
import jax
import jax.numpy as jnp
from jax import lax
from jax.experimental import pallas as pl
from jax.experimental.pallas import tpu as pltpu

F32 = jnp.float32
BF16 = jnp.bfloat16

D_MODEL = 1024
DEPTH = 4
NB_CTX, T_CTX = 16, 256
NB_LAT, T_LAT = 2, 4096
PAST = 512
GRID_W = 64
HD = 64
A_HEADS, A_KV = 6, 2
A_W = A_HEADS * HD
A_KVW = A_KV * HD
WINDOW = 128
B_HEADS, B_DQ, B_DV = 4, 32, 64
B_W = B_HEADS * B_DV
C_HEADS, C_N = 6, 64
C_W = C_HEADS * C_N
C_RANK = 64
C_G_RANK = 128
D_FF = 2816
IN_COLS = 2944
ROPE_THETA = 10000.0
NORM_EPS = 1e-6
GN_EPS = 64e-5
DECAY_SCALE = 0.606531
NEG_INF = -1e30

TM = 256
N_CTX_TOK = NB_CTX * T_CTX
N_LAT_TOK = NB_LAT * T_LAT
N_TOK = N_CTX_TOK + N_LAT_TOK
N_CTX_TILES = N_CTX_TOK // TM
N_TILES = N_TOK // TM
LAT_TILES_PER_SEQ = T_LAT // TM
LANES = 128
VMEM_LIMIT = 56 * 1024 * 1024

O_AQ, O_AK, O_AV, O_BQ, O_BK, O_BV, O_RKV, O_CW, O_CA, O_CG = (
    0, 384, 512, 640, 896, 1152, 1408, 2560, 2688, 2816)


def _cparams(n_grid):
    return pltpu.CompilerParams(dimension_semantics=("arbitrary",) * n_grid,
                                vmem_limit_bytes=VMEM_LIMIT)


def _sigmoid(x):
    return 1.0 / (1.0 + jnp.exp(-x))


def _tile_group(i):
    return jnp.where(i < N_CTX_TILES, 0, 1 + (i - N_CTX_TILES) // LAT_TILES_PER_SEQ)


def _lat_tile(i):
    return jnp.maximum(i - N_CTX_TILES, 0) % LAT_TILES_PER_SEQ


ADA_TN = 1536


def _ada_kernel(c_ref, w_ref, b_ref, o_ref):
    c = c_ref[...]
    s = c * _sigmoid(c)
    o_ref[0] = jnp.dot(s.astype(BF16), w_ref[0].astype(BF16),
                       preferred_element_type=F32) + b_ref[0]


def _ada_call(cvec8, ada_w, ada_b):
    n = 6 * D_MODEL
    return pl.pallas_call(
        _ada_kernel,
        grid=(DEPTH, n // ADA_TN),
        in_specs=[pl.BlockSpec((8, D_MODEL), lambda l, j: (0, 0)),
                  pl.BlockSpec((1, D_MODEL, ADA_TN), lambda l, j: (l, 0, j)),
                  pl.BlockSpec((1, 1, ADA_TN), lambda l, j: (l, 0, j))],
        out_specs=pl.BlockSpec((1, 8, ADA_TN), lambda l, j: (l, 0, j)),
        out_shape=jax.ShapeDtypeStruct((DEPTH, 8, n), F32),
        compiler_params=_cparams(2),
        name="ada",
    )(cvec8, ada_w, ada_b.reshape(DEPTH, 1, n))


def _modulated_norm(x, g, shift, scale):
    ms = jnp.mean(x * x, axis=-1, keepdims=True)
    h = x * lax.rsqrt(ms + NORM_EPS) * g
    return h * (1.0 + scale) + shift


def _rope_chunk(x, cos, sin_lo, sin_hi, half):
    up = pltpu.roll(x, LANES - half, axis=1)
    dn = pltpu.roll(x, half, axis=1)
    return x * cos + up * sin_lo + dn * sin_hi


def _inproj_kernel(x_ref, mod_ref, g_ref, w_ref,
                   cos_a, sl_a, sh_a, cos_b, sl_b, sh_b,
                   aq_ref, ak_ref, av_ref, bq_ref, bk_ref, bv_ref,
                   rkv_ref, cw_ref, ca_ref, cg_ref, akt_ref, bkt_ref, bvb_ref):
    i = pl.program_id(0)
    h = _modulated_norm(x_ref[...], g_ref[...], mod_ref[0, 0:1, :], mod_ref[0, 1:2, :])
    z = jnp.dot(h.astype(BF16), w_ref[...], preferred_element_type=F32)
    av_ref[...] = z[:, O_AV:O_BQ]
    bv_ref[...] = z[:, O_BV:O_RKV]
    bvb_ref[...] = z[:, O_BV:O_RKV].astype(BF16)
    rkv_ref[...] = z[:, O_RKV:O_CW]
    cw_ref[...] = z[:, O_CW:O_CA]
    ca_ref[...] = z[:, O_CA:O_CG]
    cg_ref[...] = z[:, O_CG:IN_COLS]

    def emit_keys(ak, bk):
        ak_ref[...] = ak
        bk_ref[...] = bk
        akt_ref[...] = ak.T.astype(BF16)
        bkt_ref[...] = bk.T.astype(BF16)

    @pl.when(i < N_CTX_TILES)
    def _():
        aq_ref[...] = z[:, O_AQ:O_AK]
        bq_ref[...] = z[:, O_BQ:O_BK]
        emit_keys(z[:, O_AK:O_AV], z[:, O_BK:O_BV])

    @pl.when(i >= N_CTX_TILES)
    def _():
        ca_, la_, ha_ = cos_a[...], sl_a[...], sh_a[...]
        cb_, lb_, hb_ = cos_b[...], sl_b[...], sh_b[...]
        for j in range(A_W // LANES):
            o = O_AQ + j * LANES
            aq_ref[:, j * LANES:(j + 1) * LANES] = _rope_chunk(z[:, o:o + LANES], ca_, la_, ha_, 16)
        bks = []
        for j in range(B_W // LANES):
            o = O_BQ + j * LANES
            bq_ref[:, j * LANES:(j + 1) * LANES] = _rope_chunk(z[:, o:o + LANES], cb_, lb_, hb_, 8)
            o = O_BK + j * LANES
            bks.append(_rope_chunk(z[:, o:o + LANES], cb_, lb_, hb_, 8))
        emit_keys(_rope_chunk(z[:, O_AK:O_AV], ca_, la_, ha_, 16), jnp.concatenate(bks, axis=1))


def _inproj_call(x, mod_l, g, w_bf16, tabs):
    widths = (A_W, A_KVW, A_KVW, B_W, B_W, B_W, 3 * C_W, 2 * C_RANK, 2 * C_RANK, C_G_RANK)
    tab_spec = pl.BlockSpec((TM, LANES), lambda i: (_lat_tile(i), 0))
    return pl.pallas_call(
        _inproj_kernel,
        grid=(N_TILES,),
        in_specs=[pl.BlockSpec((TM, D_MODEL), lambda i: (i, 0)),
                  pl.BlockSpec((1, 8, D_MODEL), lambda i: (_tile_group(i), 0, 0)),
                  pl.BlockSpec((1, D_MODEL), lambda i: (0, 0)),
                  pl.BlockSpec((D_MODEL, IN_COLS), lambda i: (0, 0))] + [tab_spec] * 6,
        out_specs=[pl.BlockSpec((TM, w), lambda i: (i, 0)) for w in widths]
                  + [pl.BlockSpec((A_KVW, TM), lambda i: (0, i)),
                     pl.BlockSpec((B_W, TM), lambda i: (0, i)),
                     pl.BlockSpec((TM, B_W), lambda i: (i, 0))],
        out_shape=[jax.ShapeDtypeStruct((N_TOK, w), F32) for w in widths]
                  + [jax.ShapeDtypeStruct((A_KVW, N_TOK), BF16),
                     jax.ShapeDtypeStruct((B_W, N_TOK), BF16),
                     jax.ShapeDtypeStruct((N_TOK, B_W), BF16)],
        compiler_params=_cparams(1),
        name="inproj",
    )(x, mod_l, g, w_bf16, *tabs)


def _rope_tables():
    t = np.arange(T_LAT)
    rows, cols = t // GRID_W, t % GRID_W

    def build(width):
        half = width // 4
        d = width // 2
        inv = ROPE_THETA ** (-jnp.arange(0, d, 2, dtype=F32) / d)
        lane = np.arange(LANES) % width
        part = lane // d
        p = lane % d
        f = p % half
        pos = jnp.where(jnp.asarray(part)[None, :] == 0,
                        jnp.asarray(rows, F32)[:, None], jnp.asarray(cols, F32)[:, None])
        ang = pos * inv[jnp.asarray(f)][None, :]
        cos, sin = jnp.cos(ang), jnp.sin(ang)
        lo = jnp.asarray(p < half)[None, :]
        return cos, jnp.where(lo, -sin, 0.0), jnp.where(lo, 0.0, sin)

    return build(HD) + build(B_DQ)


def _attend_a(q, sink_ref, segs):
    outs = []
    for h in range(A_HEADS):
        g = h // (A_HEADS // A_KV)
        qh = (q[:, h * HD:(h + 1) * HD] * (HD ** -0.5)).astype(BF16)
        sink = sink_ref[h]
        ss = []
        m = None
        for kt, _, mask in segs:
            s = jnp.dot(qh, kt(g), preferred_element_type=F32)
            if mask is not None:
                s = jnp.where(mask, s, NEG_INF)
            ss.append(s)
            sm = jnp.max(s, axis=-1, keepdims=True)
            m = sm if m is None else jnp.maximum(m, sm)
        m = jnp.maximum(m, sink)
        l = jnp.exp(sink - m)
        o = None
        for s, (_, v, _) in zip(ss, segs):
            p = jnp.exp(s - m)
            l = l + jnp.sum(p, axis=-1, keepdims=True)
            pv = jnp.dot(p.astype(BF16), v().astype(BF16), preferred_element_type=F32)[:, g * HD:(g + 1) * HD]
            o = pv if o is None else o + pv
        outs.append(o / l)
    return outs


def _attn_a_ctx_kernel(sink_ref, q_ref, kt_ref, v_ref, o_ref):
    outs = _attend_a(q_ref[...], sink_ref,
                     [(lambda g: kt_ref[g * HD:(g + 1) * HD, :], lambda: v_ref[...], None)])
    for h in range(A_HEADS):
        o_ref[:, h * HD:(h + 1) * HD] = outs[h]


def _attn_a_lat_kernel(sink_ref, q_ref, ktp_ref, ktc_ref, ktn_ref, vp_ref, vc_ref, vn_ref,
                       ktx_ref, vx_ref, o_ref):
    qb = pl.program_id(1)
    nqb = pl.num_programs(1)
    qi = lax.broadcasted_iota(jnp.int32, (WINDOW, WINDOW), 0)
    kj = lax.broadcasted_iota(jnp.int32, (WINDOW, WINDOW), 1)
    mask_prev = (kj >= qi) & (qb > 0)
    mask_next = (kj <= qi) & (qb < nqb - 1)
    head_rows = lambda ref: (lambda g: ref[g * HD:(g + 1) * HD, :])
    segs = [(head_rows(ktp_ref), lambda: vp_ref[...], mask_prev),
            (head_rows(ktc_ref), lambda: vc_ref[...], None),
            (head_rows(ktn_ref), lambda: vn_ref[...], mask_next),
            (lambda g: ktx_ref[0, g], lambda: vx_ref[0], None)]
    outs = _attend_a(q_ref[...], sink_ref, segs)
    for h in range(A_HEADS):
        o_ref[:, h * HD:(h + 1) * HD] = outs[h]


def _attn_a_ctx_call(sink, aq, akt, av):
    return pl.pallas_call(
        _attn_a_ctx_kernel,
        grid=(NB_CTX,),
        in_specs=[pl.BlockSpec(memory_space=pltpu.SMEM),
                  pl.BlockSpec((T_CTX, A_W), lambda b: (b, 0)),
                  pl.BlockSpec((A_KVW, T_CTX), lambda b: (0, b)),
                  pl.BlockSpec((T_CTX, A_KVW), lambda b: (b, 0))],
        out_specs=pl.BlockSpec((T_CTX, A_W), lambda b: (b, 0)),
        out_shape=jax.ShapeDtypeStruct((N_CTX_TOK, A_W), F32),
        compiler_params=_cparams(1),
        name="attn_a_ctx",
    )(sink, aq, akt, av)


def _attn_a_lat_call(sink, aq, akt, av, ktx, vx):
    nqb = T_LAT // WINDOW
    ctx_blocks = N_CTX_TOK // WINDOW
    row = lambda b, j: ctx_blocks + b * nqb + j
    prev = lambda j: jnp.maximum(j - 1, 0)
    nxt = lambda j: jnp.minimum(j + 1, nqb - 1)
    kt_spec = lambda f: pl.BlockSpec((A_KVW, WINDOW), lambda b, j: (0, row(b, f(j))))
    v_spec = lambda f: pl.BlockSpec((WINDOW, A_KVW), lambda b, j: (row(b, f(j)), 0))
    same = lambda j: j
    return pl.pallas_call(
        _attn_a_lat_kernel,
        grid=(NB_LAT, nqb),
        in_specs=[pl.BlockSpec(memory_space=pltpu.SMEM),
                  pl.BlockSpec((WINDOW, A_W), lambda b, j: (row(b, j), 0)),
                  kt_spec(prev), kt_spec(same), kt_spec(nxt),
                  v_spec(prev), v_spec(same), v_spec(nxt),
                  pl.BlockSpec((1, A_KV, HD, PAST), lambda b, j: (b, 0, 0, 0)),
                  pl.BlockSpec((1, PAST, A_KVW), lambda b, j: (b, 0, 0))],
        out_specs=pl.BlockSpec((WINDOW, A_W), lambda b, j: (b * nqb + j, 0)),
        out_shape=jax.ShapeDtypeStruct((N_LAT_TOK, A_W), F32),
        compiler_params=_cparams(2),
        name="attn_a_lat",
    )(sink, aq, akt, akt, akt, av, av, av, ktx, vx)


B_TQ = 256


def _attn_b_body(lam_init, q, lam_ref, g_ref, segs):
    lp = lam_ref[...]
    lam = (jnp.exp(jnp.sum(lp[0:1, :] * lp[1:2, :], axis=1, keepdims=True))
           - jnp.exp(jnp.sum(lp[2:3, :] * lp[3:4, :], axis=1, keepdims=True)) + lam_init)
    outs = []
    for h in range(B_HEADS):
        maps = []
        for mi in range(2):
            c0 = h * B_DV + mi * B_DQ
            qm = (q[:, c0:c0 + B_DQ] * (B_DQ ** -0.5)).astype(BF16)
            ss = [jnp.dot(qm, kt(2 * h + mi), preferred_element_type=F32) for kt, _ in segs]
            m = None
            for s in ss:
                sm = jnp.max(s, axis=-1, keepdims=True)
                m = sm if m is None else jnp.maximum(m, sm)
            l = None
            o = None
            for s, (_, v) in zip(ss, segs):
                p = jnp.exp(s - m)
                ps = jnp.sum(p, axis=-1, keepdims=True)
                l = ps if l is None else l + ps
                pv = jnp.dot(p.astype(BF16), v(), preferred_element_type=F32)[:, h * B_DV:(h + 1) * B_DV]
                o = pv if o is None else o + pv
            maps.append(o / l)
        a = maps[0] - lam * maps[1]
        ms = jnp.mean(a * a, axis=-1, keepdims=True)
        outs.append(a * lax.rsqrt(ms + NORM_EPS) * g_ref[...] * (1.0 - lam_init))
    return outs


def _own_keys(kt_ref, v_ref):
    return (lambda hm: kt_ref[hm * B_DQ:(hm + 1) * B_DQ, :], lambda: v_ref[...])


def _attn_b_ctx_kernel(lam_init, lam_ref, g_ref, q_ref, kt_ref, v_ref, o_ref):
    outs = _attn_b_body(lam_init, q_ref[...], lam_ref, g_ref, [_own_keys(kt_ref, v_ref)])
    for h in range(B_HEADS):
        o_ref[:, h * B_DV:(h + 1) * B_DV] = outs[h]


def _attn_b_lat_kernel(lam_init, lam_ref, g_ref, q_ref, kt_ref, v_ref, ktx_ref, vx_ref, o_ref):
    outs = _attn_b_body(lam_init, q_ref[...], lam_ref, g_ref,
                        [_own_keys(kt_ref, v_ref), (lambda hm: ktx_ref[0, hm], lambda: vx_ref[0])])
    for h in range(B_HEADS):
        o_ref[:, h * B_DV:(h + 1) * B_DV] = outs[h]


def _attn_b_ctx_call(lam_init, lam_p, g, bq, bkt, bvb):
    return pl.pallas_call(
        functools.partial(_attn_b_ctx_kernel, lam_init),
        grid=(NB_CTX,),
        in_specs=[pl.BlockSpec((4, B_DQ), lambda b: (0, 0)),
                  pl.BlockSpec((1, B_DV), lambda b: (0, 0)),
                  pl.BlockSpec((T_CTX, B_W), lambda b: (b, 0)),
                  pl.BlockSpec((B_W, T_CTX), lambda b: (0, b)),
                  pl.BlockSpec((T_CTX, B_W), lambda b: (b, 0))],
        out_specs=pl.BlockSpec((T_CTX, B_W), lambda b: (b, 0)),
        out_shape=jax.ShapeDtypeStruct((N_CTX_TOK, B_W), F32),
        compiler_params=_cparams(1),
        name="attn_b_ctx",
    )(lam_p, g, bq, bkt, bvb)


def _attn_b_lat_call(lam_init, lam_p, g, bq, bkt, bvb, ktx, vx):
    nq = T_LAT // B_TQ
    ctx_blocks = N_CTX_TOK // B_TQ
    ctx_seqs = N_CTX_TOK // T_LAT
    return pl.pallas_call(
        functools.partial(_attn_b_lat_kernel, lam_init),
        grid=(NB_LAT, nq),
        in_specs=[pl.BlockSpec((4, B_DQ), lambda b, j: (0, 0)),
                  pl.BlockSpec((1, B_DV), lambda b, j: (0, 0)),
                  pl.BlockSpec((B_TQ, B_W), lambda b, j: (ctx_blocks + b * nq + j, 0)),
                  pl.BlockSpec((B_W, T_LAT), lambda b, j: (0, ctx_seqs + b)),
                  pl.BlockSpec((T_LAT, B_W), lambda b, j: (ctx_seqs + b, 0)),
                  pl.BlockSpec((1, 2 * B_HEADS, B_DQ, PAST), lambda b, j: (b, 0, 0, 0)),
                  pl.BlockSpec((1, PAST, B_W), lambda b, j: (b, 0, 0))],
        out_specs=pl.BlockSpec((B_TQ, B_W), lambda b, j: (b * nq + j, 0)),
        out_shape=jax.ShapeDtypeStruct((N_LAT_TOK, B_W), F32),
        compiler_params=_cparams(2),
        name="attn_b_lat",
    )(lam_p, g, bq, bkt, bvb, ktx, vx)


HALO = 8


def _head_sum(x, ones_bf16):
    hi = x.astype(BF16)
    lo = (x - hi.astype(F32)).astype(BF16)
    return (jnp.dot(hi, ones_bf16, preferred_element_type=F32)
            + jnp.dot(lo, ones_bf16, preferred_element_type=F32))


def _rwkv_prep_kernel(rkv_ref, prev_ref, next_ref, cw_ref, ca_ref,
                      conv_ref, w0_ref, w2_ref, a0_ref, a2_ref, kk_ref, ka_ref, rk_ref, ones_ref,
                      r_ref, nkk_ref, v_ref, w_f, b_f, kd_f, w_b, b_b, kd_b, bonus_ref):
    i = pl.program_id(0)
    li = _lat_tile(i)
    is_ctx = i < N_CTX_TILES
    has_prev = jnp.logical_and(jnp.logical_not(is_ctx), li > 0).astype(F32)
    has_next = jnp.logical_and(jnp.logical_not(is_ctx), li < LAT_TILES_PER_SEQ - 1).astype(F32)
    x = rkv_ref[...]
    row = lax.broadcasted_iota(jnp.int32, x.shape, 0)
    xm = jnp.where(row == 0, prev_ref[HALO - 1:HALO, :] * has_prev, pltpu.roll(x, 1, axis=0))
    xp = jnp.where(row == TM - 1, next_ref[0:1, :] * has_next, pltpu.roll(x, TM - 1, axis=0))
    y = xm * conv_ref[0:1, :] + x * conv_ref[1:2, :] + xp * conv_ref[2:3, :]
    r, k, v = y[:, :C_W], y[:, C_W:2 * C_W], y[:, 2 * C_W:]
    ones = ones_ref[...]

    kk = k * kk_ref[...]
    kk = kk / jnp.maximum(jnp.sqrt(_head_sum(kk * kk, ones)), 1e-12)
    lw = jnp.dot(jnp.tanh(cw_ref[...]).astype(BF16), w2_ref[...],
                 preferred_element_type=F32)
    la = jnp.dot(ca_ref[...].astype(BF16), a2_ref[...], preferred_element_type=F32)
    r_ref[...] = r
    nkk_ref[...] = -kk
    v_ref[...] = v
    bonus = jnp.zeros_like(v)
    for d, (w_o, b_o, kd_o) in enumerate(((w_f, b_f, kd_f), (w_b, b_b, kd_b))):
        sl = slice(d * C_W, (d + 1) * C_W)
        a = _sigmoid(a0_ref[d:d + 1, :] + la[:, sl])
        kd = k * (1.0 + (a - 1.0) * ka_ref[...])
        w_o[...] = -DECAY_SCALE * _sigmoid(w0_ref[d:d + 1, :] + lw[:, sl])
        b_o[...] = kk * a
        kd_o[...] = kd
        bonus = bonus + _head_sum(r * kd * rk_ref[...], ones) * v
    bonus_ref[...] = bonus


def _rwkv_prep_call(rkv, cw, ca, p):
    nh = TM // HALO
    last = N_TOK // HALO - 1
    full = lambda shape: pl.BlockSpec(shape, lambda i: (0,) * len(shape))
    tile = lambda w: pl.BlockSpec((TM, w), lambda i: (i, 0))
    return pl.pallas_call(
        _rwkv_prep_kernel,
        grid=(N_TILES,),
        in_specs=[tile(3 * C_W),
                  pl.BlockSpec((HALO, 3 * C_W), lambda i: (jnp.maximum(i * nh - 1, 0), 0)),
                  pl.BlockSpec((HALO, 3 * C_W), lambda i: (jnp.minimum((i + 1) * nh, last), 0)),
                  tile(2 * C_RANK), tile(2 * C_RANK),
                  full((3, 3 * C_W)), full((2, C_W)), full((2 * C_RANK, 2 * C_W)),
                  full((2, C_W)), full((2 * C_RANK, 2 * C_W)),
                  full((1, C_W)), full((1, C_W)), full((1, C_W)), full((C_W, C_W))],
        out_specs=[tile(C_W)] * 10,
        out_shape=[jax.ShapeDtypeStruct((N_TOK, C_W), F32)] * 10,
        compiler_params=_cparams(1),
        name="rwkv_prep",
    )(rkv, rkv, rkv, cw, ca, p["conv"], p["w0"], p["w2"], p["a0"], p["a2"],
      p["kk"], p["ka"], p["rk"], p["ones"])


N_BLK = TM // LANES
N_PAIR = C_HEADS // 2


def _scan_kernel(r_f, nkk_f, lw_f, b_f, e_f, r_b, nkk_b, lw_b, b_b, e_b,
                 s0f_ref, s0b_ref, ef_ref, eb_ref,
                 yf_ref, yb_ref, sf_ref, sb_ref,
                 st, rows, et, rp, yp):
    first, last = _seq_edges(pl.program_id(0))
    lane = lax.broadcasted_iota(jnp.int32, (TM, LANES), 1)

    @pl.when(first)
    def _():
        st[...] = jnp.zeros_like(st)
        for d, s0 in enumerate((s0f_ref, s0b_ref)):
            for h in range(C_HEADS):
                o = (h % 2) * C_N
                st[d, h, :, o:o + C_N] = s0[0, h]

    for d, srcs in enumerate(((nkk_f, lw_f, b_f), (nkk_b, lw_b, b_b))):
        for a, src in enumerate(srcs):
            for pr in range(N_PAIR):
                chunk = src[:, pr * LANES:(pr + 1) * LANES]
                if a == 1:
                    chunk = jnp.exp(chunk)
                rows[d, a, 2 * pr] = jnp.where(lane < C_N, chunk, 0.0)
                rows[d, a, 2 * pr + 1] = jnp.where(lane < C_N, 0.0, chunk)
    for d, src in enumerate((e_f, e_b)):
        for pr in range(N_PAIR):
            for blk in range(N_BLK):
                tr = src[blk * LANES:(blk + 1) * LANES, pr * LANES:(pr + 1) * LANES].T
                for half in range(2):
                    tile = tr[half * C_N:(half + 1) * C_N, :]
                    swapped = pltpu.roll(tile, C_N, axis=1)
                    for q in range(2):
                        et[d, 2 * pr + half, 2 * blk + q] = tile if q != half else swapped

    for d, src in enumerate((r_f, r_b)):
        for pr in range(N_PAIR):
            rp[d, pr] = src[:, pr * LANES:(pr + 1) * LANES]

    lane1 = lax.broadcasted_iota(jnp.int32, (1, LANES), 1)

    def emit_y(d, t):
        for pr in range(N_PAIR):
            both = jnp.concatenate([st[d, 2 * pr].astype(BF16), st[d, 2 * pr + 1].astype(BF16)], axis=0)
            r8 = jnp.broadcast_to(rp[d, pr, pl.ds(t, 1), :], (8, LANES)).astype(BF16)
            y8 = lax.dot_general(r8, both, (((1,), (1,)), ((), ())), preferred_element_type=F32)
            yp[d, pr, pl.ds(t, 1), :] = y8[0:1, :]

    def step(i, carry):
        for d in range(2):
            t = i if d == 0 else TM - 1 - i
            emit_y(d, jnp.maximum(i - 1, 0) if d == 0 else jnp.minimum(TM - i, TM - 1))
            blk = t // C_N
            pick = (lane1 == (t % C_N) + C_N, lane1 == (t % C_N))
            for h in range(C_HEADS):
                s = st[d, h]
                nkk = rows[d, 0, h, pl.ds(t, 1), :]
                w = rows[d, 1, h, pl.ds(t, 1), :]
                b = rows[d, 2, h, pl.ds(t, 1), :]
                sa = jnp.sum(jnp.where(pick[h % 2], et[d, h, blk], s * nkk), axis=1, keepdims=True)
                st[d, h] = s * w + sa * b
        return carry

    lax.fori_loop(0, TM, step, 0, unroll=16)
    emit_y(0, TM - 1)
    emit_y(1, 0)

    for d, y_ref in enumerate((yf_ref, yb_ref)):
        for pr in range(N_PAIR):
            y_ref[:, pr * LANES:(pr + 1) * LANES] = yp[d, pr]

    @pl.when(last)
    def _():
        for d, (s_out, e_fin) in enumerate(((sf_ref, ef_ref), (sb_ref, eb_ref))):
            for h in range(C_HEADS):
                o = (h % 2) * C_N
                s_out[0, h] = st[d, h, :, o:o + C_N] + e_fin[0, h]


N_SEQ = NB_CTX + NB_LAT


def _lat_seq(i):
    return jnp.maximum(i - N_CTX_TILES, 0) // LAT_TILES_PER_SEQ


def _seq_of_tile(i):
    return jnp.where(i < N_CTX_TILES, i, NB_CTX + _lat_seq(i))


def _bwd_tile(i):
    rev = N_CTX_TILES + _lat_seq(i) * LAT_TILES_PER_SEQ + (LAT_TILES_PER_SEQ - 1 - _lat_tile(i))
    return jnp.where(i < N_CTX_TILES, i, rev)


def _seq_edges(i):
    is_ctx = i < N_CTX_TILES
    return (jnp.logical_or(is_ctx, _lat_tile(i) == 0),
            jnp.logical_or(is_ctx, _lat_tile(i) == LAT_TILES_PER_SEQ - 1))


_FWD_ROWS = pl.BlockSpec((TM, C_W), lambda i: (i, 0))
_BWD_ROWS = pl.BlockSpec((TM, C_W), lambda i: (_bwd_tile(i), 0))
_SEQ_STATE = pl.BlockSpec((1, C_HEADS, C_N, C_N), lambda i: (_seq_of_tile(i), 0, 0, 0))


def _scan_call(ins, s0f, s0b, e_fin_f, e_fin_b):
    r, nkk, lw_f, b_f, e_f, lw_b, b_b, e_b = ins
    return pl.pallas_call(
        _scan_kernel,
        grid=(N_TILES,),
        in_specs=[_FWD_ROWS] * 5 + [_BWD_ROWS] * 5 + [_SEQ_STATE] * 4,
        out_specs=[_FWD_ROWS, _BWD_ROWS, _SEQ_STATE, _SEQ_STATE],
        out_shape=[jax.ShapeDtypeStruct((N_TOK, C_W), F32)] * 2
                  + [jax.ShapeDtypeStruct((N_SEQ, C_HEADS, C_N, C_N), F32)] * 2,
        scratch_shapes=[pltpu.VMEM((2, C_HEADS, C_N, LANES), F32),
                        pltpu.VMEM((2, 3, C_HEADS, TM, LANES), F32),
                        pltpu.VMEM((2, C_HEADS, TM // C_N, C_N, LANES), F32),
                        pltpu.VMEM((2, N_PAIR, TM, LANES), F32),
                        pltpu.VMEM((2, N_PAIR, TM, LANES), F32)],
        compiler_params=_cparams(1),
        name="scan",
    )(r, nkk, lw_f, b_f, e_f, r, nkk, lw_b, b_b, e_b, s0f, s0b, e_fin_f, e_fin_b)


SUB = C_N


def _cum_rows(tri_bf16, x):
    hi = x.astype(BF16)
    lo = (x - hi.astype(F32)).astype(BF16)
    return (jnp.dot(tri_bf16, hi, preferred_element_type=F32)
            + jnp.dot(tri_bf16, lo, preferred_element_type=F32))


_NT = (((1,), (1,)), ((), ()))
_TN = (((0,), (0,)), ((), ()))


def _decay_kernel(r_f, nkk_f, v_f, lw_f, kd_f, r_b, nkk_b, v_b, lw_b, kd_b,
                  ef_ref, yef_ref, eb_ref, yeb_ref, stf_ref, stb_ref, est):
    first, last = _seq_edges(pl.program_id(0))

    @pl.when(first)
    def _():
        est[...] = jnp.zeros_like(est)

    ti = lax.broadcasted_iota(jnp.int32, (SUB, SUB), 0)
    si = lax.broadcasted_iota(jnp.int32, (SUB, SUB), 1)
    dirs = ((r_f, nkk_f, v_f, lw_f, kd_f, ef_ref, yef_ref), (r_b, nkk_b, v_b, lw_b, kd_b, eb_ref, yeb_ref))
    for d, (r_ref, nkk_ref, v_ref, lw_ref, kd_ref, e_out, ye_out) in enumerate(dirs):
        upto = (si <= ti) if d == 0 else (si >= ti)
        before = (si < ti) if d == 0 else (si > ti)
        both_masks = jnp.concatenate([before, upto], axis=0)
        tri = upto.astype(BF16)
        end_row = SUB - 1 if d == 0 else 0
        chunks = range(TM // SUB) if d == 0 else range(TM // SUB - 1, -1, -1)
        for c in chunks:
            rs = slice(c * SUB, (c + 1) * SUB)
            lw = lw_ref[rs, :]
            lg = _cum_rows(tri, lw)
            abar = nkk_ref[rs, :] * jnp.exp(lg - lw)
            rbar = r_ref[rs, :] * jnp.exp(lg)
            kbar = kd_ref[rs, :] * jnp.exp(-lg)
            g_end = jnp.exp(lg[end_row:end_row + 1, :])
            v = v_ref[rs, :]
            for h in range(C_HEADS):
                hs = slice(h * C_N, (h + 1) * C_N)
                lhs = jnp.concatenate([abar[:, hs], rbar[:, hs]], axis=0).astype(BF16)
                kh = kbar[:, hs].astype(BF16)
                vh = v[:, hs].astype(BF16)
                eh = est[d, h]
                sc = lax.dot_general(lhs, kh, _NT, preferred_element_type=F32)
                sc = jnp.where(both_masks, sc, 0.0).astype(BF16)
                out = (jnp.dot(sc, vh, preferred_element_type=F32)
                       + lax.dot_general(lhs, eh.astype(BF16), _NT, preferred_element_type=F32))
                e_out[rs, hs] = out[0:SUB, :]
                ye_out[rs, hs] = out[SUB:2 * SUB, :]
                est[d, h] = (eh + lax.dot_general(vh, kh, _TN, preferred_element_type=F32)) * g_end[:, hs]

    @pl.when(last)
    def _():
        stf_ref[0] = est[0]
        stb_ref[0] = est[1]


def _decay_call(ins):
    r, nkk, v, lw_f, kd_f, lw_b, kd_b = ins
    return pl.pallas_call(
        _decay_kernel,
        grid=(N_TILES,),
        in_specs=[_FWD_ROWS] * 5 + [_BWD_ROWS] * 5,
        out_specs=[_FWD_ROWS, _FWD_ROWS, _BWD_ROWS, _BWD_ROWS, _SEQ_STATE, _SEQ_STATE],
        out_shape=[jax.ShapeDtypeStruct((N_TOK, C_W), F32)] * 4
                  + [jax.ShapeDtypeStruct((N_SEQ, C_HEADS, C_N, C_N), F32)] * 2,
        scratch_shapes=[pltpu.VMEM((2, C_HEADS, C_N, C_N), F32)],
        compiler_params=_cparams(1),
        name="decay",
    )(r, nkk, v, lw_f, kd_f, r, nkk, v, lw_b, kd_b)


def _outproj_kernel(x_ref, mod_ref, aoc_ref, aol_ref, boc_ref, bol_ref,
                    yf_ref, yb_ref, yef_ref, yeb_ref, bonus_ref, cg_ref,
                    lng_ref, lnb_ref, g2_ref, mean_ref, wa_ref, wb_ref, wc_ref, o_ref):
    is_ctx = pl.program_id(0) < N_CTX_TILES
    ao = jnp.where(is_ctx, aoc_ref[...], aol_ref[...])
    bo = jnp.where(is_ctx, boc_ref[...], bol_ref[...])
    y = (yf_ref[...] + yef_ref[...]) + (yb_ref[...] + yeb_ref[...]) + bonus_ref[...]
    ones = mean_ref[...]
    mu = _head_sum(y, ones) * (1.0 / C_N)
    dy = y - mu
    var = _head_sum(dy * dy, ones) * (1.0 / C_N)
    yn = dy * lax.rsqrt(var + GN_EPS) * lng_ref[...] + lnb_ref[...]
    gate = jnp.dot(_sigmoid(cg_ref[...]).astype(BF16), g2_ref[...], preferred_element_type=F32)
    co = yn * gate
    m = (jnp.dot(ao.astype(BF16), wa_ref[...], preferred_element_type=F32)
         + jnp.dot(bo.astype(BF16), wb_ref[...], preferred_element_type=F32)
         + jnp.dot(co.astype(BF16), wc_ref[...], preferred_element_type=F32))
    o_ref[...] = x_ref[...] + mod_ref[0, 2:3, :] * m


def _outproj_call(x, mod_l, ao_ctx, ao_lat, bo_ctx, bo_lat, yf, yb, yef, yeb, bonus, cg, p):
    full = lambda shape: pl.BlockSpec(shape, lambda i: (0,) * len(shape))
    tile = lambda w: pl.BlockSpec((TM, w), lambda i: (i, 0))
    ctx_tile = lambda w: pl.BlockSpec((TM, w), lambda i: (jnp.minimum(i, N_CTX_TILES - 1), 0))
    lat_tile = lambda w: pl.BlockSpec((TM, w), lambda i: (jnp.maximum(i - N_CTX_TILES, 0), 0))
    return pl.pallas_call(
        _outproj_kernel,
        grid=(N_TILES,),
        in_specs=[tile(D_MODEL),
                  pl.BlockSpec((1, 8, D_MODEL), lambda i: (_tile_group(i), 0, 0)),
                  ctx_tile(A_W), lat_tile(A_W), ctx_tile(B_W), lat_tile(B_W),
                  tile(C_W), tile(C_W), tile(C_W), tile(C_W), tile(C_W),
                  tile(C_G_RANK),
                  full((1, C_W)), full((1, C_W)), full((C_G_RANK, C_W)), full((C_W, C_W)),
                  full((A_W, D_MODEL)), full((B_W, D_MODEL)), full((C_W, D_MODEL))],
        out_specs=tile(D_MODEL),
        out_shape=jax.ShapeDtypeStruct((N_TOK, D_MODEL), F32),
        compiler_params=_cparams(1),
        name="outproj",
    )(x, mod_l, ao_ctx, ao_lat, bo_ctx, bo_lat, yf, yb, yef, yeb, bonus, cg,
      p["lnx_g"], p["lnx_b"], p["g2"], p["mean"],
      p["wa"], p["wb"], p["wc"])


def _ffn_kernel(final, x_ref, mod_ref, g_ref, w1_ref, w3_ref, w2_ref, fg_ref, o_ref):
    x = x_ref[...]
    h = _modulated_norm(x, g_ref[...], mod_ref[0, 3:4, :], mod_ref[0, 4:5, :]).astype(BF16)
    u = jnp.dot(h, w1_ref[...], preferred_element_type=F32)
    t = jnp.dot(h, w3_ref[...], preferred_element_type=F32)
    act = (u * _sigmoid(u) * t).astype(BF16)
    y = x + mod_ref[0, 5:6, :] * jnp.dot(act, w2_ref[...], preferred_element_type=F32)
    if final:
        ms = jnp.mean(y * y, axis=-1, keepdims=True)
        y = y * lax.rsqrt(ms + NORM_EPS) * fg_ref[...]
    o_ref[...] = y


def _ffn_call(final, x, mod_l, g, w1, w3, w2, final_g):
    once = lambda shape: pl.BlockSpec(shape, lambda i: (0,) * len(shape),
                                      pipeline_mode=pl.Buffered(1))
    return pl.pallas_call(
        functools.partial(_ffn_kernel, final),
        grid=(N_TILES,),
        in_specs=[pl.BlockSpec((TM, D_MODEL), lambda i: (i, 0)),
                  pl.BlockSpec((1, 8, D_MODEL), lambda i: (_tile_group(i), 0, 0)),
                  once((1, D_MODEL)),
                  once((D_MODEL, D_FF)), once((D_MODEL, D_FF)), once((D_FF, D_MODEL)),
                  once((1, D_MODEL))],
        out_specs=pl.BlockSpec((TM, D_MODEL), lambda i: (i, 0)),
        out_shape=jax.ShapeDtypeStruct((N_TOK, D_MODEL), F32),
        compiler_params=_cparams(1),
        name="ffn",
    )(x, mod_l, g, w1, w3, w2, final_g)


def _block_diag2(m):
    z = jnp.zeros_like(m[0])
    return jnp.concatenate([jnp.concatenate([m[0], z], axis=1),
                            jnp.concatenate([z, m[1]], axis=1)], axis=0)


def _keys_t(k, nb, t, heads, dim):
    return k.reshape(nb, t, heads, dim).transpose(0, 2, 3, 1).astype(BF16)


def kernel(x_prompt, x_sample, cache_a_k, cache_a_v, cache_b_k, cache_b_v, state_c_fwd, state_c_bwd,
           c, c_ctx, ada_w, ada_b, norm1_g, norm2_g, w_in, a_sink, b_lambda, b_subln_g,
           c_conv, c_w0, c_w2, c_a0, c_a2, c_g2, c_kk, c_ka, c_rk, c_lnx_g, c_lnx_b,
           w_out, ffn_w1, ffn_w3, ffn_w2, final_g):
    x = jnp.concatenate([x_prompt.reshape(N_CTX_TOK, D_MODEL),
                         x_sample.reshape(N_LAT_TOK, D_MODEL)], axis=0)
    cvec8 = jnp.concatenate([c_ctx[None], c, jnp.zeros((8 - 1 - NB_LAT, D_MODEL), F32)], axis=0)
    mod = _ada_call(cvec8, ada_w, ada_b)
    mod = mod[:, :1 + NB_LAT].reshape(DEPTH, 1 + NB_LAT, 6, D_MODEL)
    mod = jnp.pad(mod, ((0, 0), (0, 0), (0, 2), (0, 0)))

    tabs = _rope_tables()
    head_id = np.arange(C_W) // C_N
    block_ones = jnp.asarray(head_id[:, None] == head_id[None, :], BF16)
    final_g2 = final_g.reshape(1, D_MODEL)
    s0_ctx = jnp.zeros((NB_CTX, C_HEADS, C_N, C_N), F32)

    new_ak, new_av, new_bk, new_bv, new_sf, new_sb = [], [], [], [], [], []
    for l in range(DEPTH):
        lam_init = 0.8 - 0.6 * math.exp(-0.3 * l)
        aq, ak, av, bq, bk, bv, rkv, cw, ca, cg, akt, bkt, bvb = _inproj_call(
            x, mod[l], norm1_g[l].reshape(1, D_MODEL), w_in[l].astype(BF16), tabs)

        new_ak.append(ak[:N_CTX_TOK].reshape(NB_CTX, T_CTX, A_KV, HD))
        new_av.append(av[:N_CTX_TOK].reshape(NB_CTX, T_CTX, A_KV, HD))
        new_bk.append(bk[:N_CTX_TOK].reshape(NB_CTX, T_CTX, B_HEADS, 2, B_DQ))
        new_bv.append(bv[:N_CTX_TOK].reshape(NB_CTX, T_CTX, B_HEADS, B_DV))

        sink = a_sink[l]
        ao_ctx = _attn_a_ctx_call(sink, aq, akt, av)
        ao_lat = _attn_a_lat_call(
            sink, aq, akt, av,
            _keys_t(cache_a_k[:, l].reshape(NB_LAT * PAST, A_KVW), NB_LAT, PAST, A_KV, HD),
            cache_a_v[:, l].reshape(NB_LAT, PAST, A_KVW))

        lam_p, sub_g = b_lambda[l], b_subln_g[l].reshape(1, B_DV)
        bo_ctx = _attn_b_ctx_call(lam_init, lam_p, sub_g, bq, bkt, bvb)
        bo_lat = _attn_b_lat_call(
            lam_init, lam_p, sub_g, bq, bkt, bvb,
            _keys_t(cache_b_k[:, l].reshape(NB_LAT * PAST, B_W), NB_LAT, PAST, 2 * B_HEADS, B_DQ),
            cache_b_v[:, l].reshape(NB_LAT, PAST, B_W).astype(BF16))

        prep = _rwkv_prep_call(rkv, cw, ca, dict(
            conv=c_conv[l], w0=c_w0[l], w2=_block_diag2(c_w2[l]).astype(BF16),
            a0=c_a0[l], a2=_block_diag2(c_a2[l]).astype(BF16),
            kk=c_kk[l].reshape(1, C_W), ka=c_ka[l].reshape(1, C_W), rk=c_rk[l].reshape(1, C_W),
            ones=block_ones))
        r, nkk, v, lw_f, b_f, kd_f, lw_b, b_b, kd_b, bonus = prep
        e_f, yef, e_b, yeb, e_fin_f, e_fin_b = _decay_call((r, nkk, v, lw_f, kd_f, lw_b, kd_b))
        yf, yb, sf, sb = _scan_call(
            (r, nkk, lw_f, b_f, e_f, lw_b, b_b, e_b),
            jnp.concatenate([s0_ctx, state_c_fwd[:, l]], axis=0),
            jnp.concatenate([s0_ctx, state_c_bwd[:, l]], axis=0), e_fin_f, e_fin_b)
        new_sf.append(sf[:NB_CTX])
        new_sb.append(sb[:NB_CTX])

        wo = w_out[l].astype(BF16)
        x = _outproj_call(x, mod[l], ao_ctx, ao_lat, bo_ctx, bo_lat, yf, yb, yef, yeb, bonus, cg, dict(
            lnx_g=c_lnx_g[l].reshape(1, C_W), lnx_b=c_lnx_b[l].reshape(1, C_W),
            g2=c_g2[l].astype(BF16), mean=block_ones,
            wa=wo[:A_W], wb=wo[A_W:A_W + B_W], wc=wo[A_W + B_W:]))

        x = _ffn_call(l == DEPTH - 1, x, mod[l], norm2_g[l].reshape(1, D_MODEL),
                      ffn_w1[l].astype(BF16), ffn_w3[l].astype(BF16), ffn_w2[l].astype(BF16), final_g2)

    y_prompt = x[:N_CTX_TOK].reshape(NB_CTX, T_CTX, D_MODEL)
    y_sample = x[N_CTX_TOK:].reshape(NB_LAT, T_LAT, D_MODEL)
    return (y_prompt, y_sample,
            jnp.stack(new_ak, axis=1), jnp.stack(new_av, axis=1),
            jnp.stack(new_bk, axis=1), jnp.stack(new_bv, axis=1),
            jnp.stack(new_sf, axis=1), jnp.stack(new_sb, axis=1))
```

```python
import functools
import math

import numpy as np
import jax
import jax.numpy as jnp
from jax import lax
from jax.experimental import pallas as pl
from jax.experimental.pallas import tpu as pltpu

F32 = jnp.float32
BF16 = jnp.bfloat16

D_MODEL = 1024
DEPTH = 4
NB_CTX, T_CTX = 16, 256
NB_LAT, T_LAT = 2, 4096
PAST = 512
GRID_W = 64
HD = 64
A_HEADS, A_KV = 6, 2
A_W = A_HEADS * HD
A_KVW = A_KV * HD
WINDOW = 128
B_HEADS, B_DQ, B_DV = 4, 32, 64
B_W = B_HEADS * B_DV
C_HEADS, C_N = 6, 64
C_W = C_HEADS * C_N
C_RANK = 64
C_G_RANK = 128
D_FF = 2816
IN_COLS = 2944
ROPE_THETA = 10000.0
NORM_EPS = 1e-6
GN_EPS = 64e-5
DECAY_SCALE = 0.606531
NEG_INF = -1e30

TM = 256
N_CTX_TOK = NB_CTX * T_CTX
N_LAT_TOK = NB_LAT * T_LAT
N_TOK = N_CTX_TOK + N_LAT_TOK
N_CTX_TILES = N_CTX_TOK // TM
N_TILES = N_TOK // TM
LAT_TILES_PER_SEQ = T_LAT // TM
LANES = 128
VMEM_LIMIT = 56 * 1024 * 1024

O_AQ, O_AK, O_AV, O_BQ, O_BK, O_BV, O_RKV, O_CW, O_CA, O_CG = (
    0, 384, 512, 640, 896, 1152, 1408, 2560, 2688, 2816)


def _cparams(n_grid):
    return pltpu.CompilerParams(dimension_semantics=("arbitrary",) * n_grid,
                                vmem_limit_bytes=VMEM_LIMIT)


def _sigmoid(x):
    return 1.0 / (1.0 + jnp.exp(-x))


def _tile_group(i):
    return jnp.where(i < N_CTX_TILES, 0, 1 + (i - N_CTX_TILES) // LAT_TILES_PER_SEQ)


def _lat_tile(i):
    return jnp.maximum(i - N_CTX_TILES, 0) % LAT_TILES_PER_SEQ


ADA_TN = 1536


def _ada_kernel(c_ref, w_ref, b_ref, o_ref):
    c = c_ref[...]
    s = c * _sigmoid(c)
    o_ref[0] = jnp.dot(s.astype(BF16), w_ref[0].astype(BF16),
                       preferred_element_type=F32) + b_ref[0]


def _ada_call(cvec8, ada_w, ada_b):
    n = 6 * D_MODEL
    return pl.pallas_call(
        _ada_kernel,
        grid=(DEPTH, n // ADA_TN),
        in_specs=[pl.BlockSpec((8, D_MODEL), lambda l, j: (0, 0)),
                  pl.BlockSpec((1, D_MODEL, ADA_TN), lambda l, j: (l, 0, j)),
                  pl.BlockSpec((1, 1, ADA_TN), lambda l, j: (l, 0, j))],
        out_specs=pl.BlockSpec((1, 8, ADA_TN), lambda l, j: (l, 0, j)),
        out_shape=jax.ShapeDtypeStruct((DEPTH, 8, n), F32),
        compiler_params=_cparams(2),
        name="ada",
    )(cvec8, ada_w, ada_b.reshape(DEPTH, 1, n))


def _modulated_norm(x, g, shift, scale):
    ms = jnp.mean(x * x, axis=-1, keepdims=True)
    h = x * lax.rsqrt(ms + NORM_EPS) * g
    return h * (1.0 + scale) + shift


def _rope_chunk(x, cos, sin_lo, sin_hi, half):
    up = pltpu.roll(x, LANES - half, axis=1)
    dn = pltpu.roll(x, half, axis=1)
    return x * cos + up * sin_lo + dn * sin_hi


def _inproj_kernel(x_ref, mod_ref, g_ref, w_ref,
                   cos_a, sl_a, sh_a, cos_b, sl_b, sh_b,
                   aq_ref, ak_ref, av_ref, bq_ref, bk_ref, bv_ref,
                   rkv_ref, cw_ref, ca_ref, cg_ref, akt_ref, bkt_ref, bvb_ref):
    i = pl.program_id(0)
    h = _modulated_norm(x_ref[...], g_ref[...], mod_ref[0, 0:1, :], mod_ref[0, 1:2, :])
    z = jnp.dot(h.astype(BF16), w_ref[...], preferred_element_type=F32)
    av_ref[...] = z[:, O_AV:O_BQ]
    bv_ref[...] = z[:, O_BV:O_RKV]
    bvb_ref[...] = z[:, O_BV:O_RKV].astype(BF16)
    rkv_ref[...] = z[:, O_RKV:O_CW]
    cw_ref[...] = z[:, O_CW:O_CA]
    ca_ref[...] = z[:, O_CA:O_CG]
    cg_ref[...] = z[:, O_CG:IN_COLS]

    def emit_keys(ak, bk):
        ak_ref[...] = ak
        bk_ref[...] = bk
        akt_ref[...] = ak.T.astype(BF16)
        bkt_ref[...] = bk.T.astype(BF16)

    @pl.when(i < N_CTX_TILES)
    def _():
        aq_ref[...] = z[:, O_AQ:O_AK]
        bq_ref[...] = z[:, O_BQ:O_BK]
        emit_keys(z[:, O_AK:O_AV], z[:, O_BK:O_BV])

    @pl.when(i >= N_CTX_TILES)
    def _():
        ca_, la_, ha_ = cos_a[...], sl_a[...], sh_a[...]
        cb_, lb_, hb_ = cos_b[...], sl_b[...], sh_b[...]
        for j in range(A_W // LANES):
            o = O_AQ + j * LANES
            aq_ref[:, j * LANES:(j + 1) * LANES] = _rope_chunk(z[:, o:o + LANES], ca_, la_, ha_, 16)
        bks = []
        for j in range(B_W // LANES):
            o = O_BQ + j * LANES
            bq_ref[:, j * LANES:(j + 1) * LANES] = _rope_chunk(z[:, o:o + LANES], cb_, lb_, hb_, 8)
            o = O_BK + j * LANES
            bks.append(_rope_chunk(z[:, o:o + LANES], cb_, lb_, hb_, 8))
        emit_keys(_rope_chunk(z[:, O_AK:O_AV], ca_, la_, ha_, 16), jnp.concatenate(bks, axis=1))


def _inproj_call(x, mod_l, g, w_bf16, tabs):
    widths = (A_W, A_KVW, A_KVW, B_W, B_W, B_W, 3 * C_W, 2 * C_RANK, 2 * C_RANK, C_G_RANK)
    tab_spec = pl.BlockSpec((TM, LANES), lambda i: (_lat_tile(i), 0))
    return pl.pallas_call(
        _inproj_kernel,
        grid=(N_TILES,),
        in_specs=[pl.BlockSpec((TM, D_MODEL), lambda i: (i, 0)),
                  pl.BlockSpec((1, 8, D_MODEL), lambda i: (_tile_group(i), 0, 0)),
                  pl.BlockSpec((1, D_MODEL), lambda i: (0, 0)),
                  pl.BlockSpec((D_MODEL, IN_COLS), lambda i: (0, 0))] + [tab_spec] * 6,
        out_specs=[pl.BlockSpec((TM, w), lambda i: (i, 0)) for w in widths]
                  + [pl.BlockSpec((A_KVW, TM), lambda i: (0, i)),
                     pl.BlockSpec((B_W, TM), lambda i: (0, i)),
                     pl.BlockSpec((TM, B_W), lambda i: (i, 0))],
        out_shape=[jax.ShapeDtypeStruct((N_TOK, w), F32) for w in widths]
                  + [jax.ShapeDtypeStruct((A_KVW, N_TOK), BF16),
                     jax.ShapeDtypeStruct((B_W, N_TOK), BF16),
                     jax.ShapeDtypeStruct((N_TOK, B_W), BF16)],
        compiler_params=_cparams(1),
        name="inproj",
    )(x, mod_l, g, w_bf16, *tabs)


def _rope_tables():
    t = np.arange(T_LAT)
    rows, cols = t // GRID_W, t % GRID_W

    def build(width):
        half = width // 4
        d = width // 2
        inv = ROPE_THETA ** (-jnp.arange(0, d, 2, dtype=F32) / d)
        lane = np.arange(LANES) % width
        part = lane // d
        p = lane % d
        f = p % half
        pos = jnp.where(jnp.asarray(part)[None, :] == 0,
                        jnp.asarray(rows, F32)[:, None], jnp.asarray(cols, F32)[:, None])
        ang = pos * inv[jnp.asarray(f)][None, :]
        cos, sin = jnp.cos(ang), jnp.sin(ang)
        lo = jnp.asarray(p < half)[None, :]
        return cos, jnp.where(lo, -sin, 0.0), jnp.where(lo, 0.0, sin)

    return build(HD) + build(B_DQ)


def _attend_a(q, sink_ref, segs):
    outs = []
    for h in range(A_HEADS):
        g = h // (A_HEADS // A_KV)
        qh = (q[:, h * HD:(h + 1) * HD] * (HD ** -0.5)).astype(BF16)
        sink = sink_ref[h]
        ss = []
        m = None
        for kt, _, mask in segs:
            s = jnp.dot(qh, kt(g), preferred_element_type=F32)
            if mask is not None:
                s = jnp.where(mask, s, NEG_INF)
            ss.append(s)
            sm = jnp.max(s, axis=-1, keepdims=True)
            m = sm if m is None else jnp.maximum(m, sm)
        m = jnp.maximum(m, sink)
        l = jnp.exp(sink - m)
        o = None
        for s, (_, v, _) in zip(ss, segs):
            p = jnp.exp(s - m)
            l = l + jnp.sum(p, axis=-1, keepdims=True)
            pv = jnp.dot(p.astype(BF16), v().astype(BF16), preferred_element_type=F32)[:, g * HD:(g + 1) * HD]
            o = pv if o is None else o + pv
        outs.append(o / l)
    return outs


def _attn_a_ctx_kernel(sink_ref, q_ref, kt_ref, v_ref, o_ref):
    outs = _attend_a(q_ref[...], sink_ref,
                     [(lambda g: kt_ref[g * HD:(g + 1) * HD, :], lambda: v_ref[...], None)])
    for h in range(A_HEADS):
        o_ref[:, h * HD:(h + 1) * HD] = outs[h]


def _attn_a_lat_kernel(sink_ref, q_ref, ktp_ref, ktc_ref, ktn_ref, vp_ref, vc_ref, vn_ref,
                       ktx_ref, vx_ref, o_ref):
    qb = pl.program_id(1)
    nqb = pl.num_programs(1)
    qi = lax.broadcasted_iota(jnp.int32, (WINDOW, WINDOW), 0)
    kj = lax.broadcasted_iota(jnp.int32, (WINDOW, WINDOW), 1)
    mask_prev = (kj >= qi) & (qb > 0)
    mask_next = (kj <= qi) & (qb < nqb - 1)
    head_rows = lambda ref: (lambda g: ref[g * HD:(g + 1) * HD, :])
    segs = [(head_rows(ktp_ref), lambda: vp_ref[...], mask_prev),
            (head_rows(ktc_ref), lambda: vc_ref[...], None),
            (head_rows(ktn_ref), lambda: vn_ref[...], mask_next),
            (lambda g: ktx_ref[0, g], lambda: vx_ref[0], None)]
    outs = _attend_a(q_ref[...], sink_ref, segs)
    for h in range(A_HEADS):
        o_ref[:, h * HD:(h + 1) * HD] = outs[h]


def _attn_a_ctx_call(sink, aq, akt, av):
    return pl.pallas_call(
        _attn_a_ctx_kernel,
        grid=(NB_CTX,),
        in_specs=[pl.BlockSpec(memory_space=pltpu.SMEM),
                  pl.BlockSpec((T_CTX, A_W), lambda b: (b, 0)),
                  pl.BlockSpec((A_KVW, T_CTX), lambda b: (0, b)),
                  pl.BlockSpec((T_CTX, A_KVW), lambda b: (b, 0))],
        out_specs=pl.BlockSpec((T_CTX, A_W), lambda b: (b, 0)),
        out_shape=jax.ShapeDtypeStruct((N_CTX_TOK, A_W), F32),
        compiler_params=_cparams(1),
        name="attn_a_ctx",
    )(sink, aq, akt, av)


def _attn_a_lat_call(sink, aq, akt, av, ktx, vx):
    nqb = T_LAT // WINDOW
    ctx_blocks = N_CTX_TOK // WINDOW
    row = lambda b, j: ctx_blocks + b * nqb + j
    prev = lambda j: jnp.maximum(j - 1, 0)
    nxt = lambda j: jnp.minimum(j + 1, nqb - 1)
    kt_spec = lambda f: pl.BlockSpec((A_KVW, WINDOW), lambda b, j: (0, row(b, f(j))))
    v_spec = lambda f: pl.BlockSpec((WINDOW, A_KVW), lambda b, j: (row(b, f(j)), 0))
    same = lambda j: j
    return pl.pallas_call(
        _attn_a_lat_kernel,
        grid=(NB_LAT, nqb),
        in_specs=[pl.BlockSpec(memory_space=pltpu.SMEM),
                  pl.BlockSpec((WINDOW, A_W), lambda b, j: (row(b, j), 0)),
                  kt_spec(prev), kt_spec(same), kt_spec(nxt),
                  v_spec(prev), v_spec(same), v_spec(nxt),
                  pl.BlockSpec((1, A_KV, HD, PAST), lambda b, j: (b, 0, 0, 0)),
                  pl.BlockSpec((1, PAST, A_KVW), lambda b, j: (b, 0, 0))],
        out_specs=pl.BlockSpec((WINDOW, A_W), lambda b, j: (b * nqb + j, 0)),
        out_shape=jax.ShapeDtypeStruct((N_LAT_TOK, A_W), F32),
        compiler_params=_cparams(2),
        name="attn_a_lat",
    )(sink, aq, akt, akt, akt, av, av, av, ktx, vx)


B_TQ = 256


def _attn_b_body(lam_init, q, lam_ref, g_ref, segs):
    lp = lam_ref[...]
    lam = (jnp.exp(jnp.sum(lp[0:1, :] * lp[1:2, :], axis=1, keepdims=True))
           - jnp.exp(jnp.sum(lp[2:3, :] * lp[3:4, :], axis=1, keepdims=True)) + lam_init)
    outs = []
    for h in range(B_HEADS):
        maps = []
        for mi in range(2):
            c0 = h * B_DV + mi * B_DQ
            qm = (q[:, c0:c0 + B_DQ] * (B_DQ ** -0.5)).astype(BF16)
            ss = [jnp.dot(qm, kt(2 * h + mi), preferred_element_type=F32) for kt, _ in segs]
            m = None
            for s in ss:
                sm = jnp.max(s, axis=-1, keepdims=True)
                m = sm if m is None else jnp.maximum(m, sm)
            l = None
            o = None
            for s, (_, v) in zip(ss, segs):
                p = jnp.exp(s - m)
                ps = jnp.sum(p, axis=-1, keepdims=True)
                l = ps if l is None else l + ps
                pv = jnp.dot(p.astype(BF16), v(), preferred_element_type=F32)[:, h * B_DV:(h + 1) * B_DV]
                o = pv if o is None else o + pv
            maps.append(o / l)
        a = maps[0] - lam * maps[1]
        ms = jnp.mean(a * a, axis=-1, keepdims=True)
        outs.append(a * lax.rsqrt(ms + NORM_EPS) * g_ref[...] * (1.0 - lam_init))
    return outs


def _own_keys(kt_ref, v_ref):
    return (lambda hm: kt_ref[hm * B_DQ:(hm + 1) * B_DQ, :], lambda: v_ref[...])


def _attn_b_ctx_kernel(lam_init, lam_ref, g_ref, q_ref, kt_ref, v_ref, o_ref):
    outs = _attn_b_body(lam_init, q_ref[...], lam_ref, g_ref, [_own_keys(kt_ref, v_ref)])
    for h in range(B_HEADS):
        o_ref[:, h * B_DV:(h + 1) * B_DV] = outs[h]


def _attn_b_lat_kernel(lam_init, lam_ref, g_ref, q_ref, kt_ref, v_ref, ktx_ref, vx_ref, o_ref):
    outs = _attn_b_body(lam_init, q_ref[...], lam_ref, g_ref,
                        [_own_keys(kt_ref, v_ref), (lambda hm: ktx_ref[0, hm], lambda: vx_ref[0])])
    for h in range(B_HEADS):
        o_ref[:, h * B_DV:(h + 1) * B_DV] = outs[h]


def _attn_b_ctx_call(lam_init, lam_p, g, bq, bkt, bvb):
    return pl.pallas_call(
        functools.partial(_attn_b_ctx_kernel, lam_init),
        grid=(NB_CTX,),
        in_specs=[pl.BlockSpec((4, B_DQ), lambda b: (0, 0)),
                  pl.BlockSpec((1, B_DV), lambda b: (0, 0)),
                  pl.BlockSpec((T_CTX, B_W), lambda b: (b, 0)),
                  pl.BlockSpec((B_W, T_CTX), lambda b: (0, b)),
                  pl.BlockSpec((T_CTX, B_W), lambda b: (b, 0))],
        out_specs=pl.BlockSpec((T_CTX, B_W), lambda b: (b, 0)),
        out_shape=jax.ShapeDtypeStruct((N_CTX_TOK, B_W), F32),
        compiler_params=_cparams(1),
        name="attn_b_ctx",
    )(lam_p, g, bq, bkt, bvb)


def _attn_b_lat_call(lam_init, lam_p, g, bq, bkt, bvb, ktx, vx):
    nq = T_LAT // B_TQ
    ctx_blocks = N_CTX_TOK // B_TQ
    ctx_seqs = N_CTX_TOK // T_LAT
    return pl.pallas_call(
        functools.partial(_attn_b_lat_kernel, lam_init),
        grid=(NB_LAT, nq),
        in_specs=[pl.BlockSpec((4, B_DQ), lambda b, j: (0, 0)),
                  pl.BlockSpec((1, B_DV), lambda b, j: (0, 0)),
                  pl.BlockSpec((B_TQ, B_W), lambda b, j: (ctx_blocks + b * nq + j, 0)),
                  pl.BlockSpec((B_W, T_LAT), lambda b, j: (0, ctx_seqs + b)),
                  pl.BlockSpec((T_LAT, B_W), lambda b, j: (ctx_seqs + b, 0)),
                  pl.BlockSpec((1, 2 * B_HEADS, B_DQ, PAST), lambda b, j: (b, 0, 0, 0)),
                  pl.BlockSpec((1, PAST, B_W), lambda b, j: (b, 0, 0))],
        out_specs=pl.BlockSpec((B_TQ, B_W), lambda b, j: (b * nq + j, 0)),
        out_shape=jax.ShapeDtypeStruct((N_LAT_TOK, B_W), F32),
        compiler_params=_cparams(2),
        name="attn_b_lat",
    )(lam_p, g, bq, bkt, bvb, ktx, vx)


HALO = 8


def _head_sum(x, ones_bf16):
    hi = x.astype(BF16)
    lo = (x - hi.astype(F32)).astype(BF16)
    return (jnp.dot(hi, ones_bf16, preferred_element_type=F32)
            + jnp.dot(lo, ones_bf16, preferred_element_type=F32))


def _rwkv_prep_kernel(rkv_ref, prev_ref, next_ref, cw_ref, ca_ref,
                      conv_ref, w0_ref, w2_ref, a0_ref, a2_ref, kk_ref, ka_ref, rk_ref, ones_ref,
                      r_ref, nkk_ref, v_ref, w_f, b_f, kd_f, w_b, b_b, kd_b, bonus_ref):
    i = pl.program_id(0)
    li = _lat_tile(i)
    is_ctx = i < N_CTX_TILES
    has_prev = jnp.logical_and(jnp.logical_not(is_ctx), li > 0).astype(F32)
    has_next = jnp.logical_and(jnp.logical_not(is_ctx), li < LAT_TILES_PER_SEQ - 1).astype(F32)
    x = rkv_ref[...]
    row = lax.broadcasted_iota(jnp.int32, x.shape, 0)
    xm = jnp.where(row == 0, prev_ref[HALO - 1:HALO, :] * has_prev, pltpu.roll(x, 1, axis=0))
    xp = jnp.where(row == TM - 1, next_ref[0:1, :] * has_next, pltpu.roll(x, TM - 1, axis=0))
    y = xm * conv_ref[0:1, :] + x * conv_ref[1:2, :] + xp * conv_ref[2:3, :]
    r, k, v = y[:, :C_W], y[:, C_W:2 * C_W], y[:, 2 * C_W:]
    ones = ones_ref[...]

    kk = k * kk_ref[...]
    kk = kk / jnp.maximum(jnp.sqrt(_head_sum(kk * kk, ones)), 1e-12)
    lw = jnp.dot(jnp.tanh(cw_ref[...]).astype(BF16), w2_ref[...],
                 preferred_element_type=F32)
    la = jnp.dot(ca_ref[...].astype(BF16), a2_ref[...], preferred_element_type=F32)
    r_ref[...] = r
    nkk_ref[...] = -kk
    v_ref[...] = v
    bonus = jnp.zeros_like(v)
    for d, (w_o, b_o, kd_o) in enumerate(((w_f, b_f, kd_f), (w_b, b_b, kd_b))):
        sl = slice(d * C_W, (d + 1) * C_W)
        a = _sigmoid(a0_ref[d:d + 1, :] + la[:, sl])
        kd = k * (1.0 + (a - 1.0) * ka_ref[...])
        w_o[...] = -DECAY_SCALE * _sigmoid(w0_ref[d:d + 1, :] + lw[:, sl])
        b_o[...] = kk * a
        kd_o[...] = kd
        bonus = bonus + _head_sum(r * kd * rk_ref[...], ones) * v
    bonus_ref[...] = bonus


def _rwkv_prep_call(rkv, cw, ca, p):
    nh = TM // HALO
    last = N_TOK // HALO - 1
    full = lambda shape: pl.BlockSpec(shape, lambda i: (0,) * len(shape))
    tile = lambda w: pl.BlockSpec((TM, w), lambda i: (i, 0))
    return pl.pallas_call(
        _rwkv_prep_kernel,
        grid=(N_TILES,),
        in_specs=[tile(3 * C_W),
                  pl.BlockSpec((HALO, 3 * C_W), lambda i: (jnp.maximum(i * nh - 1, 0), 0)),
                  pl.BlockSpec((HALO, 3 * C_W), lambda i: (jnp.minimum((i + 1) * nh, last), 0)),
                  tile(2 * C_RANK), tile(2 * C_RANK),
                  full((3, 3 * C_W)), full((2, C_W)), full((2 * C_RANK, 2 * C_W)),
                  full((2, C_W)), full((2 * C_RANK, 2 * C_W)),
                  full((1, C_W)), full((1, C_W)), full((1, C_W)), full((C_W, C_W))],
        out_specs=[tile(C_W)] * 10,
        out_shape=[jax.ShapeDtypeStruct((N_TOK, C_W), F32)] * 10,
        compiler_params=_cparams(1),
        name="rwkv_prep",
    )(rkv, rkv, rkv, cw, ca, p["conv"], p["w0"], p["w2"], p["a0"], p["a2"],
      p["kk"], p["ka"], p["rk"], p["ones"])


N_BLK = TM // LANES
N_PAIR = C_HEADS // 2


def _scan_kernel(r_f, nkk_f, lw_f, b_f, e_f, r_b, nkk_b, lw_b, b_b, e_b,
                 s0f_ref, s0b_ref, ef_ref, eb_ref,
                 yf_ref, yb_ref, sf_ref, sb_ref,
                 st, rows, et, rp, yp):
    first, last = _seq_edges(pl.program_id(0))
    lane = lax.broadcasted_iota(jnp.int32, (TM, LANES), 1)

    @pl.when(first)
    def _():
        st[...] = jnp.zeros_like(st)
        for d, s0 in enumerate((s0f_ref, s0b_ref)):
            for h in range(C_HEADS):
                o = (h % 2) * C_N
                st[d, h, :, o:o + C_N] = s0[0, h]

    for d, srcs in enumerate(((nkk_f, lw_f, b_f), (nkk_b, lw_b, b_b))):
        for a, src in enumerate(srcs):
            for pr in range(N_PAIR):
                chunk = src[:, pr * LANES:(pr + 1) * LANES]
                if a == 1:
                    chunk = jnp.exp(chunk)
                rows[d, a, 2 * pr] = jnp.where(lane < C_N, chunk, 0.0)
                rows[d, a, 2 * pr + 1] = jnp.where(lane < C_N, 0.0, chunk)
    for d, src in enumerate((e_f, e_b)):
        for pr in range(N_PAIR):
            for blk in range(N_BLK):
                tr = src[blk * LANES:(blk + 1) * LANES, pr * LANES:(pr + 1) * LANES].T
                for half in range(2):
                    tile = tr[half * C_N:(half + 1) * C_N, :]
                    swapped = pltpu.roll(tile, C_N, axis=1)
                    for q in range(2):
                        et[d, 2 * pr + half, 2 * blk + q] = tile if q != half else swapped

    for d, src in enumerate((r_f, r_b)):
        for pr in range(N_PAIR):
            rp[d, pr] = src[:, pr * LANES:(pr + 1) * LANES]

    lane1 = lax.broadcasted_iota(jnp.int32, (1, LANES), 1)

    def emit_y(d, t):
        for pr in range(N_PAIR):
            both = jnp.concatenate([st[d, 2 * pr].astype(BF16), st[d, 2 * pr + 1].astype(BF16)], axis=0)
            r8 = jnp.broadcast_to(rp[d, pr, pl.ds(t, 1), :], (8, LANES)).astype(BF16)
            y8 = lax.dot_general(r8, both, (((1,), (1,)), ((), ())), preferred_element_type=F32)
            yp[d, pr, pl.ds(t, 1), :] = y8[0:1, :]

    def step(i, carry):
        for d in range(2):
            t = i if d == 0 else TM - 1 - i
            emit_y(d, jnp.maximum(i - 1, 0) if d == 0 else jnp.minimum(TM - i, TM - 1))
            blk = t // C_N
            pick = (lane1 == (t % C_N) + C_N, lane1 == (t % C_N))
            for h in range(C_HEADS):
                s = st[d, h]
                nkk = rows[d, 0, h, pl.ds(t, 1), :]
                w = rows[d, 1, h, pl.ds(t, 1), :]
                b = rows[d, 2, h, pl.ds(t, 1), :]
                sa = jnp.sum(jnp.where(pick[h % 2], et[d, h, blk], s * nkk), axis=1, keepdims=True)
                st[d, h] = s * w + sa * b
        return carry

    lax.fori_loop(0, TM, step, 0, unroll=32)
    emit_y(0, TM - 1)
    emit_y(1, 0)

    for d, y_ref in enumerate((yf_ref, yb_ref)):
        for pr in range(N_PAIR):
            y_ref[:, pr * LANES:(pr + 1) * LANES] = yp[d, pr]

    @pl.when(last)
    def _():
        for d, (s_out, e_fin) in enumerate(((sf_ref, ef_ref), (sb_ref, eb_ref))):
            for h in range(C_HEADS):
                o = (h % 2) * C_N
                s_out[0, h] = st[d, h, :, o:o + C_N] + e_fin[0, h]


N_SEQ = NB_CTX + NB_LAT


def _lat_seq(i):
    return jnp.maximum(i - N_CTX_TILES, 0) // LAT_TILES_PER_SEQ


def _seq_of_tile(i):
    return jnp.where(i < N_CTX_TILES, i, NB_CTX + _lat_seq(i))


def _bwd_tile(i):
    rev = N_CTX_TILES + _lat_seq(i) * LAT_TILES_PER_SEQ + (LAT_TILES_PER_SEQ - 1 - _lat_tile(i))
    return jnp.where(i < N_CTX_TILES, i, rev)


def _seq_edges(i):
    is_ctx = i < N_CTX_TILES
    return (jnp.logical_or(is_ctx, _lat_tile(i) == 0),
            jnp.logical_or(is_ctx, _lat_tile(i) == LAT_TILES_PER_SEQ - 1))


_FWD_ROWS = pl.BlockSpec((TM, C_W), lambda i: (i, 0))
_BWD_ROWS = pl.BlockSpec((TM, C_W), lambda i: (_bwd_tile(i), 0))
_SEQ_STATE = pl.BlockSpec((1, C_HEADS, C_N, C_N), lambda i: (_seq_of_tile(i), 0, 0, 0))


def _scan_call(ins, s0f, s0b, e_fin_f, e_fin_b):
    r, nkk, lw_f, b_f, e_f, lw_b, b_b, e_b = ins
    return pl.pallas_call(
        _scan_kernel,
        grid=(N_TILES,),
        in_specs=[_FWD_ROWS] * 5 + [_BWD_ROWS] * 5 + [_SEQ_STATE] * 4,
        out_specs=[_FWD_ROWS, _BWD_ROWS, _SEQ_STATE, _SEQ_STATE],
        out_shape=[jax.ShapeDtypeStruct((N_TOK, C_W), F32)] * 2
                  + [jax.ShapeDtypeStruct((N_SEQ, C_HEADS, C_N, C_N), F32)] * 2,
        scratch_shapes=[pltpu.VMEM((2, C_HEADS, C_N, LANES), F32),
                        pltpu.VMEM((2, 3, C_HEADS, TM, LANES), F32),
                        pltpu.VMEM((2, C_HEADS, TM // C_N, C_N, LANES), F32),
                        pltpu.VMEM((2, N_PAIR, TM, LANES), F32),
                        pltpu.VMEM((2, N_PAIR, TM, LANES), F32)],
        compiler_params=_cparams(1),
        name="scan",
    )(r, nkk, lw_f, b_f, e_f, r, nkk, lw_b, b_b, e_b, s0f, s0b, e_fin_f, e_fin_b)


SUB = C_N


def _cum_rows(tri_bf16, x):
    hi = x.astype(BF16)
    lo = (x - hi.astype(F32)).astype(BF16)
    return (jnp.dot(tri_bf16, hi, preferred_element_type=F32)
            + jnp.dot(tri_bf16, lo, preferred_element_type=F32))


_NT = (((1,), (1,)), ((), ()))
_TN = (((0,), (0,)), ((), ()))


def _decay_kernel(r_f, nkk_f, v_f, lw_f, kd_f, r_b, nkk_b, v_b, lw_b, kd_b,
                  ef_ref, yef_ref, eb_ref, yeb_ref, stf_ref, stb_ref, est):
    first, last = _seq_edges(pl.program_id(0))

    @pl.when(first)
    def _():
        est[...] = jnp.zeros_like(est)

    ti = lax.broadcasted_iota(jnp.int32, (SUB, SUB), 0)
    si = lax.broadcasted_iota(jnp.int32, (SUB, SUB), 1)
    dirs = ((r_f, nkk_f, v_f, lw_f, kd_f, ef_ref, yef_ref), (r_b, nkk_b, v_b, lw_b, kd_b, eb_ref, yeb_ref))
    for d, (r_ref, nkk_ref, v_ref, lw_ref, kd_ref, e_out, ye_out) in enumerate(dirs):
        upto = (si <= ti) if d == 0 else (si >= ti)
        before = (si < ti) if d == 0 else (si > ti)
        both_masks = jnp.concatenate([before, upto], axis=0)
        tri = upto.astype(BF16)
        end_row = SUB - 1 if d == 0 else 0
        chunks = range(TM // SUB) if d == 0 else range(TM // SUB - 1, -1, -1)
        for c in chunks:
            rs = slice(c * SUB, (c + 1) * SUB)
            lw = lw_ref[rs, :]
            lg = _cum_rows(tri, lw)
            abar = nkk_ref[rs, :] * jnp.exp(lg - lw)
            rbar = r_ref[rs, :] * jnp.exp(lg)
            kbar = kd_ref[rs, :] * jnp.exp(-lg)
            g_end = jnp.exp(lg[end_row:end_row + 1, :])
            v = v_ref[rs, :]
            for h in range(C_HEADS):
                hs = slice(h * C_N, (h + 1) * C_N)
                lhs = jnp.concatenate([abar[:, hs], rbar[:, hs]], axis=0).astype(BF16)
                kh = kbar[:, hs].astype(BF16)
                vh = v[:, hs].astype(BF16)
                eh = est[d, h]
                sc = lax.dot_general(lhs, kh, _NT, preferred_element_type=F32)
                sc = jnp.where(both_masks, sc, 0.0).astype(BF16)
                out = (jnp.dot(sc, vh, preferred_element_type=F32)
                       + lax.dot_general(lhs, eh.astype(BF16), _NT, preferred_element_type=F32))
                e_out[rs, hs] = out[0:SUB, :]
                ye_out[rs, hs] = out[SUB:2 * SUB, :]
                est[d, h] = (eh + lax.dot_general(vh, kh, _TN, preferred_element_type=F32)) * g_end[:, hs]

    @pl.when(last)
    def _():
        stf_ref[0] = est[0]
        stb_ref[0] = est[1]


def _decay_call(ins):
    r, nkk, v, lw_f, kd_f, lw_b, kd_b = ins
    return pl.pallas_call(
        _decay_kernel,
        grid=(N_TILES,),
        in_specs=[_FWD_ROWS] * 5 + [_BWD_ROWS] * 5,
        out_specs=[_FWD_ROWS, _FWD_ROWS, _BWD_ROWS, _BWD_ROWS, _SEQ_STATE, _SEQ_STATE],
        out_shape=[jax.ShapeDtypeStruct((N_TOK, C_W), F32)] * 4
                  + [jax.ShapeDtypeStruct((N_SEQ, C_HEADS, C_N, C_N), F32)] * 2,
        scratch_shapes=[pltpu.VMEM((2, C_HEADS, C_N, C_N), F32)],
        compiler_params=_cparams(1),
        name="decay",
    )(r, nkk, v, lw_f, kd_f, r, nkk, v, lw_b, kd_b)


def _outproj_kernel(x_ref, mod_ref, aoc_ref, aol_ref, boc_ref, bol_ref,
                    yf_ref, yb_ref, yef_ref, yeb_ref, bonus_ref, cg_ref,
                    lng_ref, lnb_ref, g2_ref, mean_ref, wa_ref, wb_ref, wc_ref, o_ref):
    is_ctx = pl.program_id(0) < N_CTX_TILES
    ao = jnp.where(is_ctx, aoc_ref[...], aol_ref[...])
    bo = jnp.where(is_ctx, boc_ref[...], bol_ref[...])
    y = (yf_ref[...] + yef_ref[...]) + (yb_ref[...] + yeb_ref[...]) + bonus_ref[...]
    ones = mean_ref[...]
    mu = _head_sum(y, ones) * (1.0 / C_N)
    dy = y - mu
    var = _head_sum(dy * dy, ones) * (1.0 / C_N)
    yn = dy * lax.rsqrt(var + GN_EPS) * lng_ref[...] + lnb_ref[...]
    gate = jnp.dot(_sigmoid(cg_ref[...]).astype(BF16), g2_ref[...], preferred_element_type=F32)
    co = yn * gate
    m = (jnp.dot(ao.astype(BF16), wa_ref[...], preferred_element_type=F32)
         + jnp.dot(bo.astype(BF16), wb_ref[...], preferred_element_type=F32)
         + jnp.dot(co.astype(BF16), wc_ref[...], preferred_element_type=F32))
    o_ref[...] = x_ref[...] + mod_ref[0, 2:3, :] * m


def _outproj_call(x, mod_l, ao_ctx, ao_lat, bo_ctx, bo_lat, yf, yb, yef, yeb, bonus, cg, p):
    full = lambda shape: pl.BlockSpec(shape, lambda i: (0,) * len(shape))
    tile = lambda w: pl.BlockSpec((TM, w), lambda i: (i, 0))
    ctx_tile = lambda w: pl.BlockSpec((TM, w), lambda i: (jnp.minimum(i, N_CTX_TILES - 1), 0))
    lat_tile = lambda w: pl.BlockSpec((TM, w), lambda i: (jnp.maximum(i - N_CTX_TILES, 0), 0))
    return pl.pallas_call(
        _outproj_kernel,
        grid=(N_TILES,),
        in_specs=[tile(D_MODEL),
                  pl.BlockSpec((1, 8, D_MODEL), lambda i: (_tile_group(i), 0, 0)),
                  ctx_tile(A_W), lat_tile(A_W), ctx_tile(B_W), lat_tile(B_W),
                  tile(C_W), tile(C_W), tile(C_W), tile(C_W), tile(C_W),
                  tile(C_G_RANK),
                  full((1, C_W)), full((1, C_W)), full((C_G_RANK, C_W)), full((C_W, C_W)),
                  full((A_W, D_MODEL)), full((B_W, D_MODEL)), full((C_W, D_MODEL))],
        out_specs=tile(D_MODEL),
        out_shape=jax.ShapeDtypeStruct((N_TOK, D_MODEL), F32),
        compiler_params=_cparams(1),
        name="outproj",
    )(x, mod_l, ao_ctx, ao_lat, bo_ctx, bo_lat, yf, yb, yef, yeb, bonus, cg,
      p["lnx_g"], p["lnx_b"], p["g2"], p["mean"],
      p["wa"], p["wb"], p["wc"])


def _ffn_kernel(final, x_ref, mod_ref, g_ref, w1_ref, w3_ref, w2_ref, fg_ref, o_ref):
    x = x_ref[...]
    h = _modulated_norm(x, g_ref[...], mod_ref[0, 3:4, :], mod_ref[0, 4:5, :]).astype(BF16)
    u = jnp.dot(h, w1_ref[...], preferred_element_type=F32)
    t = jnp.dot(h, w3_ref[...], preferred_element_type=F32)
    act = (u * _sigmoid(u) * t).astype(BF16)
    y = x + mod_ref[0, 5:6, :] * jnp.dot(act, w2_ref[...], preferred_element_type=F32)
    if final:
        ms = jnp.mean(y * y, axis=-1, keepdims=True)
        y = y * lax.rsqrt(ms + NORM_EPS) * fg_ref[...]
    o_ref[...] = y


def _ffn_call(final, x, mod_l, g, w1, w3, w2, final_g):
    once = lambda shape: pl.BlockSpec(shape, lambda i: (0,) * len(shape),
                                      pipeline_mode=pl.Buffered(1))
    return pl.pallas_call(
        functools.partial(_ffn_kernel, final),
        grid=(N_TILES,),
        in_specs=[pl.BlockSpec((TM, D_MODEL), lambda i: (i, 0)),
                  pl.BlockSpec((1, 8, D_MODEL), lambda i: (_tile_group(i), 0, 0)),
                  once((1, D_MODEL)),
                  once((D_MODEL, D_FF)), once((D_MODEL, D_FF)), once((D_FF, D_MODEL)),
                  once((1, D_MODEL))],
        out_specs=pl.BlockSpec((TM, D_MODEL), lambda i: (i, 0)),
        out_shape=jax.ShapeDtypeStruct((N_TOK, D_MODEL), F32),
        compiler_params=_cparams(1),
        name="ffn",
    )(x, mod_l, g, w1, w3, w2, final_g)


def _block_diag2(m):
    z = jnp.zeros_like(m[0])
    return jnp.concatenate([jnp.concatenate([m[0], z], axis=1),
                            jnp.concatenate([z, m[1]], axis=1)], axis=0)


def _keys_t(k, nb, t, heads, dim):
    return k.reshape(nb, t, heads, dim).transpose(0, 2, 3, 1).astype(BF16)


def kernel(x_prompt, x_sample, cache_a_k, cache_a_v, cache_b_k, cache_b_v, state_c_fwd, state_c_bwd,
           c, c_ctx, ada_w, ada_b, norm1_g, norm2_g, w_in, a_sink, b_lambda, b_subln_g,
           c_conv, c_w0, c_w2, c_a0, c_a2, c_g2, c_kk, c_ka, c_rk, c_lnx_g, c_lnx_b,
           w_out, ffn_w1, ffn_w3, ffn_w2, final_g):
    x = jnp.concatenate([x_prompt.reshape(N_CTX_TOK, D_MODEL),
                         x_sample.reshape(N_LAT_TOK, D_MODEL)], axis=0)
    cvec8 = jnp.concatenate([c_ctx[None], c, jnp.zeros((8 - 1 - NB_LAT, D_MODEL), F32)], axis=0)
    mod = _ada_call(cvec8, ada_w, ada_b)
    mod = mod[:, :1 + NB_LAT].reshape(DEPTH, 1 + NB_LAT, 6, D_MODEL)
    mod = jnp.pad(mod, ((0, 0), (0, 0), (0, 2), (0, 0)))

    tabs = _rope_tables()
    head_id = np.arange(C_W) // C_N
    block_ones = jnp.asarray(head_id[:, None] == head_id[None, :], BF16)
    final_g2 = final_g.reshape(1, D_MODEL)
    s0_ctx = jnp.zeros((NB_CTX, C_HEADS, C_N, C_N), F32)

    new_ak, new_av, new_bk, new_bv, new_sf, new_sb = [], [], [], [], [], []
    for l in range(DEPTH):
        lam_init = 0.8 - 0.6 * math.exp(-0.3 * l)
        aq, ak, av, bq, bk, bv, rkv, cw, ca, cg, akt, bkt, bvb = _inproj_call(
            x, mod[l], norm1_g[l].reshape(1, D_MODEL), w_in[l].astype(BF16), tabs)

        new_ak.append(ak[:N_CTX_TOK].reshape(NB_CTX, T_CTX, A_KV, HD))
        new_av.append(av[:N_CTX_TOK].reshape(NB_CTX, T_CTX, A_KV, HD))
        new_bk.append(bk[:N_CTX_TOK].reshape(NB_CTX, T_CTX, B_HEADS, 2, B_DQ))
        new_bv.append(bv[:N_CTX_TOK].reshape(NB_CTX, T_CTX, B_HEADS, B_DV))

        sink = a_sink[l]
        ao_ctx = _attn_a_ctx_call(sink, aq, akt, av)
        ao_lat = _attn_a_lat_call(
            sink, aq, akt, av,
            _keys_t(cache_a_k[:, l].reshape(NB_LAT * PAST, A_KVW), NB_LAT, PAST, A_KV, HD),
            cache_a_v[:, l].reshape(NB_LAT, PAST, A_KVW))

        lam_p, sub_g = b_lambda[l], b_subln_g[l].reshape(1, B_DV)
        bo_ctx = _attn_b_ctx_call(lam_init, lam_p, sub_g, bq, bkt, bvb)
        bo_lat = _attn_b_lat_call(
            lam_init, lam_p, sub_g, bq, bkt, bvb,
            _keys_t(cache_b_k[:, l].reshape(NB_LAT * PAST, B_W), NB_LAT, PAST, 2 * B_HEADS, B_DQ),
            cache_b_v[:, l].reshape(NB_LAT, PAST, B_W).astype(BF16))

        prep = _rwkv_prep_call(rkv, cw, ca, dict(
            conv=c_conv[l], w0=c_w0[l], w2=_block_diag2(c_w2[l]).astype(BF16),
            a0=c_a0[l], a2=_block_diag2(c_a2[l]).astype(BF16),
            kk=c_kk[l].reshape(1, C_W), ka=c_ka[l].reshape(1, C_W), rk=c_rk[l].reshape(1, C_W),
            ones=block_ones))
        r, nkk, v, lw_f, b_f, kd_f, lw_b, b_b, kd_b, bonus = prep
        e_f, yef, e_b, yeb, e_fin_f, e_fin_b = _decay_call((r, nkk, v, lw_f, kd_f, lw_b, kd_b))
        yf, yb, sf, sb = _scan_call(
            (r, nkk, lw_f, b_f, e_f, lw_b, b_b, e_b),
            jnp.concatenate([s0_ctx, state_c_fwd[:, l]], axis=0),
            jnp.concatenate([s0_ctx, state_c_bwd[:, l]], axis=0), e_fin_f, e_fin_b)
        new_sf.append(sf[:NB_CTX])
        new_sb.append(sb[:NB_CTX])

        wo = w_out[l].astype(BF16)
        x = _outproj_call(x, mod[l], ao_ctx, ao_lat, bo_ctx, bo_lat, yf, yb, yef, yeb, bonus, cg, dict(
            lnx_g=c_lnx_g[l].reshape(1, C_W), lnx_b=c_lnx_b[l].reshape(1, C_W),
            g2=c_g2[l].astype(BF16), mean=block_ones,
            wa=wo[:A_W], wb=wo[A_W:A_W + B_W], wc=wo[A_W + B_W:]))

        x = _ffn_call(l == DEPTH - 1, x, mod[l], norm2_g[l].reshape(1, D_MODEL),
                      ffn_w1[l].astype(BF16), ffn_w3[l].astype(BF16), ffn_w2[l].astype(BF16), final_g2)

    y_prompt = x[:N_CTX_TOK].reshape(NB_CTX, T_CTX, D_MODEL)
    y_sample = x[N_CTX_TOK:].reshape(NB_LAT, T_LAT, D_MODEL)
    return (y_prompt, y_sample,
            jnp.stack(new_ak, axis=1), jnp.stack(new_av, axis=1),
            jnp.stack(new_bk, axis=1), jnp.stack(new_bv, axis=1),
            jnp.stack(new_sf, axis=1), jnp.stack(new_sb, axis=1))
```
